```python
import math
import jax, jax.numpy as jnp
from jax import lax
import numpy as np

D_MODEL = 2048
BATCH = 4
SEQ = 2048
DEPTH = 2

PLE_DIM = 256
W_CONF = 512
W_GLA = 512
W_DIL = 512
W_SC = 512
CONF_KERNEL = 31
GLA_HEADS = 4
GLA_DV = W_GLA // GLA_HEADS
GLA_DK = GLA_DV // 2
GLA_RANK = 16
GLA_TAU = 16.0
GLA_CHUNK = 32
DIL_HEADS = 8
DIL_HD = W_DIL // DIL_HEADS
DIL_BRANCHES = ((128, 1), (512, 4), (2048, 16))
DIL_BLOCK = 128
REL_BUCKETS = 32
REL_MAX_DIST = 2048
SC_KERNEL = 3
N_GROUPS = 4
EXPERTS_PER_GROUP = 8
N_EXPERTS = N_GROUPS * EXPERTS_PER_GROUP
TOP_K = 2
D_EXPERT = 256
DN_ALPHA = (2 * DEPTH) ** 0.25
DN_BETA = (8 * DEPTH) ** -0.25
LN_EPS = 1e-5
RMS_EPS = 1e-6
IN_SIZES = (2 * W_CONF,
            GLA_HEADS * GLA_DK, GLA_HEADS * GLA_DK, W_GLA, GLA_RANK, W_GLA,
            W_DIL, W_DIL, W_DIL,
            W_SC, W_SC, W_SC)
D_IN = 2 * W_CONF + 2 * GLA_HEADS * GLA_DK + 2 * W_GLA + GLA_RANK + 3 * W_DIL + 3 * W_SC

kernel_name = 'hybrid_parallel_mixers_hmoe_deepnorm'


def layer_norm(x, g, b):
    xf = x.astype(jnp.float32)
    mu = jnp.mean(xf, axis=-1, keepdims=True)
    var = jnp.mean(jnp.square(xf - mu), axis=-1, keepdims=True)
    return ((xf - mu) * lax.rsqrt(var + LN_EPS)).astype(x.dtype) * g + b


def causal_dwconv(x, w):
    K, C = w.shape
    return lax.conv_general_dilated(x, w[:, None, :], window_strides=(1,), padding=[(K - 1, 0)],
                                    dimension_numbers=('NWC', 'WIO', 'NWC'), feature_group_count=C)


def conformer_conv(u, dw_w, dw_b, ln_g, ln_b):
    a, gate = jnp.split(u, 2, axis=-1)
    h = a * jax.nn.sigmoid(gate)
    h = causal_dwconv(h, dw_w) + dw_b
    h = layer_norm(h, ln_g, ln_b)
    return jax.nn.silu(h)


def gla_mixer(q, k, v, g_low, r, w_g2, b_g2, norm_g):
    Bsz, S, _ = q.shape
    H, dk, dv, C = GLA_HEADS, GLA_DK, GLA_DV, GLA_CHUNK
    n = S // C
    f32 = jnp.float32
    log_a = jax.nn.log_sigmoid((g_low @ w_g2 + b_g2).astype(f32)) / GLA_TAU

    def chunked(t, d):
        return t.astype(f32).reshape(Bsz, n, C, H, d).transpose(0, 3, 1, 2, 4)

    qc = chunked(q, dk) * (dk ** -0.5)
    kc = chunked(k, dk)
    vc = chunked(v, dv)
    b = jnp.cumsum(chunked(log_a, dk), axis=3)
    causal = jnp.tril(jnp.ones((C, C), dtype=bool))
    rel = jnp.where(causal[:, :, None], b[:, :, :, :, None, :] - b[:, :, :, None, :, :], -jnp.inf)
    scores = jnp.sum(qc[:, :, :, :, None, :] * kc[:, :, :, None, :, :] * jnp.exp(rel), axis=-1)
    o_intra = jnp.einsum('bhnij,bhnjv->bhniv', scores, vc)
    b_last = b[:, :, :, -1:, :]
    upd = jnp.einsum('bhnjk,bhnjv->bhnkv', kc * jnp.exp(b_last - b), vc)
    decay = jnp.exp(b_last[:, :, :, 0, :])

    def step(state, inp):
        dec, u = inp
        return state * dec[..., None] + u, state

    s0 = jnp.zeros((Bsz, H, dk, dv), f32)
    _, s_prev = lax.scan(step, s0, (jnp.moveaxis(decay, 2, 0), jnp.moveaxis(upd, 2, 0)))
    s_prev = jnp.moveaxis(s_prev, 0, 2)
    o_inter = jnp.einsum('bhnik,bhnkv->bhniv', qc * jnp.exp(b), s_prev)
    o = (o_intra + o_inter).transpose(0, 2, 3, 1, 4).reshape(Bsz, S, H, dv)
    o = o * lax.rsqrt(jnp.mean(o * o, axis=-1, keepdims=True) + RMS_EPS)
    o = (o.astype(v.dtype) * norm_g).reshape(Bsz, S, H * dv)
    return o * jax.nn.silu(r)


def t5_bucket(dist):
    max_exact = REL_BUCKETS // 2
    large = max_exact + (jnp.log(jnp.maximum(dist, 1).astype(jnp.float32) / max_exact)
                         / math.log(REL_MAX_DIST / max_exact) * (REL_BUCKETS - max_exact)).astype(jnp.int32)
    large = jnp.minimum(large, REL_BUCKETS - 1)
    return jnp.where(dist < max_exact, dist, large)


def dilated_branch(q, k, v, rel_bias, window, dil):
    Bsz, S, H, Dh = q.shape
    L = S // dil
    nb = -(-L // DIL_BLOCK)
    Lp = nb * DIL_BLOCK
    span = window // dil
    f32 = jnp.float32

    def by_stride(t):
        t = t.reshape(Bsz, L, dil, H, Dh).transpose(0, 2, 3, 1, 4)
        t = jnp.pad(t, ((0, 0), (0, 0), (0, 0), (0, Lp - L), (0, 0)))
        return t.reshape(Bsz, dil, H, nb, DIL_BLOCK, Dh)

    def band(t):
        prev = jnp.pad(t, ((0, 0), (0, 0), (0, 0), (1, 0), (0, 0), (0, 0)))[:, :, :, :-1]
        return jnp.concatenate([prev, t], axis=4)

    qs = by_stride(q)
    kb = band(by_stride(k))
    vb = band(by_stride(v))
    qi = jnp.arange(DIL_BLOCK)[:, None]
    kj = jnp.arange(2 * DIL_BLOCK)[None, :]
    steps = qi + DIL_BLOCK - kj
    valid = ((steps >= 0) & (steps <= span))[None] & ((jnp.arange(nb)[:, None, None] > 0) | (kj >= DIL_BLOCK)[None])
    bias = rel_bias[t5_bucket(jnp.maximum(steps, 0) * dil)].transpose(2, 0, 1).astype(f32)
    s = jnp.einsum('brhnqe,brhnke->brhnqk', qs, kb).astype(f32) * (Dh ** -0.5) + bias[None, None, :, None]
    s = jnp.where(valid[None, None, None], s, -jnp.inf)
    m = jnp.max(s, axis=-1, keepdims=True)
    pexp = jnp.exp(s - m)
    l = jnp.sum(pexp, axis=-1, keepdims=True)
    o = jnp.einsum('brhnqk,brhnke->brhnqe', pexp, vb.astype(f32)) / l
    lse = (m + jnp.log(l))[..., 0]
    o = o.reshape(Bsz, dil, H, Lp, Dh)[:, :, :, :L].transpose(0, 3, 1, 2, 4).reshape(Bsz, S, H, Dh)
    lse = lse.reshape(Bsz, dil, H, Lp)[:, :, :, :L].transpose(0, 3, 1, 2).reshape(Bsz, S, H)
    return o, lse


def dilated_mixer(q, k, v, rel_bias):
    Bsz, S, _ = q.shape
    q = q.reshape(Bsz, S, DIL_HEADS, DIL_HD)
    k = k.reshape(Bsz, S, DIL_HEADS, DIL_HD)
    v = v.reshape(Bsz, S, DIL_HEADS, DIL_HD)
    outs, lses = [], []
    for window, dil in DIL_BRANCHES:
        o, lse = dilated_branch(q, k, v, rel_bias, window, dil)
        outs.append(o)
        lses.append(lse)
    w = jax.nn.softmax(jnp.stack(lses, axis=0), axis=0)
    o = jnp.sum(w[..., None] * jnp.stack(outs, axis=0), axis=0)
    return o.reshape(Bsz, S, W_DIL).astype(v.dtype)


def short_gated_conv(b_gate, c_gate, h, conv_w):
    return b_gate * causal_dwconv(c_gate * h, conv_w)


def hier_moe(x, rg_w, rg_b, re_w, re_b, w_gate, w_up, w_down):
    Bsz, S, D = x.shape
    t = x.reshape(-1, D)
    f32 = jnp.float32
    g_prob = jax.nn.softmax((t @ rg_w + rg_b).astype(f32), axis=-1)
    g_top, g_idx = lax.top_k(g_prob, 1)
    e_logits = (t @ re_w + re_b).astype(f32).reshape(-1, N_GROUPS, EXPERTS_PER_GROUP)
    e_logits = jnp.take_along_axis(e_logits, g_idx[:, :, None], axis=1)[:, 0]
    e_top, e_idx = lax.top_k(jax.nn.softmax(e_logits, axis=-1), TOP_K)
    gate = g_top * (e_top / jnp.sum(e_top, axis=-1, keepdims=True))
    expert = g_idx * EXPERTS_PER_GROUP + e_idx
    combine = jnp.sum(jax.nn.one_hot(expert, N_EXPERTS, dtype=f32) * gate[..., None], axis=1)
    h = jax.nn.silu(jnp.einsum('td,edf->tef', t, w_gate)) * jnp.einsum('td,edf->tef', t, w_up)
    h = h * combine.astype(h.dtype)[:, :, None]
    y = jnp.einsum('tef,efd->td', h, w_down)
    return y.reshape(Bsz, S, D)


def setup_inputs(seed: int = 0) -> dict:
    key = jax.random.key(seed)
    ks = iter(jax.random.split(key, 32))

    def nrm(shape, scale):
        return jax.random.normal(next(ks), shape, jnp.float32) * scale

    L = DEPTH
    return {
        'x': nrm((BATCH, SEQ, D_MODEL), 1.0),
        'p': nrm((DEPTH, BATCH, SEQ, PLE_DIM), 1.0),
        'w_in': nrm((L, D_MODEL, D_IN), D_MODEL ** -0.5),
        'conf_dw_w': nrm((L, CONF_KERNEL, W_CONF), CONF_KERNEL ** -0.5),
        'conf_dw_b': nrm((L, W_CONF), 0.02),
        'conf_ln_g': 1.0 + nrm((L, W_CONF), 0.02),
        'conf_ln_b': nrm((L, W_CONF), 0.02),
        'gla_w_g2': nrm((L, GLA_RANK, GLA_HEADS * GLA_DK), GLA_RANK ** -0.5),
        'gla_b_g2': nrm((L, GLA_HEADS * GLA_DK), 0.1),
        'gla_norm_g': 1.0 + nrm((L, GLA_DV), 0.02),
        'sc_conv_w': nrm((L, SC_KERNEL, W_SC), SC_KERNEL ** -0.5),
        'mix_scale': 1.0 + nrm((L, D_MODEL), 0.02),
        'w_out': nrm((L, D_MODEL, D_MODEL), DN_BETA * D_MODEL ** -0.5),
        'rel_bias': nrm((REL_BUCKETS, DIL_HEADS), 0.5),
        'ln1_g': 1.0 + nrm((L, D_MODEL), 0.02),
        'ln1_b': nrm((L, D_MODEL), 0.02),
        'router_g_w': nrm((L, D_MODEL, N_GROUPS), D_MODEL ** -0.5),
        'router_g_b': nrm((L, N_GROUPS), 0.01),
        'router_e_w': nrm((L, D_MODEL, N_EXPERTS), D_MODEL ** -0.5),
        'router_e_b': nrm((L, N_EXPERTS), 0.01),
        'exp_w_gate': nrm((L, N_EXPERTS, D_MODEL, D_EXPERT), D_MODEL ** -0.5),
        'exp_w_up': nrm((L, N_EXPERTS, D_MODEL, D_EXPERT), D_MODEL ** -0.5),
        'exp_w_down': nrm((L, N_EXPERTS, D_EXPERT, D_MODEL), DN_BETA * D_EXPERT ** -0.5),
        'ple_w_gate': nrm((L, D_MODEL, D_MODEL), D_MODEL ** -0.5),
        'ple_b_gate': nrm((L, D_MODEL), 0.02),
        'ple_w_proj': nrm((L, PLE_DIM, D_MODEL), DN_BETA * PLE_DIM ** -0.5),
        'ln2_g': 1.0 + nrm((L, D_MODEL), 0.02),
        'ln2_b': nrm((L, D_MODEL), 0.02),
    }


def reference(x, p, w_in, conf_dw_w, conf_dw_b, conf_ln_g, conf_ln_b, gla_w_g2, gla_b_g2, gla_norm_g,
              sc_conv_w, mix_scale, w_out, rel_bias, ln1_g, ln1_b, router_g_w, router_g_b, router_e_w,
              router_e_b, exp_w_gate, exp_w_up, exp_w_down, ple_w_gate, ple_b_gate, ple_w_proj, ln2_g, ln2_b):
    split_idx = np.cumsum(IN_SIZES)[:-1].tolist()
    for i in range(DEPTH):
        u = x @ w_in[i]
        a_u, gq, gk, gv, g_low, g_r, cq, ck, cv, sb, sc, sh = jnp.split(u, split_idx, axis=-1)
        y_a = conformer_conv(a_u, conf_dw_w[i], conf_dw_b[i], conf_ln_g[i], conf_ln_b[i])
        y_b = gla_mixer(gq, gk, gv, g_low, g_r, gla_w_g2[i], gla_b_g2[i], gla_norm_g[i])
        y_c = dilated_mixer(cq, ck, cv, rel_bias)
        y_d = short_gated_conv(sb, sc, sh, sc_conv_w[i])
        mix = jnp.concatenate([y_a, y_b, y_c, y_d], axis=-1) * mix_scale[i]
        x = layer_norm(DN_ALPHA * x + mix @ w_out[i], ln1_g[i], ln1_b[i])
        ffn = hier_moe(x, router_g_w[i], router_g_b[i], router_e_w[i], router_e_b[i],
                       exp_w_gate[i], exp_w_up[i], exp_w_down[i])
        ple = jax.nn.sigmoid(x @ ple_w_gate[i] + ple_b_gate[i]) * (p[i] @ ple_w_proj[i])
        x = layer_norm(DN_ALPHA * x + ffn + ple, ln2_g[i], ln2_b[i])
    return x
```

```python
import functools
import math

import jax
import jax.numpy as jnp
from jax import lax
from jax.experimental import pallas as pl
from jax.experimental.pallas import tpu as pltpu

F32 = jnp.float32
BF16 = jnp.bfloat16
I32 = jnp.int32
HIGHEST = lax.Precision.HIGHEST

D_MODEL = 2048
DEPTH = 2
PLE_DIM = 256
W_CONF = 512
W_GLA = 512
W_DIL = 512
W_SC = 512
CONF_KERNEL = 31
GLA_HEADS = 4
GLA_DV = 128
GLA_DK = 64
GLA_RANK = 16
GLA_TAU = 16.0
DIL_HEADS = 8
DIL_HD = 64
DIL_BRANCHES = ((128, 1), (512, 4), (2048, 16))
DIL_BLOCK = 128
REL_BUCKETS = 32
REL_MAX_DIST = 2048
SC_KERNEL = 3
N_GROUPS = 4
EXPERTS_PER_GROUP = 8
N_EXPERTS = 32
TOP_K = 2
D_EXPERT = 256
DN_ALPHA = (2 * DEPTH) ** 0.25
LN_EPS = 1e-5
RMS_EPS = 1e-6
IN_SIZES = (2 * W_CONF, GLA_HEADS * GLA_DK, GLA_HEADS * GLA_DK, W_GLA, GLA_RANK, W_GLA,
            W_DIL, W_DIL, W_DIL, W_SC, W_SC, W_SC)

LANES = 128
SUBLANES = 8
VMEM_LIMIT = 56 * 1024 * 1024

COL_A = 0
COL_GV = 1024
COL_GR = 1536
COL_CQ = 2048
COL_CK = 2560
COL_CV = 3072
COL_SB = 3584
COL_SC = 4096
COL_SH = 4608
COL_GQ = 5120
COL_GK = 5376
COL_GL = 5632
U_COLS = 5760

MOE_TILE = 256


def _cparams(n_axes):
    return pltpu.CompilerParams(dimension_semantics=("arbitrary",) * n_axes, vmem_limit_bytes=VMEM_LIMIT)


def _layer_norm(x, g, b):
    mu = jnp.mean(x, axis=-1, keepdims=True)
    xc = x - mu
    var = jnp.mean(xc * xc, axis=-1, keepdims=True)
    return xc * lax.rsqrt(var + LN_EPS) * g + b


def _sigmoid(x):
    return 1.0 / (1.0 + jnp.exp(-x))


def _silu(x):
    return x * _sigmoid(x)


def _matmul_kernel(x_ref, w_ref, o_ref):
    o_ref[...] = jnp.dot(x_ref[...].astype(BF16), w_ref[...].astype(BF16),
                         preferred_element_type=F32).astype(o_ref.dtype)


def _matmul(x, w, tm, tn, out_dtype):
    m, k = x.shape
    n = w.shape[1]
    return pl.pallas_call(
        _matmul_kernel,
        grid=(n // tn, m // tm),
        in_specs=[pl.BlockSpec((tm, k), lambda j, i: (i, 0)),
                  pl.BlockSpec((k, tn), lambda j, i: (0, j))],
        out_specs=pl.BlockSpec((tm, tn), lambda j, i: (i, j)),
        out_shape=jax.ShapeDtypeStruct((m, n), out_dtype),
        compiler_params=_cparams(2),
        name="in_proj",
    )(x, w)


CONV_TS = 512
CONV_RC = 64
CONF_HIST = 32
SC_HIST = 8


def _conv_kernel(a_ref, gate_ref, sb_ref, sc_ref, sh_ref, cw_ref, cb_ref, lg_ref, lb_ref, sw_ref,
                 ya_ref, yd_ref, ha, hd):
    s = pl.program_id(1)
    ts = a_ref.shape[1]

    @pl.when(s == 0)
    def _():
        ha[0:CONF_HIST, :] = jnp.zeros((CONF_HIST, W_CONF), F32)
        hd[0:SC_HIST, :] = jnp.zeros((SC_HIST, W_SC), F32)

    ha[CONF_HIST:CONF_HIST + ts, :] = a_ref[0] * _sigmoid(gate_ref[0])
    hd[SC_HIST:SC_HIST + ts, :] = sc_ref[0] * sh_ref[0]

    cb = cb_ref[...]
    lg = lg_ref[...]
    lb = lb_ref[...]
    off_a = CONF_HIST - (CONF_KERNEL - 1)
    for c in range(ts // CONV_RC):
        r0 = c * CONV_RC
        acc = jnp.zeros((CONV_RC, W_CONF), F32)
        for j in range(CONF_KERNEL):
            acc = acc + cw_ref[j:j + 1, :] * ha[r0 + off_a + j:r0 + off_a + j + CONV_RC, :]
        hh = _layer_norm(acc + cb, lg, lb)
        ya_ref[0, r0:r0 + CONV_RC, :] = _silu(hh)

    off_d = SC_HIST - (SC_KERNEL - 1)
    accd = jnp.zeros((ts, W_SC), F32)
    for j in range(SC_KERNEL):
        accd = accd + sw_ref[j:j + 1, :] * hd[off_d + j:off_d + j + ts, :]
    yd_ref[0] = sb_ref[0] * accd

    ha[0:CONF_HIST, :] = ha[ts:ts + CONF_HIST, :]
    hd[0:SC_HIST, :] = hd[ts:ts + SC_HIST, :]


def _conv_mixers(u3, conf_w, conf_b, conf_lg, conf_lb, sc_w):
    bsz, seq, _ = u3.shape
    ts = CONV_TS
    wblk = W_CONF

    def col(c):
        return pl.BlockSpec((1, ts, wblk), lambda b, s, c=c: (b, s, c // wblk))

    def full(shape):
        return pl.BlockSpec(shape, lambda b, s: (0,) * len(shape))

    cw = jnp.zeros((32, W_CONF), F32).at[:CONF_KERNEL].set(conf_w)
    sw = jnp.zeros((8, W_SC), F32).at[:SC_KERNEL].set(sc_w)
    out_spec = pl.BlockSpec((1, ts, wblk), lambda b, s: (b, s, 0))
    return pl.pallas_call(
        _conv_kernel,
        grid=(bsz, seq // ts),
        in_specs=[col(COL_A), col(COL_A + W_CONF), col(COL_SB), col(COL_SC), col(COL_SH),
                  full((32, W_CONF)), full((1, W_CONF)), full((1, W_CONF)), full((1, W_CONF)), full((8, W_SC))],
        out_specs=[out_spec, out_spec],
        out_shape=[jax.ShapeDtypeStruct((bsz, seq, W_CONF), F32), jax.ShapeDtypeStruct((bsz, seq, W_SC), F32)],
        scratch_shapes=[pltpu.VMEM((CONF_HIST + ts, W_CONF), F32), pltpu.VMEM((SC_HIST + ts, W_SC), F32)],
        compiler_params=_cparams(2),
        name="conv_mixers",
    )(u3, u3, u3, u3, u3, cw, conf_b.reshape(1, -1), conf_lg.reshape(1, -1), conf_lb.reshape(1, -1), sw)


GLA_BLK = 128
GLA_SUB = 32


def _gla_kernel(q_ref, k_ref, v_ref, gl_ref, r_ref, wg2_ref, bg2_ref, ng_ref, tril_ref, o_ref, st_ref):
    seq = q_ref.shape[1]
    nblk = seq // GLA_BLK
    nsub = GLA_BLK // GLA_SUB
    st_ref[...] = jnp.zeros(st_ref.shape, F32)

    lane = lax.broadcasted_iota(I32, (1, LANES), 1)
    head_mask = [(lane < GLA_DK).astype(F32), (lane >= GLA_DK).astype(F32)]
    row = lax.broadcasted_iota(I32, (GLA_BLK, GLA_BLK), 0)
    col = lax.broadcasted_iota(I32, (GLA_BLK, GLA_BLK), 1)
    sub_shift = GLA_SUB.bit_length() - 1
    same_sub = (row >> sub_shift) == (col >> sub_shift)
    diag_mask = same_sub & (col <= row)
    row_sub = lax.broadcasted_iota(I32, (GLA_BLK, 1), 0) >> sub_shift
    scale = GLA_DK ** -0.5
    wg2 = wg2_ref[...]
    bg2 = bg2_ref[...]
    ng = ng_ref[...]
    tril = tril_ref[...]

    def body(s, carry):
        r0 = pl.multiple_of(s * GLA_BLK, GLA_BLK)
        q = q_ref[0, pl.ds(r0, GLA_BLK), :] * scale
        k = k_ref[0, pl.ds(r0, GLA_BLK), :]
        v = v_ref[0, pl.ds(r0, GLA_BLK), :]
        z = jnp.dot(gl_ref[0, pl.ds(r0, GLA_BLK), :], wg2, precision=HIGHEST, preferred_element_type=F32) + bg2
        log_a = (jnp.minimum(z, 0.0) - jnp.log(1.0 + jnp.exp(-jnp.abs(z)))) * (1.0 / GLA_TAU)
        b = jnp.dot(tril, log_a, precision=HIGHEST, preferred_element_type=F32)

        b3 = b.reshape(nsub, GLA_SUB, LANES)
        mid = jnp.broadcast_to(b3[:, GLA_SUB // 2 - 1:GLA_SUB // 2, :], b3.shape).reshape(GLA_BLK, LANES)
        qd = (q * jnp.exp(b - mid)).astype(BF16)
        kd = (k * jnp.exp(mid - b)).astype(BF16)
        q_off, k_off = [], []
        for j in range(nsub - 1):
            e_j = b[(j + 1) * GLA_SUB - 1:(j + 1) * GLA_SUB, :]
            q_off.append(jnp.where(row_sub > j, q * jnp.exp(jnp.minimum(b - e_j, 0.0)), 0.0))
            k_off.append(jnp.where(row_sub == j, k * jnp.exp(jnp.minimum(e_j - b, 0.0)), 0.0))
        q_cat = jnp.concatenate(q_off, axis=1)
        k_cat = jnp.concatenate(k_off, axis=1).astype(BF16)
        q_st = q * jnp.exp(b)
        b_last = b[GLA_BLK - 1:GLA_BLK, :]
        kk = (k * jnp.exp(b_last - b)).astype(BF16)
        st = st_ref[...]
        st_bf = st.astype(BF16)

        outs = []
        for h in range(2):
            hm = head_mask[h]
            hm3 = jnp.concatenate([hm] * (nsub - 1), axis=1)
            s_d = lax.dot_general((qd * hm.astype(BF16)), kd, (((1,), (1,)), ((), ())), preferred_element_type=F32)
            s_o = lax.dot_general((q_cat * hm3).astype(BF16), k_cat, (((1,), (1,)), ((), ())),
                                  preferred_element_type=F32)
            scores = jnp.where(diag_mask, s_d, 0.0) + s_o
            vh = v[:, h * GLA_DV:(h + 1) * GLA_DV].astype(BF16)
            o_h = jnp.dot(scores.astype(BF16), vh, preferred_element_type=F32)
            o_h = o_h + lax.dot_general((q_st * hm).astype(BF16), st_bf[h * GLA_DV:(h + 1) * GLA_DV, :],
                                        (((1,), (1,)), ((), ())), preferred_element_type=F32)
            o_h = o_h * lax.rsqrt(jnp.mean(o_h * o_h, axis=-1, keepdims=True) + RMS_EPS)
            outs.append(o_h)
        o = jnp.concatenate(outs, axis=1) * ng
        o_ref[0, pl.ds(r0, GLA_BLK), :] = o * _silu(r_ref[0, pl.ds(r0, GLA_BLK), :])

        upd = lax.dot_general(v.astype(BF16), kk, (((0,), (0,)), ((), ())), preferred_element_type=F32)
        st_ref[...] = st * jnp.exp(b_last) + upd
        return carry

    lax.fori_loop(0, nblk, body, 0)


def _gla(u3, w_g2, b_g2, norm_g):
    bsz, seq, _ = u3.shape
    npair = GLA_HEADS // 2
    kw = 2 * GLA_DK
    vw = 2 * GLA_DV

    def col(c, w):
        return pl.BlockSpec((1, seq, w), lambda b, p, c=c, w=w: (b, 0, c // w + p))

    wg2 = jnp.zeros((LANES, GLA_HEADS * GLA_DK), F32).at[:GLA_RANK].set(w_g2)
    ng = jnp.concatenate([norm_g, norm_g]).reshape(1, vw)
    tril = jnp.tril(jnp.ones((GLA_BLK, GLA_BLK), F32))
    return pl.pallas_call(
        _gla_kernel,
        grid=(bsz, npair),
        in_specs=[col(COL_GQ, kw), col(COL_GK, kw), col(COL_GV, vw),
                  pl.BlockSpec((1, seq, LANES), lambda b, p: (b, 0, COL_GL // LANES)),
                  col(COL_GR, vw),
                  pl.BlockSpec((LANES, kw), lambda b, p: (0, p)),
                  pl.BlockSpec((1, kw), lambda b, p: (0, p)),
                  pl.BlockSpec((1, vw), lambda b, p: (0, 0)),
                  pl.BlockSpec((GLA_BLK, GLA_BLK), lambda b, p: (0, 0))],
        out_specs=pl.BlockSpec((1, seq, vw), lambda b, p: (b, 0, p)),
        out_shape=jax.ShapeDtypeStruct((bsz, seq, W_GLA), F32),
        scratch_shapes=[pltpu.VMEM((vw, kw), F32)],
        compiler_params=_cparams(2),
        name="gla",
    )(u3, u3, u3, u3, u3, wg2, b_g2.reshape(1, -1), ng, tril)


def _t5_bucket(dist):
    max_exact = REL_BUCKETS // 2
    large = max_exact + (jnp.log(jnp.maximum(dist, 1).astype(F32) / max_exact)
                         / math.log(REL_MAX_DIST / max_exact) * (REL_BUCKETS - max_exact)).astype(I32)
    large = jnp.minimum(large, REL_BUCKETS - 1)
    return jnp.where(dist < max_exact, dist, large)


def _bias_kernel(rel_ref, bucket_ref, o_ref):
    h = pl.program_id(1)
    bucket = bucket_ref[0]
    acc = jnp.zeros(bucket.shape, F32)
    for b in range(REL_BUCKETS):
        acc = jnp.where(bucket == b, rel_ref[b, h], acc)
    o_ref[0, 0] = acc


def _bias_tables(rel_bias):
    qi = jnp.arange(DIL_BLOCK)[:, None]
    kj = jnp.arange(2 * DIL_BLOCK)[None, :]
    steps = jnp.maximum(qi + DIL_BLOCK - kj, 0)
    buckets = jnp.stack([_t5_bucket(steps * dil) for _, dil in DIL_BRANCHES]).astype(I32)
    nbr = len(DIL_BRANCHES)
    return pl.pallas_call(
        _bias_kernel,
        grid=(nbr, DIL_HEADS),
        in_specs=[pl.BlockSpec(memory_space=pltpu.SMEM),
                  pl.BlockSpec((1, DIL_BLOCK, 2 * DIL_BLOCK), lambda i, h: (i, 0, 0))],
        out_specs=pl.BlockSpec((1, 1, DIL_BLOCK, 2 * DIL_BLOCK), lambda i, h: (i, h, 0, 0)),
        out_shape=jax.ShapeDtypeStruct((nbr, DIL_HEADS, DIL_BLOCK, 2 * DIL_BLOCK), F32),
        compiler_params=_cparams(2),
        name="rel_bias_tables",
    )(rel_bias, buckets)


def _dil_kernel(q_ref, k_ref, v_ref, bias_ref, o_ref, qs, ks, vs, ob, lb):
    seq = q_ref.shape[1]
    blk = DIL_BLOCK
    scale = DIL_HD ** -0.5
    lane = lax.broadcasted_iota(I32, (1, LANES), 1)
    first = lane < DIL_HD
    head_mask = [first.astype(F32), 1.0 - first.astype(F32)]
    qi = lax.broadcasted_iota(I32, (blk, 2 * blk), 0)
    kj = lax.broadcasted_iota(I32, (blk, 2 * blk), 1)
    steps = qi + blk - kj
    neg = jnp.float32(-jnp.inf)

    ks[0:blk, :] = jnp.zeros((blk, LANES), BF16)
    vs[0:blk, :] = jnp.zeros((blk, LANES), BF16)

    for bi, (window, dil) in enumerate(DIL_BRANCHES):
        sub_len = seq // dil
        nb = sub_len // blk
        span = window // dil
        in_window = (steps >= 0) & (steps <= span)
        for r in range(dil):
            rows = pl.ds(r, sub_len, stride=dil) if dil > 1 else pl.ds(0, sub_len)
            qs[r * sub_len:(r + 1) * sub_len, :] = (q_ref[0, rows, :] * scale).astype(BF16)
            ks[blk + r * sub_len:blk + (r + 1) * sub_len, :] = k_ref[0, rows, :].astype(BF16)
            vs[blk + r * sub_len:blk + (r + 1) * sub_len, :] = v_ref[0, rows, :].astype(BF16)

        for r in range(dil):
            def body(n, carry, r=r, bi=bi, dil=dil, sub_len=sub_len, in_window=in_window):
                g0 = pl.multiple_of(r * sub_len + n * blk, blk)
                qb = qs[pl.ds(g0, blk), :]
                kb = ks[pl.ds(g0, 2 * blk), :]
                vb = vs[pl.ds(g0, 2 * blk), :]
                valid = in_window & (kj >= jnp.where(n > 0, 0, blk))
                o_acc = None
                l_acc = None
                for h in range(2):
                    sc = lax.dot_general(qb * head_mask[h].astype(BF16), kb, (((1,), (1,)), ((), ())),
                                         preferred_element_type=F32)
                    sc = jnp.where(valid, sc + bias_ref[bi, h], neg)
                    m = jnp.max(sc, axis=-1, keepdims=True)
                    p = jnp.exp(sc - m)
                    l = jnp.sum(p, axis=-1, keepdims=True)
                    o_h = jnp.dot(p.astype(BF16), vb, preferred_element_type=F32) / l
                    lse = jnp.broadcast_to(m + jnp.log(l), (blk, LANES))
                    if h == 0:
                        o_acc, l_acc = o_h, lse
                    else:
                        o_acc = jnp.where(first, o_acc, o_h)
                        l_acc = jnp.where(first, l_acc, lse)
                if dil > 1:
                    dst = pl.ds(n * (blk * dil) + r, blk, stride=dil)
                else:
                    dst = pl.ds(pl.multiple_of(n * blk, blk), blk)
                ob[bi, dst, :] = o_acc
                lb[bi, dst, :] = l_acc
                return carry

            lax.fori_loop(0, nb, body, 0)

    nbr = len(DIL_BRANCHES)
    mc = 256
    for c in range(seq // mc):
        rs = slice(c * mc, (c + 1) * mc)
        ls = [lb[i, rs, :] for i in range(nbr)]
        m = functools.reduce(jnp.maximum, ls)
        ws = [jnp.exp(l - m) for l in ls]
        num = functools.reduce(lambda a, b: a + b, [w * ob[i, rs, :] for i, w in enumerate(ws)])
        den = functools.reduce(lambda a, b: a + b, ws)
        o_ref[0, rs, :] = num / den


def _dilated(u3, bias_tab):
    bsz, seq, _ = u3.shape
    npair = DIL_HEADS // 2
    nbr = len(DIL_BRANCHES)

    def col(c):
        return pl.BlockSpec((1, seq, LANES), lambda b, p, c=c: (b, 0, c // LANES + p))

    return pl.pallas_call(
        _dil_kernel,
        grid=(bsz, npair),
        in_specs=[col(COL_CQ), col(COL_CK), col(COL_CV),
                  pl.BlockSpec((nbr, 2, DIL_BLOCK, 2 * DIL_BLOCK), lambda b, p: (0, p, 0, 0))],
        out_specs=pl.BlockSpec((1, seq, LANES), lambda b, p: (b, 0, p)),
        out_shape=jax.ShapeDtypeStruct((bsz, seq, W_DIL), F32),
        scratch_shapes=[pltpu.VMEM((seq, LANES), BF16),
                        pltpu.VMEM((DIL_BLOCK + seq, LANES), BF16),
                        pltpu.VMEM((DIL_BLOCK + seq, LANES), BF16),
                        pltpu.VMEM((nbr, seq, LANES), F32),
                        pltpu.VMEM((nbr, seq, LANES), F32)],
        compiler_params=_cparams(2),
        name="dilated_attn",
    )(u3, u3, u3, bias_tab)


TOK_TILE = 256


def _outproj_kernel(x_ref, ya_ref, yb_ref, yc_ref, yd_ref, ms_ref, wo_ref, g_ref, b_ref, rw_ref, rb_ref,
                    x1_ref, x1b_ref, ids_ref, gates_ref):
    acc = None
    for gi, y_ref in enumerate((ya_ref, yb_ref, yc_ref, yd_ref)):
        lo = gi * W_CONF
        y = (y_ref[...] * ms_ref[:, lo:lo + W_CONF]).astype(BF16)
        part = jnp.dot(y, wo_ref[lo:lo + W_CONF, :], preferred_element_type=F32)
        acc = part if acc is None else acc + part
    x1 = _layer_norm(DN_ALPHA * x_ref[...] + acc, g_ref[...], b_ref[...])
    x1_ref[...] = x1
    x1b_ref[...] = x1.astype(BF16)

    logits = jnp.dot(x1, rw_ref[...], precision=HIGHEST, preferred_element_type=F32) + rb_ref[...]
    tm = logits.shape[0]
    lane = lax.broadcasted_iota(I32, (tm, LANES), 1)
    neg = jnp.float32(-jnp.inf)
    gl = jnp.where(lane < N_GROUPS, logits, neg)
    gmax = jnp.max(gl, axis=-1, keepdims=True)
    g_top = 1.0 / jnp.sum(jnp.exp(gl - gmax), axis=-1, keepdims=True)
    g_idx = jnp.min(jnp.where(gl == gmax, lane, LANES), axis=-1, keepdims=True)
    e_lo = N_GROUPS + g_idx * EXPERTS_PER_GROUP
    el = jnp.where((lane >= e_lo) & (lane < e_lo + EXPERTS_PER_GROUP), logits, neg)
    m1 = jnp.max(el, axis=-1, keepdims=True)
    i1 = jnp.min(jnp.where(el == m1, lane, LANES), axis=-1, keepdims=True)
    el2 = jnp.where(lane == i1, neg, el)
    m2 = jnp.max(el2, axis=-1, keepdims=True)
    i2 = jnp.min(jnp.where(el2 == m2, lane, LANES), axis=-1, keepdims=True)
    e21 = jnp.exp(m2 - m1)
    w1 = g_top / (1.0 + e21)
    w2 = g_top * e21 / (1.0 + e21)
    ids_ref[...] = jnp.where(lane == 0, i1 - N_GROUPS, jnp.where(lane == 1, i2 - N_GROUPS, N_EXPERTS))
    gates_ref[...] = jnp.where(lane == 0, w1, jnp.where(lane == 1, w2, 0.0))


def _outproj(x, ya, yb, yc, yd, mix_scale, w_out_bf, ln_g, ln_b, rw, rb):
    t, d = x.shape
    tm = TOK_TILE

    def row(w):
        return pl.BlockSpec((tm, w), lambda i: (i, 0))

    def full(shape):
        return pl.BlockSpec(shape, lambda i: (0,) * len(shape))

    return pl.pallas_call(
        _outproj_kernel,
        grid=(t // tm,),
        in_specs=[row(d), row(W_CONF), row(W_GLA), row(W_DIL), row(W_SC), full((1, d)), full((d, d)),
                  full((1, d)), full((1, d)), full((d, LANES)), full((1, LANES))],
        out_specs=[row(d), row(d), row(LANES), row(LANES)],
        out_shape=[jax.ShapeDtypeStruct((t, d), F32), jax.ShapeDtypeStruct((t, d), BF16),
                   jax.ShapeDtypeStruct((t, LANES), I32), jax.ShapeDtypeStruct((t, LANES), F32)],
        compiler_params=_cparams(1),
        name="out_proj_ln_router",
    )(x, ya, yb, yc, yd, mix_scale.reshape(1, d), w_out_bf, ln_g.reshape(1, d), ln_b.reshape(1, d), rw, rb)


def _plan_kernel(ids_ref, tril_ref, upper_ref, pos_ref, tile_ref, cnt, carry, start):
    phase = pl.program_id(0)
    i = pl.program_id(1)
    tm = ids_ref.shape[0]
    lane = lax.broadcasted_iota(I32, (tm, LANES), 1)
    ids = ids_ref[...]
    e1 = ids[:, 0:1]
    e2 = ids[:, 1:2]
    oh1 = (lane == e1).astype(F32)
    oh2 = (lane == e2).astype(F32)
    oh = oh1 + oh2

    @pl.when((phase == 0) & (i == 0))
    def _():
        cnt[...] = jnp.zeros(cnt.shape, F32)

    @pl.when(phase == 0)
    def _():
        cnt[...] = cnt[...] + jnp.sum(oh, axis=0, keepdims=True)

    @pl.when((phase == 1) & (i == 0))
    def _():
        padded = jnp.floor((cnt[...] + (MOE_TILE - 1)) * (1.0 / MOE_TILE)) * MOE_TILE
        p8 = jnp.broadcast_to(padded, (SUBLANES, LANES))
        st = jnp.dot(p8, upper_ref[...], precision=HIGHEST, preferred_element_type=F32)
        start[...] = st[0:1, :]
        carry[...] = jnp.zeros(carry.shape, F32)
        end = st[0:1, :] + padded
        trow = lax.broadcasted_iota(I32, (LANES, LANES), 0).astype(F32) * MOE_TILE
        tlane = lax.broadcasted_iota(I32, (LANES, LANES), 1)
        done = jnp.where((end <= trow) & (tlane < N_EXPERTS), 1.0, 0.0)
        tile_ref[...] = jnp.broadcast_to(jnp.sum(done, axis=-1, keepdims=True), (LANES, LANES)).astype(I32)

    @pl.when(phase == 1)
    def _():
        incl = jnp.dot(tril_ref[...], oh.astype(BF16), preferred_element_type=F32) + carry[...]
        base = start[...] + incl - oh
        p1 = jnp.sum(oh1 * base, axis=-1, keepdims=True)
        p2 = jnp.sum(oh2 * base, axis=-1, keepdims=True)
        pos_ref[...] = jnp.where(lane == 0, p1, jnp.where(lane == 1, p2, 0.0)).astype(I32)
        carry[...] = carry[...] + jnp.sum(oh, axis=0, keepdims=True)


def _plan(ids):
    t = ids.shape[0]
    tm = TOK_TILE
    tril = jnp.tril(jnp.ones((tm, tm), BF16))
    upper = jnp.triu(jnp.ones((LANES, LANES), F32), k=1)
    return pl.pallas_call(
        _plan_kernel,
        grid=(2, t // tm),
        in_specs=[pl.BlockSpec((tm, LANES), lambda ph, i: (i, 0)),
                  pl.BlockSpec((tm, tm), lambda ph, i: (0, 0)),
                  pl.BlockSpec((LANES, LANES), lambda ph, i: (0, 0))],
        out_specs=[pl.BlockSpec((tm, LANES), lambda ph, i: (i * ph, 0)),
                   pl.BlockSpec((LANES, LANES), lambda ph, i: (0, 0))],
        out_shape=[jax.ShapeDtypeStruct((t, LANES), I32), jax.ShapeDtypeStruct((LANES, LANES), I32)],
        scratch_shapes=[pltpu.VMEM((1, LANES), F32), pltpu.VMEM((1, LANES), F32), pltpu.VMEM((1, LANES), F32)],
        compiler_params=_cparams(2),
        name="route_plan",
    )(ids, tril, upper)


def _dispatch_kernel(pos_ref, x_ref, xs_ref, sem):
    i = pl.program_id(0)
    tm = x_ref.shape[0]
    t = pl.num_programs(0) * tm

    def copy(m, k):
        dst = pos_ref[k * t + i * tm + m]
        return pltpu.make_async_copy(x_ref.at[pl.ds(m, 1)], xs_ref.at[pl.ds(dst, 1)], sem)

    def start(m, c):
        copy(m, 0).start()
        copy(m, 1).start()
        return c

    def wait(m, c):
        copy(m, 0).wait()
        copy(m, 1).wait()
        return c

    lax.fori_loop(0, tm, start, 0)
    lax.fori_loop(0, tm, wait, 0)


def _dispatch(pos_flat, x1, n_rows):
    t, d = x1.shape
    tm = TOK_TILE
    return pl.pallas_call(
        _dispatch_kernel,
        grid_spec=pltpu.PrefetchScalarGridSpec(
            num_scalar_prefetch=1,
            grid=(t // tm,),
            in_specs=[pl.BlockSpec((tm, d), lambda i, pos: (i, 0))],
            out_specs=pl.BlockSpec(memory_space=pl.ANY),
            scratch_shapes=[pltpu.SemaphoreType.DMA],
        ),
        out_shape=jax.ShapeDtypeStruct((n_rows, d), F32),
        compiler_params=_cparams(1),
        name="moe_dispatch",
    )(pos_flat, x1)


def _expert_kernel(te_ref, na_ref, xs_ref, wg_ref, wu_ref, wd_ref, ys_ref):
    i = pl.program_id(0)

    @pl.when(i < na_ref[0])
    def _():
        xb = xs_ref[...].astype(BF16)
        g = jnp.dot(xb, wg_ref[0].astype(BF16), preferred_element_type=F32)
        u = jnp.dot(xb, wu_ref[0].astype(BF16), preferred_element_type=F32)
        h = (_silu(g) * u).astype(BF16)
        ys_ref[...] = jnp.dot(h, wd_ref[0].astype(BF16), preferred_element_type=F32)


def _experts(tile_expert, n_active, xs, w_gate, w_up, w_down):
    n_rows, d = xs.shape
    n_tiles = n_rows // MOE_TILE

    def rows(i, te, na):
        return (jnp.minimum(i, na[0] - 1), 0)

    return pl.pallas_call(
        _expert_kernel,
        grid_spec=pltpu.PrefetchScalarGridSpec(
            num_scalar_prefetch=2,
            grid=(n_tiles,),
            in_specs=[pl.BlockSpec((MOE_TILE, d), rows),
                      pl.BlockSpec((1, d, D_EXPERT), lambda i, te, na: (te[i], 0, 0)),
                      pl.BlockSpec((1, d, D_EXPERT), lambda i, te, na: (te[i], 0, 0)),
                      pl.BlockSpec((1, D_EXPERT, d), lambda i, te, na: (te[i], 0, 0))],
            out_specs=pl.BlockSpec((MOE_TILE, d), rows),
        ),
        out_shape=jax.ShapeDtypeStruct((n_rows, d), F32),
        compiler_params=_cparams(1),
        name="moe_experts",
    )(tile_expert, n_active, xs, w_gate, w_up, w_down)


def _final_kernel(pos_ref, x1_ref, x1b_ref, gates_ref, p_ref, wg_ref, bg_ref, wp_ref, g_ref, b_ref, ys_ref,
                  x2_ref, x2b_ref, ybuf, sem):
    i = pl.program_id(0)
    tm = x1_ref.shape[0]
    t = pl.num_programs(0) * tm

    def copy(m, k):
        src = pos_ref[k * t + i * tm + m]
        return pltpu.make_async_copy(ys_ref.at[pl.ds(src, 1)], ybuf.at[k, pl.ds(m, 1)], sem)

    def start(m, c):
        copy(m, 0).start()
        copy(m, 1).start()
        return c

    def wait(m, c):
        copy(m, 0).wait()
        copy(m, 1).wait()
        return c

    lax.fori_loop(0, tm, start, 0)
    gate = _sigmoid(jnp.dot(x1b_ref[...], wg_ref[...], preferred_element_type=F32) + bg_ref[...])
    ple = gate * jnp.dot(p_ref[...].astype(BF16), wp_ref[...], preferred_element_type=F32)
    lax.fori_loop(0, tm, wait, 0)
    gates = gates_ref[...]
    ffn = gates[:, 0:1] * ybuf[0] + gates[:, 1:2] * ybuf[1]
    x2 = _layer_norm(DN_ALPHA * x1_ref[...] + ffn + ple, g_ref[...], b_ref[...])
    x2_ref[...] = x2
    x2b_ref[...] = x2.astype(BF16)


def _final(pos_flat, x1, x1b, gates, p, wg_bf, bg, wp_bf, ln_g, ln_b, ys):
    t, d = x1.shape
    tm = TOK_TILE

    def row(w):
        return pl.BlockSpec((tm, w), lambda i, pos: (i, 0))

    def full(shape):
        return pl.BlockSpec(shape, lambda i, pos: (0,) * len(shape))

    return pl.pallas_call(
        _final_kernel,
        grid_spec=pltpu.PrefetchScalarGridSpec(
            num_scalar_prefetch=1,
            grid=(t // tm,),
            in_specs=[row(d), row(d), row(LANES), row(PLE_DIM), full((d, d)), full((1, d)), full((PLE_DIM, d)),
                      full((1, d)), full((1, d)), pl.BlockSpec(memory_space=pl.ANY)],
            out_specs=[row(d), row(d)],
            scratch_shapes=[pltpu.VMEM((TOP_K, tm, d), F32), pltpu.SemaphoreType.DMA],
        ),
        out_shape=[jax.ShapeDtypeStruct((t, d), F32), jax.ShapeDtypeStruct((t, d), BF16)],
        compiler_params=_cparams(1),
        name="combine_ple_ln",
    )(pos_flat, x1, x1b, gates, p, wg_bf, bg.reshape(1, d), wp_bf, ln_g.reshape(1, d), ln_b.reshape(1, d), ys)


def _arrange_w_in(w):
    offs = [0]
    for s in IN_SIZES:
        offs.append(offs[-1] + s)
    seg = [w[:, offs[i]:offs[i + 1]] for i in range(len(IN_SIZES))]
    a_u, gq, gk, gv, gl, gr, cq, ck, cv, sb, sc, sh = seg
    pad = jnp.zeros((w.shape[0], LANES - GLA_RANK), w.dtype)
    return jnp.concatenate([a_u, gv, gr, cq, ck, cv, sb, sc, sh, gq, gk, gl, pad], axis=1).astype(BF16)


def kernel(x, p, w_in, conf_dw_w, conf_dw_b, conf_ln_g, conf_ln_b, gla_w_g2, gla_b_g2, gla_norm_g, sc_conv_w, mix_scale, w_out, rel_bias, ln1_g, ln1_b, router_g_w, router_g_b, router_e_w, router_e_b, exp_w_gate, exp_w_up, exp_w_down, ple_w_gate, ple_b_gate, ple_w_proj, ln2_g, ln2_b):
    bsz, seq, d = x.shape
    t = bsz * seq
    n_rows = TOP_K * t + N_EXPERTS * MOE_TILE
    n_tiles = n_rows // MOE_TILE
    bias_tab = _bias_tables(rel_bias)
    xf = x.reshape(t, d)
    xb = xf
    for i in range(DEPTH):
        u = _matmul(xb, _arrange_w_in(w_in[i]), 512, 1920, F32)
        u3 = u.reshape(bsz, seq, U_COLS)
        ya, yd = _conv_mixers(u3, conf_dw_w[i], conf_dw_b[i], conf_ln_g[i], conf_ln_b[i], sc_conv_w[i])
        yb = _gla(u3, gla_w_g2[i], gla_b_g2[i], gla_norm_g[i])
        yc = _dilated(u3, bias_tab)
        rw = jnp.zeros((d, LANES), F32).at[:, :N_GROUPS].set(router_g_w[i]).at[:, N_GROUPS:N_GROUPS + N_EXPERTS].set(router_e_w[i])
        rb = jnp.zeros((1, LANES), F32).at[0, :N_GROUPS].set(router_g_b[i]).at[0, N_GROUPS:N_GROUPS + N_EXPERTS].set(router_e_b[i])
        x1, x1b, ids, gates = _outproj(xf, ya.reshape(t, -1), yb.reshape(t, -1), yc.reshape(t, -1), yd.reshape(t, -1),
                                       mix_scale[i], w_out[i].astype(BF16), ln1_g[i], ln1_b[i], rw, rb)
        pos, tile_tab = _plan(ids)
        pos_flat = pos[:, :TOP_K].T.reshape(-1)
        tile_end = tile_tab[:n_tiles, 0]
        n_active = jnp.sum((tile_end < N_EXPERTS).astype(I32)).reshape(1)
        tile_expert = jnp.minimum(tile_end, N_EXPERTS - 1)
        xs = _dispatch(pos_flat, x1, n_rows)
        ys = _experts(tile_expert, n_active, xs, exp_w_gate[i], exp_w_up[i], exp_w_down[i])
        xf, xb = _final(pos_flat, x1, x1b, gates, p[i].reshape(t, PLE_DIM), ple_w_gate[i].astype(BF16), ple_b_gate[i],
                        ple_w_proj[i].astype(BF16), ln2_g[i], ln2_b[i], ys)
    return xf.reshape(bsz, seq, d)
```

```python
import functools
import math

import jax
import jax.numpy as jnp
from jax import lax
from jax.experimental import pallas as pl
from jax.experimental.pallas import tpu as pltpu

F32 = jnp.float32
BF16 = jnp.bfloat16
I32 = jnp.int32
HIGHEST = lax.Precision.HIGHEST

D_MODEL = 2048
DEPTH = 2
PLE_DIM = 256
W_CONF = 512
W_GLA = 512
W_DIL = 512
W_SC = 512
CONF_KERNEL = 31
GLA_HEADS = 4
GLA_DV = 128
GLA_DK = 64
GLA_RANK = 16
GLA_TAU = 16.0
DIL_HEADS = 8
DIL_HD = 64
DIL_BRANCHES = ((128, 1), (512, 4), (2048, 16))
DIL_BLOCK = 128
REL_BUCKETS = 32
REL_MAX_DIST = 2048
SC_KERNEL = 3
N_GROUPS = 4
EXPERTS_PER_GROUP = 8
N_EXPERTS = 32
TOP_K = 2
D_EXPERT = 256
DN_ALPHA = (2 * DEPTH) ** 0.25
LN_EPS = 1e-5
RMS_EPS = 1e-6
IN_SIZES = (2 * W_CONF, GLA_HEADS * GLA_DK, GLA_HEADS * GLA_DK, W_GLA, GLA_RANK, W_GLA,
            W_DIL, W_DIL, W_DIL, W_SC, W_SC, W_SC)

LANES = 128
SUBLANES = 8
VMEM_LIMIT = 56 * 1024 * 1024

COL_A = 0
COL_GV = 1024
COL_GR = 1536
COL_CQ = 2048
COL_CK = 2560
COL_CV = 3072
COL_SB = 3584
COL_SC = 4096
COL_SH = 4608
COL_GQ = 5120
COL_GK = 5376
COL_GL = 5632
U_COLS = 5760

MOE_TILE = 256


def _cparams(n_axes):
    return pltpu.CompilerParams(dimension_semantics=("arbitrary",) * n_axes, vmem_limit_bytes=VMEM_LIMIT)


def _layer_norm(x, g, b):
    mu = jnp.mean(x, axis=-1, keepdims=True)
    xc = x - mu
    var = jnp.mean(xc * xc, axis=-1, keepdims=True)
    return xc * lax.rsqrt(var + LN_EPS) * g + b


def _sigmoid(x):
    return 1.0 / (1.0 + jnp.exp(-x))


def _silu(x):
    return x * _sigmoid(x)


def _matmul_kernel(x_ref, w_ref, o_ref):
    o_ref[...] = jnp.dot(x_ref[...].astype(BF16), w_ref[0],
                         preferred_element_type=F32).astype(o_ref.dtype)


def _in_proj(x, w_all, layer, tm, tn, out_dtype):
    m, k = x.shape
    n = w_all.shape[2]
    return pl.pallas_call(
        _matmul_kernel,
        grid=(n // tn, m // tm),
        in_specs=[pl.BlockSpec((tm, k), lambda j, i: (i, 0)),
                  pl.BlockSpec((1, k, tn), lambda j, i: (layer, 0, j))],
        out_specs=pl.BlockSpec((tm, tn), lambda j, i: (i, j)),
        out_shape=jax.ShapeDtypeStruct((m, n), out_dtype),
        compiler_params=_cparams(2),
        name="in_proj",
    )(x, w_all)


CONV_TS = 512
CONV_RC = 64
CONF_HIST = 32
SC_HIST = 8


def _conv_kernel(a_ref, gate_ref, sb_ref, sc_ref, sh_ref, cw_ref, cb_ref, lg_ref, lb_ref, sw_ref,
                 ya_ref, yd_ref, ha, hd):
    s = pl.program_id(1)
    ts = a_ref.shape[1]

    @pl.when(s == 0)
    def _():
        ha[0:CONF_HIST, :] = jnp.zeros((CONF_HIST, W_CONF), F32)
        hd[0:SC_HIST, :] = jnp.zeros((SC_HIST, W_SC), F32)

    ha[CONF_HIST:CONF_HIST + ts, :] = a_ref[0] * _sigmoid(gate_ref[0])
    hd[SC_HIST:SC_HIST + ts, :] = sc_ref[0] * sh_ref[0]

    cb = cb_ref[...]
    lg = lg_ref[...]
    lb = lb_ref[...]
    off_a = CONF_HIST - (CONF_KERNEL - 1)
    for c in range(ts // CONV_RC):
        r0 = c * CONV_RC
        acc = jnp.zeros((CONV_RC, W_CONF), F32)
        for j in range(CONF_KERNEL):
            acc = acc + cw_ref[j:j + 1, :] * ha[r0 + off_a + j:r0 + off_a + j + CONV_RC, :]
        hh = _layer_norm(acc + cb, lg, lb)
        ya_ref[0, r0:r0 + CONV_RC, :] = _silu(hh)

    off_d = SC_HIST - (SC_KERNEL - 1)
    accd = jnp.zeros((ts, W_SC), F32)
    for j in range(SC_KERNEL):
        accd = accd + sw_ref[j:j + 1, :] * hd[off_d + j:off_d + j + ts, :]
    yd_ref[0] = sb_ref[0] * accd

    ha[0:CONF_HIST, :] = ha[ts:ts + CONF_HIST, :]
    hd[0:SC_HIST, :] = hd[ts:ts + SC_HIST, :]


def _conv_mixers(u3, conf_w, conf_b, conf_lg, conf_lb, sc_w):
    bsz, seq, _ = u3.shape
    ts = CONV_TS
    wblk = W_CONF

    def col(c):
        return pl.BlockSpec((1, ts, wblk), lambda b, s, c=c: (b, s, c // wblk))

    def full(shape):
        return pl.BlockSpec(shape, lambda b, s: (0,) * len(shape))

    cw = jnp.zeros((32, W_CONF), F32).at[:CONF_KERNEL].set(conf_w)
    sw = jnp.zeros((8, W_SC), F32).at[:SC_KERNEL].set(sc_w)
    out_spec = pl.BlockSpec((1, ts, wblk), lambda b, s: (b, s, 0))
    return pl.pallas_call(
        _conv_kernel,
        grid=(bsz, seq // ts),
        in_specs=[col(COL_A), col(COL_A + W_CONF), col(COL_SB), col(COL_SC), col(COL_SH),
                  full((32, W_CONF)), full((1, W_CONF)), full((1, W_CONF)), full((1, W_CONF)), full((8, W_SC))],
        out_specs=[out_spec, out_spec],
        out_shape=[jax.ShapeDtypeStruct((bsz, seq, W_CONF), F32), jax.ShapeDtypeStruct((bsz, seq, W_SC), F32)],
        scratch_shapes=[pltpu.VMEM((CONF_HIST + ts, W_CONF), F32), pltpu.VMEM((SC_HIST + ts, W_SC), F32)],
        compiler_params=_cparams(2),
        name="conv_mixers",
    )(u3, u3, u3, u3, u3, cw, conf_b.reshape(1, -1), conf_lg.reshape(1, -1), conf_lb.reshape(1, -1), sw)


GLA_BLK = 128
GLA_SUB = 32


def _gla_kernel(q_ref, k_ref, v_ref, gl_ref, r_ref, wg2_ref, bg2_ref, ng_ref, tril_ref, o_ref, st_ref):
    seq = q_ref.shape[1]
    nblk = seq // GLA_BLK
    nsub = GLA_BLK // GLA_SUB
    st_ref[...] = jnp.zeros(st_ref.shape, F32)

    lane = lax.broadcasted_iota(I32, (1, LANES), 1)
    head_mask = [(lane < GLA_DK).astype(F32), (lane >= GLA_DK).astype(F32)]
    row = lax.broadcasted_iota(I32, (GLA_BLK, GLA_BLK), 0)
    col = lax.broadcasted_iota(I32, (GLA_BLK, GLA_BLK), 1)
    sub_shift = GLA_SUB.bit_length() - 1
    same_sub = (row >> sub_shift) == (col >> sub_shift)
    diag_mask = same_sub & (col <= row)
    row_sub = lax.broadcasted_iota(I32, (GLA_BLK, 1), 0) >> sub_shift
    scale = GLA_DK ** -0.5
    wg2 = wg2_ref[...]
    bg2 = bg2_ref[...]
    ng = ng_ref[...]
    tril = tril_ref[...]

    def body(s, carry):
        r0 = pl.multiple_of(s * GLA_BLK, GLA_BLK)
        q = q_ref[0, pl.ds(r0, GLA_BLK), :] * scale
        k = k_ref[0, pl.ds(r0, GLA_BLK), :]
        v = v_ref[0, pl.ds(r0, GLA_BLK), :]
        z = jnp.dot(gl_ref[0, pl.ds(r0, GLA_BLK), :], wg2, precision=HIGHEST, preferred_element_type=F32) + bg2
        log_a = (jnp.minimum(z, 0.0) - jnp.log(1.0 + jnp.exp(-jnp.abs(z)))) * (1.0 / GLA_TAU)
        b = jnp.dot(tril, log_a, precision=HIGHEST, preferred_element_type=F32)

        b3 = b.reshape(nsub, GLA_SUB, LANES)
        mid = jnp.broadcast_to(b3[:, GLA_SUB // 2 - 1:GLA_SUB // 2, :], b3.shape).reshape(GLA_BLK, LANES)
        qd = (q * jnp.exp(b - mid)).astype(BF16)
        kd = (k * jnp.exp(mid - b)).astype(BF16)
        q_off, k_off = [], []
        for j in range(nsub - 1):
            e_j = b[(j + 1) * GLA_SUB - 1:(j + 1) * GLA_SUB, :]
            q_off.append(jnp.where(row_sub > j, q * jnp.exp(jnp.minimum(b - e_j, 0.0)), 0.0))
            k_off.append(jnp.where(row_sub == j, k * jnp.exp(jnp.minimum(e_j - b, 0.0)), 0.0))
        q_cat = jnp.concatenate(q_off, axis=1)
        k_cat = jnp.concatenate(k_off, axis=1).astype(BF16)
        q_st = q * jnp.exp(b)
        b_last = b[GLA_BLK - 1:GLA_BLK, :]
        kk = (k * jnp.exp(b_last - b)).astype(BF16)
        st = st_ref[...]
        st_bf = st.astype(BF16)

        outs = []
        for h in range(2):
            hm = head_mask[h]
            hm3 = jnp.concatenate([hm] * (nsub - 1), axis=1)
            s_d = lax.dot_general((qd * hm.astype(BF16)), kd, (((1,), (1,)), ((), ())), preferred_element_type=F32)
            s_o = lax.dot_general((q_cat * hm3).astype(BF16), k_cat, (((1,), (1,)), ((), ())),
                                  preferred_element_type=F32)
            scores = jnp.where(diag_mask, s_d, 0.0) + s_o
            vh = v[:, h * GLA_DV:(h + 1) * GLA_DV].astype(BF16)
            o_h = jnp.dot(scores.astype(BF16), vh, preferred_element_type=F32)
            o_h = o_h + lax.dot_general((q_st * hm).astype(BF16), st_bf[h * GLA_DV:(h + 1) * GLA_DV, :],
                                        (((1,), (1,)), ((), ())), preferred_element_type=F32)
            o_h = o_h * lax.rsqrt(jnp.mean(o_h * o_h, axis=-1, keepdims=True) + RMS_EPS)
            outs.append(o_h)
        o = jnp.concatenate(outs, axis=1) * ng
        o_ref[0, pl.ds(r0, GLA_BLK), :] = o * _silu(r_ref[0, pl.ds(r0, GLA_BLK), :])

        upd = lax.dot_general(v.astype(BF16), kk, (((0,), (0,)), ((), ())), preferred_element_type=F32)
        st_ref[...] = st * jnp.exp(b_last) + upd
        return carry

    lax.fori_loop(0, nblk, body, 0)


def _gla(u3, w_g2, b_g2, norm_g):
    bsz, seq, _ = u3.shape
    npair = GLA_HEADS // 2
    kw = 2 * GLA_DK
    vw = 2 * GLA_DV

    def col(c, w):
        return pl.BlockSpec((1, seq, w), lambda b, p, c=c, w=w: (b, 0, c // w + p))

    wg2 = jnp.zeros((LANES, GLA_HEADS * GLA_DK), F32).at[:GLA_RANK].set(w_g2)
    ng = jnp.concatenate([norm_g, norm_g]).reshape(1, vw)
    tril = jnp.tril(jnp.ones((GLA_BLK, GLA_BLK), F32))
    return pl.pallas_call(
        _gla_kernel,
        grid=(bsz, npair),
        in_specs=[col(COL_GQ, kw), col(COL_GK, kw), col(COL_GV, vw),
                  pl.BlockSpec((1, seq, LANES), lambda b, p: (b, 0, COL_GL // LANES)),
                  col(COL_GR, vw),
                  pl.BlockSpec((LANES, kw), lambda b, p: (0, p)),
                  pl.BlockSpec((1, kw), lambda b, p: (0, p)),
                  pl.BlockSpec((1, vw), lambda b, p: (0, 0)),
                  pl.BlockSpec((GLA_BLK, GLA_BLK), lambda b, p: (0, 0))],
        out_specs=pl.BlockSpec((1, seq, vw), lambda b, p: (b, 0, p)),
        out_shape=jax.ShapeDtypeStruct((bsz, seq, W_GLA), F32),
        scratch_shapes=[pltpu.VMEM((vw, kw), F32)],
        compiler_params=_cparams(2),
        name="gla",
    )(u3, u3, u3, u3, u3, wg2, b_g2.reshape(1, -1), ng, tril)


def _t5_bucket(dist):
    max_exact = REL_BUCKETS // 2
    large = max_exact + (jnp.log(jnp.maximum(dist, 1).astype(F32) / max_exact)
                         / math.log(REL_MAX_DIST / max_exact) * (REL_BUCKETS - max_exact)).astype(I32)
    large = jnp.minimum(large, REL_BUCKETS - 1)
    return jnp.where(dist < max_exact, dist, large)


LOG2E = math.log2(math.e)
DIL_GROUP = 4


def _bias_kernel(rel_ref, bucket_ref, o_ref):
    h = pl.program_id(1)
    blk = DIL_BLOCK
    bucket = bucket_ref[0]
    acc = jnp.zeros(bucket.shape, F32)
    for b in range(REL_BUCKETS):
        acc = jnp.where(bucket == b, rel_ref[b, h], acc)
    acc = acc * LOG2E
    qi = lax.broadcasted_iota(I32, (blk, 2 * blk), 0)
    kj = lax.broadcasted_iota(I32, (blk, 2 * blk), 1)
    steps = qi + blk - kj
    in_window = (steps >= 0) & (steps <= blk)
    neg = jnp.float32(-jnp.inf)
    o_ref[0, 0, 0] = jnp.where(in_window, acc, neg)
    o_ref[0, 0, 1] = jnp.where(in_window & (kj >= blk), acc, neg)


def _bias_tables(rel_bias):
    assert all(window // dil == DIL_BLOCK for window, dil in DIL_BRANCHES)
    qi = jnp.arange(DIL_BLOCK)[:, None]
    kj = jnp.arange(2 * DIL_BLOCK)[None, :]
    steps = jnp.maximum(qi + DIL_BLOCK - kj, 0)
    buckets = jnp.stack([_t5_bucket(steps * dil) for _, dil in DIL_BRANCHES]).astype(I32)
    nbr = len(DIL_BRANCHES)
    return pl.pallas_call(
        _bias_kernel,
        grid=(nbr, DIL_HEADS),
        in_specs=[pl.BlockSpec(memory_space=pltpu.SMEM),
                  pl.BlockSpec((1, DIL_BLOCK, 2 * DIL_BLOCK), lambda i, h: (i, 0, 0))],
        out_specs=pl.BlockSpec((1, 1, 2, DIL_BLOCK, 2 * DIL_BLOCK), lambda i, h: (i, h, 0, 0, 0)),
        out_shape=jax.ShapeDtypeStruct((nbr, DIL_HEADS, 2, DIL_BLOCK, 2 * DIL_BLOCK), F32),
        compiler_params=_cparams(2),
        name="rel_bias_tables",
    )(rel_bias, buckets)


def _dil_kernel(q_ref, k_ref, v_ref, bias_ref, o_ref, qs, ks, vs, ob, lb):
    seq = q_ref.shape[1]
    blk = DIL_BLOCK
    scale = DIL_HD ** -0.5 * LOG2E
    lane = lax.broadcasted_iota(I32, (1, LANES), 1)
    first = lane < DIL_HD
    head_mask = [first.astype(F32), 1.0 - first.astype(F32)]

    ks[0:blk, :] = jnp.zeros((blk, LANES), BF16)
    vs[0:blk, :] = jnp.zeros((blk, LANES), BF16)

    for bi, (window, dil) in enumerate(DIL_BRANCHES):
        sub_len = seq // dil
        nb = sub_len // blk
        nb_shift = nb.bit_length() - 1
        for r in range(dil):
            rows = pl.ds(r, sub_len, stride=dil) if dil > 1 else pl.ds(0, sub_len)
            q = q_ref[0, rows, :] * scale
            for h in range(2):
                qs[h, r * sub_len:(r + 1) * sub_len, :] = (q * head_mask[h]).astype(BF16)
            ks[blk + r * sub_len:blk + (r + 1) * sub_len, :] = k_ref[0, rows, :].astype(BF16)
            vs[blk + r * sub_len:blk + (r + 1) * sub_len, :] = v_ref[0, rows, :].astype(BF16)

        def body(it, carry, bi=bi, dil=dil, nb=nb, nb_shift=nb_shift):
            for gg in range(DIL_GROUP):
                g = it * DIL_GROUP + gg
                n = g & (nb - 1)
                r = g >> nb_shift
                g0 = pl.multiple_of(g * blk, blk)
                kb = ks[pl.ds(g0, 2 * blk), :]
                vb = vs[pl.ds(g0, 2 * blk), :]
                variant = jnp.where(n == 0, 1, 0)
                o_acc = None
                l_acc = None
                for h in range(2):
                    sc = lax.dot_general(qs[h, pl.ds(g0, blk), :], kb, (((1,), (1,)), ((), ())),
                                         preferred_element_type=F32)
                    sc = sc + bias_ref[bi, h, variant]
                    m = jnp.max(sc, axis=-1, keepdims=True)
                    p = jnp.exp2(sc - m)
                    l = jnp.sum(p, axis=-1, keepdims=True)
                    o_h = jnp.dot(p.astype(BF16), vb, preferred_element_type=F32) * (1.0 / l)
                    lse = jnp.broadcast_to(m + jnp.log2(l), (blk, LANES))
                    if h == 0:
                        o_acc, l_acc = o_h, lse
                    else:
                        o_acc = jnp.where(first, o_acc, o_h)
                        l_acc = jnp.where(first, l_acc, lse)
                if dil > 1:
                    dst = pl.ds(n * (blk * dil) + r, blk, stride=dil)
                else:
                    dst = pl.ds(g0, blk)
                ob[bi, dst, :] = o_acc
                lb[bi, dst, :] = l_acc
            return carry

        lax.fori_loop(0, seq // blk // DIL_GROUP, body, 0)

    nbr = len(DIL_BRANCHES)
    mc = 256
    for c in range(seq // mc):
        rs = slice(c * mc, (c + 1) * mc)
        ls = [lb[i, rs, :] for i in range(nbr)]
        m = functools.reduce(jnp.maximum, ls)
        ws = [jnp.exp2(l - m) for l in ls]
        num = functools.reduce(lambda a, b: a + b, [w * ob[i, rs, :] for i, w in enumerate(ws)])
        den = functools.reduce(lambda a, b: a + b, ws)
        o_ref[0, rs, :] = num / den


def _dilated(u3, bias_tab):
    bsz, seq, _ = u3.shape
    npair = DIL_HEADS // 2
    nbr = len(DIL_BRANCHES)

    def col(c):
        return pl.BlockSpec((1, seq, LANES), lambda b, p, c=c: (b, 0, c // LANES + p))

    return pl.pallas_call(
        _dil_kernel,
        grid=(bsz, npair),
        in_specs=[col(COL_CQ), col(COL_CK), col(COL_CV),
                  pl.BlockSpec((nbr, 2, 2, DIL_BLOCK, 2 * DIL_BLOCK), lambda b, p: (0, p, 0, 0, 0))],
        out_specs=pl.BlockSpec((1, seq, LANES), lambda b, p: (b, 0, p)),
        out_shape=jax.ShapeDtypeStruct((bsz, seq, W_DIL), F32),
        scratch_shapes=[pltpu.VMEM((2, seq, LANES), BF16),
                        pltpu.VMEM((DIL_BLOCK + seq, LANES), BF16),
                        pltpu.VMEM((DIL_BLOCK + seq, LANES), BF16),
                        pltpu.VMEM((nbr, seq, LANES), F32),
                        pltpu.VMEM((nbr, seq, LANES), F32)],
        compiler_params=_cparams(2),
        name="dilated_attn",
    )(u3, u3, u3, bias_tab)


TOK_TILE = 256
OUT_TILE = 512


def _outproj_kernel(x_ref, ya_ref, yb_ref, yc_ref, yd_ref, ms_ref, wo_ref, g_ref, b_ref, rwh_ref, rwl_ref, rb_ref,
                    x1_ref, x1b_ref, ids_ref, gates_ref):
    acc = None
    for gi, y_ref in enumerate((ya_ref, yb_ref, yc_ref, yd_ref)):
        lo = gi * W_CONF
        y = (y_ref[...] * ms_ref[0, :, lo:lo + W_CONF]).astype(BF16)
        part = jnp.dot(y, wo_ref[0, lo:lo + W_CONF, :], preferred_element_type=F32)
        acc = part if acc is None else acc + part
    x1 = _layer_norm(DN_ALPHA * x_ref[...] + acc, g_ref[0], b_ref[0])
    x1_ref[...] = x1
    x1_hi = x1.astype(BF16)
    x1b_ref[...] = x1_hi

    x1_lo = (x1 - x1_hi.astype(F32)).astype(BF16)
    rwh = rwh_ref[...]
    logits = (jnp.dot(x1_hi, rwh, preferred_element_type=F32) + jnp.dot(x1_lo, rwh, preferred_element_type=F32)
              + jnp.dot(x1_hi, rwl_ref[...], preferred_element_type=F32) + rb_ref[...])
    tm = logits.shape[0]
    lane = lax.broadcasted_iota(I32, (tm, LANES), 1)
    neg = jnp.float32(-jnp.inf)
    gl = jnp.where(lane < N_GROUPS, logits, neg)
    gmax = jnp.max(gl, axis=-1, keepdims=True)
    g_top = 1.0 / jnp.sum(jnp.exp(gl - gmax), axis=-1, keepdims=True)
    g_idx = jnp.min(jnp.where(gl == gmax, lane, LANES), axis=-1, keepdims=True)
    e_lo = N_GROUPS + g_idx * EXPERTS_PER_GROUP
    el = jnp.where((lane >= e_lo) & (lane < e_lo + EXPERTS_PER_GROUP), logits, neg)
    m1 = jnp.max(el, axis=-1, keepdims=True)
    i1 = jnp.min(jnp.where(el == m1, lane, LANES), axis=-1, keepdims=True)
    el2 = jnp.where(lane == i1, neg, el)
    m2 = jnp.max(el2, axis=-1, keepdims=True)
    i2 = jnp.min(jnp.where(el2 == m2, lane, LANES), axis=-1, keepdims=True)
    e21 = jnp.exp(m2 - m1)
    w1 = g_top / (1.0 + e21)
    w2 = g_top * e21 / (1.0 + e21)
    ids_ref[...] = jnp.where(lane == 0, i1 - N_GROUPS, jnp.where(lane == 1, i2 - N_GROUPS, N_EXPERTS))
    gates_ref[...] = jnp.where(lane == 0, w1, jnp.where(lane == 1, w2, 0.0))


def _outproj(layer, x, ya, yb, yc, yd, mix_scale, w_out_bf, ln_g, ln_b, rw_hi, rw_lo, rb):
    t, d = x.shape
    tm = OUT_TILE

    def row(w):
        return pl.BlockSpec((tm, w), lambda i: (i, 0))

    def full(shape):
        return pl.BlockSpec(shape, lambda i: (0,) * len(shape))

    def per_layer(shape):
        return pl.BlockSpec((1,) + shape, lambda i: (layer,) + (0,) * len(shape))

    return pl.pallas_call(
        _outproj_kernel,
        grid=(t // tm,),
        in_specs=[row(d), row(W_CONF), row(W_GLA), row(W_DIL), row(W_SC), per_layer((1, d)), per_layer((d, d)),
                  per_layer((1, d)), per_layer((1, d)), full((d, LANES)), full((d, LANES)), full((1, LANES))],
        out_specs=[row(d), row(d), row(LANES), row(LANES)],
        out_shape=[jax.ShapeDtypeStruct((t, d), F32), jax.ShapeDtypeStruct((t, d), BF16),
                   jax.ShapeDtypeStruct((t, LANES), I32), jax.ShapeDtypeStruct((t, LANES), F32)],
        compiler_params=_cparams(1),
        name="out_proj_ln_router",
    )(x, ya, yb, yc, yd, mix_scale, w_out_bf, ln_g, ln_b, rw_hi, rw_lo, rb)


def _plan_kernel(ids_ref, tril_ref, upper_ref, pos_ref, tile_ref, cnt, carry, start):
    phase = pl.program_id(0)
    i = pl.program_id(1)
    tm = ids_ref.shape[0]
    lane = lax.broadcasted_iota(I32, (tm, LANES), 1)
    ids = ids_ref[...]
    e1 = ids[:, 0:1]
    e2 = ids[:, 1:2]
    oh1 = (lane == e1).astype(F32)
    oh2 = (lane == e2).astype(F32)
    oh = oh1 + oh2

    @pl.when((phase == 0) & (i == 0))
    def _():
        cnt[...] = jnp.zeros(cnt.shape, F32)

    @pl.when(phase == 0)
    def _():
        cnt[...] = cnt[...] + jnp.sum(oh, axis=0, keepdims=True)

    @pl.when((phase == 1) & (i == 0))
    def _():
        padded = jnp.floor((cnt[...] + (MOE_TILE - 1)) * (1.0 / MOE_TILE)) * MOE_TILE
        p8 = jnp.broadcast_to(padded, (SUBLANES, LANES))
        st = jnp.dot(p8, upper_ref[...], precision=HIGHEST, preferred_element_type=F32)
        start[...] = st[0:1, :]
        carry[...] = jnp.zeros(carry.shape, F32)
        end = st[0:1, :] + padded
        trow = lax.broadcasted_iota(I32, (LANES, LANES), 0).astype(F32) * MOE_TILE
        tlane = lax.broadcasted_iota(I32, (LANES, LANES), 1)
        done = jnp.where((end <= trow) & (tlane < N_EXPERTS), 1.0, 0.0)
        tile_ref[...] = jnp.broadcast_to(jnp.sum(done, axis=-1, keepdims=True), (LANES, LANES)).astype(I32)

    @pl.when(phase == 1)
    def _():
        incl = jnp.dot(tril_ref[...], oh.astype(BF16), preferred_element_type=F32) + carry[...]
        base = start[...] + incl - oh
        p1 = jnp.sum(oh1 * base, axis=-1, keepdims=True)
        p2 = jnp.sum(oh2 * base, axis=-1, keepdims=True)
        pos_ref[...] = jnp.where(lane == 0, p1, jnp.where(lane == 1, p2, 0.0)).astype(I32)
        carry[...] = carry[...] + jnp.sum(oh, axis=0, keepdims=True)


def _plan(ids):
    t = ids.shape[0]
    tm = TOK_TILE
    tril = jnp.tril(jnp.ones((tm, tm), BF16))
    upper = jnp.triu(jnp.ones((LANES, LANES), F32), k=1)
    return pl.pallas_call(
        _plan_kernel,
        grid=(2, t // tm),
        in_specs=[pl.BlockSpec((tm, LANES), lambda ph, i: (i, 0)),
                  pl.BlockSpec((tm, tm), lambda ph, i: (0, 0)),
                  pl.BlockSpec((LANES, LANES), lambda ph, i: (0, 0))],
        out_specs=[pl.BlockSpec((tm, LANES), lambda ph, i: (i * ph, 0)),
                   pl.BlockSpec((LANES, LANES), lambda ph, i: (0, 0))],
        out_shape=[jax.ShapeDtypeStruct((t, LANES), I32), jax.ShapeDtypeStruct((LANES, LANES), I32)],
        scratch_shapes=[pltpu.VMEM((1, LANES), F32), pltpu.VMEM((1, LANES), F32), pltpu.VMEM((1, LANES), F32)],
        compiler_params=_cparams(2),
        name="route_plan",
    )(ids, tril, upper)


def _dispatch_kernel(pos_ref, x_ref, xs_ref, sem):
    i = pl.program_id(0)
    tm = x_ref.shape[0]
    t = pl.num_programs(0) * tm

    def copy(m, k):
        dst = pos_ref[k * t + i * tm + m]
        return pltpu.make_async_copy(x_ref.at[pl.ds(m, 1)], xs_ref.at[pl.ds(dst, 1)], sem)

    def start(m, c):
        copy(m, 0).start()
        copy(m, 1).start()
        return c

    lax.fori_loop(0, tm, start, 0, unroll=8)
    for _ in range(TOP_K):
        pltpu.make_async_copy(x_ref, xs_ref.at[pl.ds(0, tm)], sem).wait()


def _dispatch(pos_flat, x1, n_rows):
    t, d = x1.shape
    tm = TOK_TILE
    return pl.pallas_call(
        _dispatch_kernel,
        grid_spec=pltpu.PrefetchScalarGridSpec(
            num_scalar_prefetch=1,
            grid=(t // tm,),
            in_specs=[pl.BlockSpec((tm, d), lambda i, pos: (i, 0))],
            out_specs=pl.BlockSpec(memory_space=pl.ANY),
            scratch_shapes=[pltpu.SemaphoreType.DMA],
        ),
        out_shape=jax.ShapeDtypeStruct((n_rows, d), F32),
        compiler_params=_cparams(1),
        name="moe_dispatch",
    )(pos_flat, x1)


def _expert_kernel(te_ref, na_ref, xs_ref, wg_ref, wu_ref, wd_ref, ys_ref):
    i = pl.program_id(0)

    @pl.when(i < na_ref[0])
    def _():
        xb = xs_ref[...].astype(BF16)
        g = jnp.dot(xb, wg_ref[0, 0].astype(BF16), preferred_element_type=F32)
        u = jnp.dot(xb, wu_ref[0, 0].astype(BF16), preferred_element_type=F32)
        h = (_silu(g) * u).astype(BF16)
        ys_ref[...] = jnp.dot(h, wd_ref[0, 0].astype(BF16), preferred_element_type=F32)


def _experts(layer, tile_expert, n_active, xs, w_gate, w_up, w_down):
    n_rows, d = xs.shape
    n_tiles = n_rows // MOE_TILE

    def rows(i, te, na):
        return (jnp.minimum(i, na[0] - 1), 0)

    def expert(i, te, na):
        return (layer, te[i], 0, 0)

    return pl.pallas_call(
        _expert_kernel,
        grid_spec=pltpu.PrefetchScalarGridSpec(
            num_scalar_prefetch=2,
            grid=(n_tiles,),
            in_specs=[pl.BlockSpec((MOE_TILE, d), rows),
                      pl.BlockSpec((1, 1, d, D_EXPERT), expert),
                      pl.BlockSpec((1, 1, d, D_EXPERT), expert),
                      pl.BlockSpec((1, 1, D_EXPERT, d), expert)],
            out_specs=pl.BlockSpec((MOE_TILE, d), rows),
        ),
        out_shape=jax.ShapeDtypeStruct((n_rows, d), F32),
        compiler_params=_cparams(1),
        name="moe_experts",
    )(tile_expert, n_active, xs, w_gate, w_up, w_down)


def _final_kernel(pos_ref, x1_ref, x1b_ref, gates_ref, p_ref, wg_ref, bg_ref, wp_ref, g_ref, b_ref, ys_ref,
                  x2_ref, x2b_ref, ybuf, sem):
    i = pl.program_id(0)
    tm = x1_ref.shape[0]
    t = pl.num_programs(0) * tm

    def copy(m, k):
        src = pos_ref[k * t + i * tm + m]
        return pltpu.make_async_copy(ys_ref.at[pl.ds(src, 1)], ybuf.at[k, pl.ds(m, 1)], sem)

    def start(m, c):
        copy(m, 0).start()
        copy(m, 1).start()
        return c

    lax.fori_loop(0, tm, start, 0, unroll=8)
    gate = _sigmoid(jnp.dot(x1b_ref[...], wg_ref[0], preferred_element_type=F32) + bg_ref[0])
    ple = gate * jnp.dot(p_ref[0].astype(BF16), wp_ref[0], preferred_element_type=F32)
    for k in range(TOP_K):
        pltpu.make_async_copy(ys_ref.at[pl.ds(0, tm)], ybuf.at[k], sem).wait()
    gates = gates_ref[...]
    ffn = gates[:, 0:1] * ybuf[0] + gates[:, 1:2] * ybuf[1]
    x2 = _layer_norm(DN_ALPHA * x1_ref[...] + ffn + ple, g_ref[0], b_ref[0])
    x2_ref[...] = x2
    x2b_ref[...] = x2.astype(BF16)


def _final(layer, pos_flat, x1, x1b, gates, p, wg_bf, bg, wp_bf, ln_g, ln_b, ys):
    t, d = x1.shape
    tm = TOK_TILE

    def row(w):
        return pl.BlockSpec((tm, w), lambda i, pos: (i, 0))

    def per_layer(shape):
        return pl.BlockSpec((1,) + shape, lambda i, pos: (layer,) + (0,) * len(shape))

    return pl.pallas_call(
        _final_kernel,
        grid_spec=pltpu.PrefetchScalarGridSpec(
            num_scalar_prefetch=1,
            grid=(t // tm,),
            in_specs=[row(d), row(d), row(LANES), pl.BlockSpec((1, tm, PLE_DIM), lambda i, pos: (layer, i, 0)),
                      per_layer((d, d)), per_layer((1, d)), per_layer((PLE_DIM, d)),
                      per_layer((1, d)), per_layer((1, d)), pl.BlockSpec(memory_space=pl.ANY)],
            out_specs=[row(d), row(d)],
            scratch_shapes=[pltpu.VMEM((TOP_K, tm, d), F32), pltpu.SemaphoreType.DMA],
        ),
        out_shape=[jax.ShapeDtypeStruct((t, d), F32), jax.ShapeDtypeStruct((t, d), BF16)],
        compiler_params=_cparams(1),
        name="combine_ple_ln",
    )(pos_flat, x1, x1b, gates, p, wg_bf, bg, wp_bf, ln_g, ln_b, ys)


def _arrange_w_in(w):
    offs = [0]
    for s in IN_SIZES:
        offs.append(offs[-1] + s)
    seg = [w[..., offs[i]:offs[i + 1]] for i in range(len(IN_SIZES))]
    a_u, gq, gk, gv, gl, gr, cq, ck, cv, sb, sc, sh = seg
    pad = jnp.zeros(w.shape[:-1] + (LANES - GLA_RANK,), w.dtype)
    return jnp.concatenate([a_u, gv, gr, cq, ck, cv, sb, sc, sh, gq, gk, gl, pad], axis=-1).astype(BF16)


def kernel(x, p, w_in, conf_dw_w, conf_dw_b, conf_ln_g, conf_ln_b, gla_w_g2, gla_b_g2, gla_norm_g, sc_conv_w, mix_scale, w_out, rel_bias, ln1_g, ln1_b, router_g_w, router_g_b, router_e_w, router_e_b, exp_w_gate, exp_w_up, exp_w_down, ple_w_gate, ple_b_gate, ple_w_proj, ln2_g, ln2_b):
    bsz, seq, d = x.shape
    t = bsz * seq
    n_rows = TOP_K * t + N_EXPERTS * MOE_TILE
    n_tiles = n_rows // MOE_TILE
    bias_tab = _bias_tables(rel_bias)
    w_in_bf = _arrange_w_in(w_in)
    w_out_bf = w_out.astype(BF16)
    ple_wg_bf = ple_w_gate.astype(BF16)
    ple_wp_bf = ple_w_proj.astype(BF16)
    p3 = p.reshape(DEPTH, t, PLE_DIM)

    def vec(a):
        return a.reshape(DEPTH, 1, -1)

    xf = x.reshape(t, d)
    xb = xf
    for i in range(DEPTH):
        u = _in_proj(xb, w_in_bf, i, 512, 1920, F32)
        u3 = u.reshape(bsz, seq, U_COLS)
        ya, yd = _conv_mixers(u3, conf_dw_w[i], conf_dw_b[i], conf_ln_g[i], conf_ln_b[i], sc_conv_w[i])
        yb = _gla(u3, gla_w_g2[i], gla_b_g2[i], gla_norm_g[i])
        yc = _dilated(u3, bias_tab)
        rw = jnp.zeros((d, LANES), F32).at[:, :N_GROUPS].set(router_g_w[i]).at[:, N_GROUPS:N_GROUPS + N_EXPERTS].set(router_e_w[i])
        rb = jnp.zeros((1, LANES), F32).at[0, :N_GROUPS].set(router_g_b[i]).at[0, N_GROUPS:N_GROUPS + N_EXPERTS].set(router_e_b[i])
        rw_hi = rw.astype(BF16)
        rw_lo = (rw - rw_hi.astype(F32)).astype(BF16)
        x1, x1b, ids, gates = _outproj(i, xf, ya.reshape(t, -1), yb.reshape(t, -1), yc.reshape(t, -1), yd.reshape(t, -1),
                                       vec(mix_scale), w_out_bf, vec(ln1_g), vec(ln1_b), rw_hi, rw_lo, rb)
        pos, tile_tab = _plan(ids)
        pos_flat = pos[:, :TOP_K].T.reshape(-1)
        tile_end = tile_tab[:n_tiles, 0]
        n_active = jnp.sum((tile_end < N_EXPERTS).astype(I32)).reshape(1)
        tile_expert = jnp.minimum(tile_end, N_EXPERTS - 1)
        xs = _dispatch(pos_flat, x1, n_rows)
        ys = _experts(i, tile_expert, n_active, xs, exp_w_gate, exp_w_up, exp_w_down)
        xf, xb = _final(i, pos_flat, x1, x1b, gates, p3, ple_wg_bf, vec(ple_b_gate), ple_wp_bf, vec(ln2_g), vec(ln2_b), ys)
    return xf.reshape(bsz, seq, d)
```

```python
import functools
import math

import jax
import jax.numpy as jnp
from jax import lax
from jax.experimental import pallas as pl
from jax.experimental.pallas import tpu as pltpu

F32 = jnp.float32
BF16 = jnp.bfloat16
I32 = jnp.int32
HIGHEST = lax.Precision.HIGHEST

D_MODEL = 2048
DEPTH = 2
PLE_DIM = 256
W_CONF = 512
W_GLA = 512
W_DIL = 512
W_SC = 512
CONF_KERNEL = 31
GLA_HEADS = 4
GLA_DV = 128
GLA_DK = 64
GLA_RANK = 16
GLA_TAU = 16.0
DIL_HEADS = 8
DIL_HD = 64
DIL_BRANCHES = ((128, 1), (512, 4), (2048, 16))
DIL_BLOCK = 128
REL_BUCKETS = 32
REL_MAX_DIST = 2048
SC_KERNEL = 3
N_GROUPS = 4
EXPERTS_PER_GROUP = 8
N_EXPERTS = 32
TOP_K = 2
D_EXPERT = 256
DN_ALPHA = (2 * DEPTH) ** 0.25
LN_EPS = 1e-5
RMS_EPS = 1e-6
IN_SIZES = (2 * W_CONF, GLA_HEADS * GLA_DK, GLA_HEADS * GLA_DK, W_GLA, GLA_RANK, W_GLA,
            W_DIL, W_DIL, W_DIL, W_SC, W_SC, W_SC)

LANES = 128
SUBLANES = 8
VMEM_LIMIT = 56 * 1024 * 1024

GL_PAD = 512
COL_A = 0
COL_GQ = 1024
COL_GK = 1280
COL_GV = 1536
COL_GL = 2048
COL_GR = 2560
COL_CQ = 3072
COL_CK = 3584
COL_CV = 4096
COL_SB = 4608
COL_SC = 5120
COL_SH = 5632
U_COLS = 6144

MOE_TILE = 256


def _cparams(n_axes):
    return pltpu.CompilerParams(dimension_semantics=("arbitrary",) * n_axes, vmem_limit_bytes=VMEM_LIMIT)


def _layer_norm(x, g, b):
    mu = jnp.mean(x, axis=-1, keepdims=True)
    xc = x - mu
    var = jnp.mean(xc * xc, axis=-1, keepdims=True)
    return xc * lax.rsqrt(var + LN_EPS) * g + b


def _sigmoid(x):
    return 1.0 / (1.0 + jnp.exp(-x))


def _silu(x):
    return x * _sigmoid(x)


def _matmul_kernel(x_ref, w_ref, o_ref):
    o_ref[...] = jnp.dot(x_ref[...].astype(BF16), w_ref[0],
                         preferred_element_type=F32).astype(o_ref.dtype)


def _in_proj(x, w_all, layer, tm, tn, out_dtype):
    m, k = x.shape
    n = w_all.shape[2]
    return pl.pallas_call(
        _matmul_kernel,
        grid=(n // tn, m // tm),
        in_specs=[pl.BlockSpec((tm, k), lambda j, i: (i, 0)),
                  pl.BlockSpec((1, k, tn), lambda j, i: (layer, 0, j))],
        out_specs=pl.BlockSpec((tm, tn), lambda j, i: (i, j)),
        out_shape=jax.ShapeDtypeStruct((m, n), out_dtype),
        compiler_params=_cparams(2),
        name="in_proj",
    )(x, w_all)


CONV_TS = 512
CONV_RC = 64
CONF_HIST = 32
SC_HIST = 8


def _conv_kernel(a_ref, gate_ref, sb_ref, sc_ref, sh_ref, cw_ref, cb_ref, lg_ref, lb_ref, sw_ref,
                 ya_ref, yd_ref, ha, hd, hs):
    s = pl.program_id(1)
    ts = a_ref.shape[1]
    n_ext = CONF_HIST + ts

    @pl.when(s == 0)
    def _():
        ha[0:CONF_HIST, :] = jnp.zeros((CONF_HIST, W_CONF), F32)
        hd[0:SC_HIST, :] = jnp.zeros((SC_HIST, W_SC), F32)

    ha[CONF_HIST:CONF_HIST + ts, :] = a_ref[0] * _sigmoid(gate_ref[0])
    hd[SC_HIST:SC_HIST + ts, :] = sc_ref[0] * sh_ref[0]

    cb = cb_ref[...]
    lg = lg_ref[...]
    lb = lb_ref[...]
    off_a = CONF_HIST - (CONF_KERNEL - 1)
    h_all = ha[...]
    for sh in range(1, SUBLANES):
        hs[sh - 1] = pltpu.roll(h_all, n_ext - sh, 0)
    for c in range(ts // CONV_RC):
        r0 = c * CONV_RC
        acc = jnp.zeros((CONV_RC, W_CONF), F32)
        for j in range(CONF_KERNEL):
            sh = (off_a + j) % SUBLANES
            lo = r0 + off_a + j - sh
            rows = ha[lo:lo + CONV_RC, :] if sh == 0 else hs[sh - 1, lo:lo + CONV_RC, :]
            acc = acc + cw_ref[j:j + 1, :] * rows
        hh = _layer_norm(acc + cb, lg, lb)
        ya_ref[0, r0:r0 + CONV_RC, :] = _silu(hh)

    off_d = SC_HIST - (SC_KERNEL - 1)
    accd = jnp.zeros((ts, W_SC), F32)
    for j in range(SC_KERNEL):
        accd = accd + sw_ref[j:j + 1, :] * hd[off_d + j:off_d + j + ts, :]
    yd_ref[0] = sb_ref[0] * accd

    ha[0:CONF_HIST, :] = ha[ts:ts + CONF_HIST, :]
    hd[0:SC_HIST, :] = hd[ts:ts + SC_HIST, :]


def _conv_mixers(u3, conf_w, conf_b, conf_lg, conf_lb, sc_w):
    bsz, seq, _ = u3.shape
    ts = CONV_TS
    wblk = W_CONF

    def col(c):
        return pl.BlockSpec((1, ts, wblk), lambda b, s, c=c: (b, s, c // wblk))

    def full(shape):
        return pl.BlockSpec(shape, lambda b, s: (0,) * len(shape))

    cw = jnp.zeros((32, W_CONF), F32).at[:CONF_KERNEL].set(conf_w)
    sw = jnp.zeros((8, W_SC), F32).at[:SC_KERNEL].set(sc_w)
    out_spec = pl.BlockSpec((1, ts, wblk), lambda b, s: (b, s, 0))
    return pl.pallas_call(
        _conv_kernel,
        grid=(bsz, seq // ts),
        in_specs=[col(COL_A), col(COL_A + W_CONF), col(COL_SB), col(COL_SC), col(COL_SH),
                  full((32, W_CONF)), full((1, W_CONF)), full((1, W_CONF)), full((1, W_CONF)), full((8, W_SC))],
        out_specs=[out_spec, out_spec],
        out_shape=[jax.ShapeDtypeStruct((bsz, seq, W_CONF), F32), jax.ShapeDtypeStruct((bsz, seq, W_SC), F32)],
        scratch_shapes=[pltpu.VMEM((CONF_HIST + ts, W_CONF), F32), pltpu.VMEM((SC_HIST + ts, W_SC), F32),
                        pltpu.VMEM((SUBLANES - 1, CONF_HIST + ts, W_CONF), F32)],
        compiler_params=_cparams(2),
        name="conv_mixers",
    )(u3, u3, u3, u3, u3, cw, conf_b.reshape(1, -1), conf_lg.reshape(1, -1), conf_lb.reshape(1, -1), sw)


GLA_BLK = 128
GLA_SUB = 64
GLA_UNROLL = 4


def _split3(x):
    hi = x.astype(BF16)
    r1 = x - hi.astype(F32)
    mid = r1.astype(BF16)
    lo = (r1 - mid.astype(F32)).astype(BF16)
    return hi, mid, lo


def _gla_kernel(q_ref, k_ref, v_ref, gl_ref, r_ref, wg2h_ref, wg2l_ref, bg2_ref, ng_ref, tril_ref, o_ref,
                st_ref, la_ref):
    seq = q_ref.shape[1]
    nblk = seq // GLA_BLK
    nsub = GLA_BLK // GLA_SUB
    assert nsub == 2
    st_ref[...] = jnp.zeros(st_ref.shape, F32)

    lane = lax.broadcasted_iota(I32, (1, LANES), 1)
    head_mask = [(lane < GLA_DK).astype(F32), (lane >= GLA_DK).astype(F32)]
    row = lax.broadcasted_iota(I32, (GLA_BLK, GLA_BLK), 0)
    col = lax.broadcasted_iota(I32, (GLA_BLK, GLA_BLK), 1)
    sub_shift = GLA_SUB.bit_length() - 1
    diag_mask = ((row >> sub_shift) == (col >> sub_shift)) & (col <= row)
    second_half = lax.broadcasted_iota(I32, (GLA_BLK, 1), 0) >= GLA_SUB
    scale = GLA_DK ** -0.5
    ng = ng_ref[...]
    tril = tril_ref[...]

    gl = gl_ref[0]
    gl_hi = gl.astype(BF16)
    gl_lo = (gl - gl_hi.astype(F32)).astype(BF16)
    wh = wg2h_ref[...]
    z = (jnp.dot(gl_hi, wh, preferred_element_type=F32) + jnp.dot(gl_lo, wh, preferred_element_type=F32)
         + jnp.dot(gl_hi, wg2l_ref[...], preferred_element_type=F32) + bg2_ref[...])
    la_ref[...] = (jnp.minimum(z, 0.0) - jnp.log(1.0 + jnp.exp(-jnp.abs(z)))) * (1.0 / GLA_TAU)

    def body(s, carry):
        r0 = pl.multiple_of(s * GLA_BLK, GLA_BLK)
        q = q_ref[0, pl.ds(r0, GLA_BLK), :] * scale
        k = k_ref[0, pl.ds(r0, GLA_BLK), :]
        v = v_ref[0, pl.ds(r0, GLA_BLK), :].astype(BF16)
        b = None
        for part in _split3(la_ref[pl.ds(r0, GLA_BLK), :]):
            t = jnp.dot(tril, part, preferred_element_type=F32)
            b = t if b is None else b + t

        b3 = b.reshape(nsub, GLA_SUB, LANES)
        mid = jnp.broadcast_to(b3[:, GLA_SUB // 2 - 1:GLA_SUB // 2, :], b3.shape).reshape(GLA_BLK, LANES)
        qd = q * jnp.exp(b - mid)
        kd = (k * jnp.exp(mid - b)).astype(BF16)
        e0 = b[GLA_SUB - 1:GLA_SUB, :]
        q_off = jnp.where(second_half, q * jnp.exp(jnp.minimum(b - e0, 0.0)), 0.0)
        k_off = jnp.where(second_half, 0.0, k * jnp.exp(jnp.minimum(e0 - b, 0.0))).astype(BF16)
        q_st = q * jnp.exp(b)
        b_last = b[GLA_BLK - 1:GLA_BLK, :]
        kk = (k * jnp.exp(b_last - b)).astype(BF16)
        st = st_ref[...]
        st_bf = st.astype(BF16)

        outs = []
        for h in range(2):
            hm = head_mask[h]
            s_d = lax.dot_general((qd * hm).astype(BF16), kd, (((1,), (1,)), ((), ())), preferred_element_type=F32)
            s_o = lax.dot_general((q_off * hm).astype(BF16), k_off, (((1,), (1,)), ((), ())),
                                  preferred_element_type=F32)
            scores = jnp.where(diag_mask, s_d, 0.0) + s_o
            lhs = jnp.concatenate([scores.astype(BF16), (q_st * hm).astype(BF16)], axis=1)
            rhs = jnp.concatenate([v[:, h * GLA_DV:(h + 1) * GLA_DV], st_bf[:, h * GLA_DV:(h + 1) * GLA_DV]], axis=0)
            o_h = jnp.dot(lhs, rhs, preferred_element_type=F32)
            o_h = o_h * lax.rsqrt(jnp.mean(o_h * o_h, axis=-1, keepdims=True) + RMS_EPS)
            outs.append(o_h)
        o = jnp.concatenate(outs, axis=1) * ng
        o_ref[0, pl.ds(r0, GLA_BLK), :] = o * _silu(r_ref[0, pl.ds(r0, GLA_BLK), :])

        upd = lax.dot_general(kk, v, (((0,), (0,)), ((), ())), preferred_element_type=F32)
        decay = jnp.transpose(jnp.broadcast_to(jnp.exp(b_last), (GLA_BLK, LANES)))
        st_ref[...] = st * jnp.concatenate([decay, decay], axis=1) + upd
        return carry

    lax.fori_loop(0, nblk, body, 0, unroll=GLA_UNROLL)


def _gla(u3, w_g2, b_g2, norm_g):
    bsz, seq, _ = u3.shape
    npair = GLA_HEADS // 2
    kw = 2 * GLA_DK
    vw = 2 * GLA_DV

    def col(c, w):
        return pl.BlockSpec((1, seq, w), lambda b, p, c=c, w=w: (b, 0, c // w + p))

    wg2 = jnp.zeros((LANES, GLA_HEADS * GLA_DK), F32).at[:GLA_RANK].set(w_g2)
    wg2_hi = wg2.astype(BF16)
    wg2_lo = (wg2 - wg2_hi.astype(F32)).astype(BF16)
    ng = jnp.concatenate([norm_g, norm_g]).reshape(1, vw)
    tril = jnp.tril(jnp.ones((GLA_BLK, GLA_BLK), BF16))
    return pl.pallas_call(
        _gla_kernel,
        grid=(bsz, npair),
        in_specs=[col(COL_GQ, kw), col(COL_GK, kw), col(COL_GV, vw),
                  pl.BlockSpec((1, seq, LANES), lambda b, p: (b, 0, COL_GL // LANES)),
                  col(COL_GR, vw),
                  pl.BlockSpec((LANES, kw), lambda b, p: (0, p)),
                  pl.BlockSpec((LANES, kw), lambda b, p: (0, p)),
                  pl.BlockSpec((1, kw), lambda b, p: (0, p)),
                  pl.BlockSpec((1, vw), lambda b, p: (0, 0)),
                  pl.BlockSpec((GLA_BLK, GLA_BLK), lambda b, p: (0, 0))],
        out_specs=pl.BlockSpec((1, seq, vw), lambda b, p: (b, 0, p)),
        out_shape=jax.ShapeDtypeStruct((bsz, seq, W_GLA), F32),
        scratch_shapes=[pltpu.VMEM((kw, vw), F32), pltpu.VMEM((seq, LANES), F32)],
        compiler_params=_cparams(2),
        name="gla",
    )(u3, u3, u3, u3, u3, wg2_hi, wg2_lo, b_g2.reshape(1, -1), ng, tril)


def _t5_bucket(dist):
    max_exact = REL_BUCKETS // 2
    large = max_exact + (jnp.log(jnp.maximum(dist, 1).astype(F32) / max_exact)
                         / math.log(REL_MAX_DIST / max_exact) * (REL_BUCKETS - max_exact)).astype(I32)
    large = jnp.minimum(large, REL_BUCKETS - 1)
    return jnp.where(dist < max_exact, dist, large)


LOG2E = math.log2(math.e)
DIL_GROUP = 4


def _bias_kernel(rel_ref, bucket_ref, o_ref):
    h = pl.program_id(1)
    blk = DIL_BLOCK
    bucket = bucket_ref[0]
    acc = jnp.zeros(bucket.shape, F32)
    for b in range(REL_BUCKETS):
        acc = jnp.where(bucket == b, rel_ref[b, h], acc)
    acc = acc * LOG2E
    qi = lax.broadcasted_iota(I32, (blk, 2 * blk), 0)
    kj = lax.broadcasted_iota(I32, (blk, 2 * blk), 1)
    steps = qi + blk - kj
    in_window = (steps >= 0) & (steps <= blk)
    neg = jnp.float32(-jnp.inf)
    o_ref[0, 0, 0] = jnp.where(in_window, acc, neg)
    o_ref[0, 0, 1] = jnp.where(in_window & (kj >= blk), acc, neg)


def _bias_tables(rel_bias):
    assert all(window // dil == DIL_BLOCK for window, dil in DIL_BRANCHES)
    qi = jnp.arange(DIL_BLOCK)[:, None]
    kj = jnp.arange(2 * DIL_BLOCK)[None, :]
    steps = jnp.maximum(qi + DIL_BLOCK - kj, 0)
    buckets = jnp.stack([_t5_bucket(steps * dil) for _, dil in DIL_BRANCHES]).astype(I32)
    nbr = len(DIL_BRANCHES)
    return pl.pallas_call(
        _bias_kernel,
        grid=(nbr, DIL_HEADS),
        in_specs=[pl.BlockSpec(memory_space=pltpu.SMEM),
                  pl.BlockSpec((1, DIL_BLOCK, 2 * DIL_BLOCK), lambda i, h: (i, 0, 0))],
        out_specs=pl.BlockSpec((1, 1, 2, DIL_BLOCK, 2 * DIL_BLOCK), lambda i, h: (i, h, 0, 0, 0)),
        out_shape=jax.ShapeDtypeStruct((nbr, DIL_HEADS, 2, DIL_BLOCK, 2 * DIL_BLOCK), F32),
        compiler_params=_cparams(2),
        name="rel_bias_tables",
    )(rel_bias, buckets)


def _dil_kernel(q_ref, k_ref, v_ref, bias_ref, o_ref, qs, ks, vs, ob, lb):
    seq = q_ref.shape[1]
    blk = DIL_BLOCK
    scale = DIL_HD ** -0.5 * LOG2E
    lane = lax.broadcasted_iota(I32, (1, LANES), 1)
    first = lane < DIL_HD
    head_mask = [first.astype(F32), 1.0 - first.astype(F32)]

    ks[0:blk, :] = jnp.zeros((blk, LANES), BF16)
    vs[0:blk, :] = jnp.zeros((blk, LANES), BF16)

    for bi, (window, dil) in enumerate(DIL_BRANCHES):
        sub_len = seq // dil
        nb = sub_len // blk
        nb_shift = nb.bit_length() - 1
        for r in range(dil):
            rows = pl.ds(r, sub_len, stride=dil) if dil > 1 else pl.ds(0, sub_len)
            q = q_ref[0, rows, :] * scale
            for h in range(2):
                qs[h, r * sub_len:(r + 1) * sub_len, :] = (q * head_mask[h]).astype(BF16)
            ks[blk + r * sub_len:blk + (r + 1) * sub_len, :] = k_ref[0, rows, :].astype(BF16)
            vs[blk + r * sub_len:blk + (r + 1) * sub_len, :] = v_ref[0, rows, :].astype(BF16)

        def body(it, carry, bi=bi, dil=dil, nb=nb, nb_shift=nb_shift):
            for gg in range(DIL_GROUP):
                g = it * DIL_GROUP + gg
                n = g & (nb - 1)
                r = g >> nb_shift
                g0 = pl.multiple_of(g * blk, blk)
                kb = ks[pl.ds(g0, 2 * blk), :]
                vb = vs[pl.ds(g0, 2 * blk), :]
                variant = jnp.where(n == 0, 1, 0)
                o_acc = None
                l_acc = None
                for h in range(2):
                    sc = lax.dot_general(qs[h, pl.ds(g0, blk), :], kb, (((1,), (1,)), ((), ())),
                                         preferred_element_type=F32)
                    sc = sc + bias_ref[bi, h, variant]
                    m = jnp.max(sc, axis=-1, keepdims=True)
                    p = jnp.exp2(sc - m)
                    l = jnp.sum(p, axis=-1, keepdims=True)
                    o_h = jnp.dot(p.astype(BF16), vb, preferred_element_type=F32) * (1.0 / l)
                    lse = jnp.broadcast_to(m + jnp.log2(l), (blk, LANES))
                    if h == 0:
                        o_acc, l_acc = o_h, lse
                    else:
                        o_acc = jnp.where(first, o_acc, o_h)
                        l_acc = jnp.where(first, l_acc, lse)
                if dil > 1:
                    dst = pl.ds(n * (blk * dil) + r, blk, stride=dil)
                else:
                    dst = pl.ds(g0, blk)
                ob[bi, dst, :] = o_acc
                lb[bi, dst, :] = l_acc
            return carry

        lax.fori_loop(0, seq // blk // DIL_GROUP, body, 0)

    nbr = len(DIL_BRANCHES)
    mc = 256
    for c in range(seq // mc):
        rs = slice(c * mc, (c + 1) * mc)
        ls = [lb[i, rs, :] for i in range(nbr)]
        m = functools.reduce(jnp.maximum, ls)
        ws = [jnp.exp2(l - m) for l in ls]
        num = functools.reduce(lambda a, b: a + b, [w * ob[i, rs, :] for i, w in enumerate(ws)])
        den = functools.reduce(lambda a, b: a + b, ws)
        o_ref[0, rs, :] = num / den


def _dilated(u3, bias_tab):
    bsz, seq, _ = u3.shape
    npair = DIL_HEADS // 2
    nbr = len(DIL_BRANCHES)

    def col(c):
        return pl.BlockSpec((1, seq, LANES), lambda b, p, c=c: (b, 0, c // LANES + p))

    return pl.pallas_call(
        _dil_kernel,
        grid=(bsz, npair),
        in_specs=[col(COL_CQ), col(COL_CK), col(COL_CV),
                  pl.BlockSpec((nbr, 2, 2, DIL_BLOCK, 2 * DIL_BLOCK), lambda b, p: (0, p, 0, 0, 0))],
        out_specs=pl.BlockSpec((1, seq, LANES), lambda b, p: (b, 0, p)),
        out_shape=jax.ShapeDtypeStruct((bsz, seq, W_DIL), F32),
        scratch_shapes=[pltpu.VMEM((2, seq, LANES), BF16),
                        pltpu.VMEM((DIL_BLOCK + seq, LANES), BF16),
                        pltpu.VMEM((DIL_BLOCK + seq, LANES), BF16),
                        pltpu.VMEM((nbr, seq, LANES), F32),
                        pltpu.VMEM((nbr, seq, LANES), F32)],
        compiler_params=_cparams(2),
        name="dilated_attn",
    )(u3, u3, u3, bias_tab)


TOK_TILE = 256
OUT_TILE = 512
PLAN_TILE = 1024


def _outproj_kernel(x_ref, ya_ref, yb_ref, yc_ref, yd_ref, ms_ref, wo_ref, g_ref, b_ref, rwh_ref, rwl_ref, rb_ref,
                    x1_ref, x1b_ref, ids_ref, gates_ref):
    acc = None
    for gi, y_ref in enumerate((ya_ref, yb_ref, yc_ref, yd_ref)):
        lo = gi * W_CONF
        y = (y_ref[...] * ms_ref[0, :, lo:lo + W_CONF]).astype(BF16)
        part = jnp.dot(y, wo_ref[0, lo:lo + W_CONF, :], preferred_element_type=F32)
        acc = part if acc is None else acc + part
    x1 = _layer_norm(DN_ALPHA * x_ref[...] + acc, g_ref[0], b_ref[0])
    x1_ref[...] = x1
    x1_hi = x1.astype(BF16)
    x1b_ref[...] = x1_hi

    x1_lo = (x1 - x1_hi.astype(F32)).astype(BF16)
    rwh = rwh_ref[...]
    logits = (jnp.dot(x1_hi, rwh, preferred_element_type=F32) + jnp.dot(x1_lo, rwh, preferred_element_type=F32)
              + jnp.dot(x1_hi, rwl_ref[...], preferred_element_type=F32) + rb_ref[...])
    tm = logits.shape[0]
    lane = lax.broadcasted_iota(I32, (tm, LANES), 1)
    neg = jnp.float32(-jnp.inf)
    gl = jnp.where(lane < N_GROUPS, logits, neg)
    gmax = jnp.max(gl, axis=-1, keepdims=True)
    g_top = 1.0 / jnp.sum(jnp.exp(gl - gmax), axis=-1, keepdims=True)
    g_idx = jnp.min(jnp.where(gl == gmax, lane, LANES), axis=-1, keepdims=True)
    e_lo = N_GROUPS + g_idx * EXPERTS_PER_GROUP
    el = jnp.where((lane >= e_lo) & (lane < e_lo + EXPERTS_PER_GROUP), logits, neg)
    m1 = jnp.max(el, axis=-1, keepdims=True)
    i1 = jnp.min(jnp.where(el == m1, lane, LANES), axis=-1, keepdims=True)
    el2 = jnp.where(lane == i1, neg, el)
    m2 = jnp.max(el2, axis=-1, keepdims=True)
    i2 = jnp.min(jnp.where(el2 == m2, lane, LANES), axis=-1, keepdims=True)
    e21 = jnp.exp(m2 - m1)
    w1 = g_top / (1.0 + e21)
    w2 = g_top * e21 / (1.0 + e21)
    ids_ref[...] = jnp.where(lane == 0, i1 - N_GROUPS, jnp.where(lane == 1, i2 - N_GROUPS, N_EXPERTS))
    gates_ref[...] = jnp.where(lane == 0, w1, jnp.where(lane == 1, w2, 0.0))


def _outproj(layer, x, ya, yb, yc, yd, mix_scale, w_out_bf, ln_g, ln_b, rw_hi, rw_lo, rb):
    t, d = x.shape
    tm = OUT_TILE

    def row(w):
        return pl.BlockSpec((tm, w), lambda i: (i, 0))

    def full(shape):
        return pl.BlockSpec(shape, lambda i: (0,) * len(shape))

    def per_layer(shape):
        return pl.BlockSpec((1,) + shape, lambda i: (layer,) + (0,) * len(shape))

    return pl.pallas_call(
        _outproj_kernel,
        grid=(t // tm,),
        in_specs=[row(d), row(W_CONF), row(W_GLA), row(W_DIL), row(W_SC), per_layer((1, d)), per_layer((d, d)),
                  per_layer((1, d)), per_layer((1, d)), full((d, LANES)), full((d, LANES)), full((1, LANES))],
        out_specs=[row(d), row(d), row(LANES), row(LANES)],
        out_shape=[jax.ShapeDtypeStruct((t, d), F32), jax.ShapeDtypeStruct((t, d), BF16),
                   jax.ShapeDtypeStruct((t, LANES), I32), jax.ShapeDtypeStruct((t, LANES), F32)],
        compiler_params=_cparams(1),
        name="out_proj_ln_router",
    )(x, ya, yb, yc, yd, mix_scale, w_out_bf, ln_g, ln_b, rw_hi, rw_lo, rb)


def _plan_kernel(ids_ref, tril_ref, upper_ref, pos_ref, tile_ref, cnt, carry, start):
    phase = pl.program_id(0)
    i = pl.program_id(1)
    tm = ids_ref.shape[0]
    lane = lax.broadcasted_iota(I32, (tm, LANES), 1)
    ids = ids_ref[...]
    e1 = ids[:, 0:1]
    e2 = ids[:, 1:2]
    oh1 = (lane == e1).astype(F32)
    oh2 = (lane == e2).astype(F32)
    oh = oh1 + oh2

    @pl.when((phase == 0) & (i == 0))
    def _():
        cnt[...] = jnp.zeros(cnt.shape, F32)

    @pl.when(phase == 0)
    def _():
        cnt[...] = cnt[...] + jnp.sum(oh, axis=0, keepdims=True)

    @pl.when((phase == 1) & (i == 0))
    def _():
        padded = jnp.floor((cnt[...] + (MOE_TILE - 1)) * (1.0 / MOE_TILE)) * MOE_TILE
        p8 = jnp.broadcast_to(padded, (SUBLANES, LANES))
        st = jnp.dot(p8, upper_ref[...], precision=HIGHEST, preferred_element_type=F32)
        start[...] = st[0:1, :]
        carry[...] = jnp.zeros(carry.shape, F32)
        end = st[0:1, :] + padded
        trow = lax.broadcasted_iota(I32, (LANES, LANES), 0).astype(F32) * MOE_TILE
        tlane = lax.broadcasted_iota(I32, (LANES, LANES), 1)
        done = jnp.where((end <= trow) & (tlane < N_EXPERTS), 1.0, 0.0)
        tile_ref[...] = jnp.broadcast_to(jnp.sum(done, axis=-1, keepdims=True), (LANES, LANES)).astype(I32)

    @pl.when(phase == 1)
    def _():
        incl = jnp.dot(tril_ref[...], oh.astype(BF16), preferred_element_type=F32) + carry[...]
        base = start[...] + incl - oh
        p1 = jnp.sum(oh1 * base, axis=-1, keepdims=True)
        p2 = jnp.sum(oh2 * base, axis=-1, keepdims=True)
        pos_ref[...] = jnp.where(lane == 0, p1, jnp.where(lane == 1, p2, 0.0)).astype(I32)
        carry[...] = carry[...] + jnp.sum(oh, axis=0, keepdims=True)


def _plan(ids):
    t = ids.shape[0]
    tm = PLAN_TILE
    tril = jnp.tril(jnp.ones((tm, tm), BF16))
    upper = jnp.triu(jnp.ones((LANES, LANES), F32), k=1)
    return pl.pallas_call(
        _plan_kernel,
        grid=(2, t // tm),
        in_specs=[pl.BlockSpec((tm, LANES), lambda ph, i: (i, 0)),
                  pl.BlockSpec((tm, tm), lambda ph, i: (0, 0)),
                  pl.BlockSpec((LANES, LANES), lambda ph, i: (0, 0))],
        out_specs=[pl.BlockSpec((tm, LANES), lambda ph, i: (i * ph, 0)),
                   pl.BlockSpec((LANES, LANES), lambda ph, i: (0, 0))],
        out_shape=[jax.ShapeDtypeStruct((t, LANES), I32), jax.ShapeDtypeStruct((LANES, LANES), I32)],
        scratch_shapes=[pltpu.VMEM((1, LANES), F32), pltpu.VMEM((1, LANES), F32), pltpu.VMEM((1, LANES), F32)],
        compiler_params=_cparams(2),
        name="route_plan",
    )(ids, tril, upper)


def _dispatch_kernel(pos_ref, x_ref, xs_ref, sem):
    i = pl.program_id(0)
    tm = x_ref.shape[0]
    t = pl.num_programs(0) * tm

    def copy(m, k):
        dst = pos_ref[k * t + i * tm + m]
        return pltpu.make_async_copy(x_ref.at[pl.ds(m, 1)], xs_ref.at[pl.ds(dst, 1)], sem)

    def start(m, c):
        copy(m, 0).start(priority=0)
        copy(m, 1).start(priority=1)
        return c

    lax.fori_loop(0, tm, start, 0, unroll=8)
    for _ in range(TOP_K):
        pltpu.make_async_copy(x_ref, xs_ref.at[pl.ds(0, tm)], sem).wait()


def _dispatch(pos_flat, x1, n_rows):
    t, d = x1.shape
    tm = TOK_TILE
    return pl.pallas_call(
        _dispatch_kernel,
        grid_spec=pltpu.PrefetchScalarGridSpec(
            num_scalar_prefetch=1,
            grid=(t // tm,),
            in_specs=[pl.BlockSpec((tm, d), lambda i, pos: (i, 0))],
            out_specs=pl.BlockSpec(memory_space=pl.ANY),
            scratch_shapes=[pltpu.SemaphoreType.DMA],
        ),
        out_shape=jax.ShapeDtypeStruct((n_rows, d), F32),
        compiler_params=_cparams(1),
        name="moe_dispatch",
    )(pos_flat, x1)


def _expert_kernel(te_ref, na_ref, xs_ref, wg_ref, wu_ref, wd_ref, ys_ref):
    i = pl.program_id(0)

    @pl.when(i < na_ref[0])
    def _():
        xb = xs_ref[...].astype(BF16)
        g = jnp.dot(xb, wg_ref[0, 0].astype(BF16), preferred_element_type=F32)
        u = jnp.dot(xb, wu_ref[0, 0].astype(BF16), preferred_element_type=F32)
        h = (_silu(g) * u).astype(BF16)
        ys_ref[...] = jnp.dot(h, wd_ref[0, 0].astype(BF16), preferred_element_type=F32)


def _experts(layer, tile_expert, n_active, xs, w_gate, w_up, w_down):
    n_rows, d = xs.shape
    n_tiles = n_rows // MOE_TILE

    def rows(i, te, na):
        return (jnp.minimum(i, na[0] - 1), 0)

    def expert(i, te, na):
        return (layer, te[i], 0, 0)

    return pl.pallas_call(
        _expert_kernel,
        grid_spec=pltpu.PrefetchScalarGridSpec(
            num_scalar_prefetch=2,
            grid=(n_tiles,),
            in_specs=[pl.BlockSpec((MOE_TILE, d), rows),
                      pl.BlockSpec((1, 1, d, D_EXPERT), expert),
                      pl.BlockSpec((1, 1, d, D_EXPERT), expert),
                      pl.BlockSpec((1, 1, D_EXPERT, d), expert)],
            out_specs=pl.BlockSpec((MOE_TILE, d), rows),
        ),
        out_shape=jax.ShapeDtypeStruct((n_rows, d), F32),
        compiler_params=_cparams(1),
        name="moe_experts",
    )(tile_expert, n_active, xs, w_gate, w_up, w_down)


def _final_kernel(pos_ref, x1_ref, x1b_ref, gates_ref, p_ref, wg_ref, bg_ref, wp_ref, g_ref, b_ref, ys_ref,
                  x2_ref, x2b_ref, ybuf, sem):
    i = pl.program_id(0)
    tm = x1_ref.shape[0]
    t = pl.num_programs(0) * tm

    def copy(m, k):
        src = pos_ref[k * t + i * tm + m]
        return pltpu.make_async_copy(ys_ref.at[pl.ds(src, 1)], ybuf.at[k, pl.ds(m, 1)], sem)

    def start(m, c):
        copy(m, 0).start(priority=0)
        copy(m, 1).start(priority=1)
        return c

    lax.fori_loop(0, tm, start, 0, unroll=8)
    gate = _sigmoid(jnp.dot(x1b_ref[...], wg_ref[0], preferred_element_type=F32) + bg_ref[0])
    ple = gate * jnp.dot(p_ref[0].astype(BF16), wp_ref[0], preferred_element_type=F32)
    for k in range(TOP_K):
        pltpu.make_async_copy(ys_ref.at[pl.ds(0, tm)], ybuf.at[k], sem).wait()
    gates = gates_ref[...]
    ffn = gates[:, 0:1] * ybuf[0] + gates[:, 1:2] * ybuf[1]
    x2 = _layer_norm(DN_ALPHA * x1_ref[...] + ffn + ple, g_ref[0], b_ref[0])
    x2_ref[...] = x2
    x2b_ref[...] = x2.astype(BF16)


def _final(layer, pos_flat, x1, x1b, gates, p, wg_bf, bg, wp_bf, ln_g, ln_b, ys):
    t, d = x1.shape
    tm = TOK_TILE

    def row(w):
        return pl.BlockSpec((tm, w), lambda i, pos: (i, 0))

    def per_layer(shape):
        return pl.BlockSpec((1,) + shape, lambda i, pos: (layer,) + (0,) * len(shape))

    return pl.pallas_call(
        _final_kernel,
        grid_spec=pltpu.PrefetchScalarGridSpec(
            num_scalar_prefetch=1,
            grid=(t // tm,),
            in_specs=[row(d), row(d), row(LANES), pl.BlockSpec((1, tm, PLE_DIM), lambda i, pos: (layer, i, 0)),
                      per_layer((d, d)), per_layer((1, d)), per_layer((PLE_DIM, d)),
                      per_layer((1, d)), per_layer((1, d)), pl.BlockSpec(memory_space=pl.ANY)],
            out_specs=[row(d), row(d)],
            scratch_shapes=[pltpu.VMEM((TOP_K, tm, d), F32), pltpu.SemaphoreType.DMA],
        ),
        out_shape=[jax.ShapeDtypeStruct((t, d), F32), jax.ShapeDtypeStruct((t, d), BF16)],
        compiler_params=_cparams(1),
        name="combine_ple_ln",
    )(pos_flat, x1, x1b, gates, p, wg_bf, bg, wp_bf, ln_g, ln_b, ys)


def _arrange_w_in(w):
    cut = COL_GL + GLA_RANK
    assert sum(IN_SIZES[:5]) == cut and sum(IN_SIZES) + GL_PAD - GLA_RANK == U_COLS
    wb = w.astype(BF16)
    pad = jnp.zeros(w.shape[:-1] + (GL_PAD - GLA_RANK,), BF16)
    return jnp.concatenate([wb[..., :cut], pad, wb[..., cut:]], axis=-1)


def kernel(x, p, w_in, conf_dw_w, conf_dw_b, conf_ln_g, conf_ln_b, gla_w_g2, gla_b_g2, gla_norm_g, sc_conv_w, mix_scale, w_out, rel_bias, ln1_g, ln1_b, router_g_w, router_g_b, router_e_w, router_e_b, exp_w_gate, exp_w_up, exp_w_down, ple_w_gate, ple_b_gate, ple_w_proj, ln2_g, ln2_b):
    bsz, seq, d = x.shape
    t = bsz * seq
    n_rows = TOP_K * t + N_EXPERTS * MOE_TILE
    n_tiles = n_rows // MOE_TILE
    bias_tab = _bias_tables(rel_bias)
    w_in_bf = _arrange_w_in(w_in)
    w_out_bf = w_out.astype(BF16)
    ple_wg_bf = ple_w_gate.astype(BF16)
    ple_wp_bf = ple_w_proj.astype(BF16)
    p3 = p.reshape(DEPTH, t, PLE_DIM)

    def vec(a):
        return a.reshape(DEPTH, 1, -1)

    xf = x.reshape(t, d)
    xb = xf
    for i in range(DEPTH):
        u = _in_proj(xb, w_in_bf, i, 512, 2048, F32)
        u3 = u.reshape(bsz, seq, U_COLS)
        ya, yd = _conv_mixers(u3, conf_dw_w[i], conf_dw_b[i], conf_ln_g[i], conf_ln_b[i], sc_conv_w[i])
        yb = _gla(u3, gla_w_g2[i], gla_b_g2[i], gla_norm_g[i])
        yc = _dilated(u3, bias_tab)
        rw = jnp.zeros((d, LANES), F32).at[:, :N_GROUPS].set(router_g_w[i]).at[:, N_GROUPS:N_GROUPS + N_EXPERTS].set(router_e_w[i])
        rb = jnp.zeros((1, LANES), F32).at[0, :N_GROUPS].set(router_g_b[i]).at[0, N_GROUPS:N_GROUPS + N_EXPERTS].set(router_e_b[i])
        rw_hi = rw.astype(BF16)
        rw_lo = (rw - rw_hi.astype(F32)).astype(BF16)
        x1, x1b, ids, gates = _outproj(i, xf, ya.reshape(t, -1), yb.reshape(t, -1), yc.reshape(t, -1), yd.reshape(t, -1),
                                       vec(mix_scale), w_out_bf, vec(ln1_g), vec(ln1_b), rw_hi, rw_lo, rb)
        pos, tile_tab = _plan(ids)
        pos_flat = pos[:, :TOP_K].T.reshape(-1)
        tile_end = tile_tab[:n_tiles, 0]
        n_active = jnp.sum((tile_end < N_EXPERTS).astype(I32)).reshape(1)
        tile_expert = jnp.minimum(tile_end, N_EXPERTS - 1)
        xs = _dispatch(pos_flat, x1, n_rows)
        ys = _experts(i, tile_expert, n_active, xs, exp_w_gate, exp_w_up, exp_w_down)
        xf, xb = _final(i, pos_flat, x1, x1b, gates, p3, ple_wg_bf, vec(ple_b_gate), ple_wp_bf, vec(ln2_g), vec(ln2_b), ys)
    return xf.reshape(bsz, seq, d)
```

```python
import functools
import math

import jax
import jax.numpy as jnp
from jax import lax
from jax.experimental import pallas as pl
from jax.experimental.pallas import tpu as pltpu

F32 = jnp.float32
BF16 = jnp.bfloat16
I32 = jnp.int32
HIGHEST = lax.Precision.HIGHEST

D_MODEL = 2048
DEPTH = 2
PLE_DIM = 256
W_CONF = 512
W_GLA = 512
W_DIL = 512
W_SC = 512
CONF_KERNEL = 31
GLA_HEADS = 4
GLA_DV = 128
GLA_DK = 64
GLA_RANK = 16
GLA_TAU = 16.0
DIL_HEADS = 8
DIL_HD = 64
DIL_BRANCHES = ((128, 1), (512, 4), (2048, 16))
DIL_BLOCK = 128
REL_BUCKETS = 32
REL_MAX_DIST = 2048
SC_KERNEL = 3
N_GROUPS = 4
EXPERTS_PER_GROUP = 8
N_EXPERTS = 32
TOP_K = 2
D_EXPERT = 256
DN_ALPHA = (2 * DEPTH) ** 0.25
LN_EPS = 1e-5
RMS_EPS = 1e-6
IN_SIZES = (2 * W_CONF, GLA_HEADS * GLA_DK, GLA_HEADS * GLA_DK, W_GLA, GLA_RANK, W_GLA,
            W_DIL, W_DIL, W_DIL, W_SC, W_SC, W_SC)

LANES = 128
SUBLANES = 8
VMEM_LIMIT = 56 * 1024 * 1024

GL_PAD = 512
COL_A = 0
COL_GQ = 1024
COL_GK = 1280
COL_GV = 1536
COL_GL = 2048
COL_GR = 2560
COL_CQ = 3072
COL_CK = 3584
COL_CV = 4096
COL_SB = 4608
COL_SC = 5120
COL_SH = 5632
U_COLS = 6144

MOE_TILE = 256


def _cparams(n_axes):
    return pltpu.CompilerParams(dimension_semantics=("arbitrary",) * n_axes, vmem_limit_bytes=VMEM_LIMIT)


def _layer_norm(x, g, b):
    mu = jnp.mean(x, axis=-1, keepdims=True)
    xc = x - mu
    var = jnp.mean(xc * xc, axis=-1, keepdims=True)
    return xc * lax.rsqrt(var + LN_EPS) * g + b


U32 = jnp.uint32


def _pack_bf16_pairs(x):
    n = x.shape[1] // 2
    bits = lax.bitcast_convert_type(x.astype(BF16).astype(F32), U32)
    return (bits[:, :n] >> 16) | (bits[:, n:] & jnp.uint32(0xFFFF0000))


def _unpack_bf16_pairs(w):
    lo = lax.bitcast_convert_type(w << 16, F32)
    hi = lax.bitcast_convert_type(w & jnp.uint32(0xFFFF0000), F32)
    return jnp.concatenate([lo, hi], axis=1)


def _sigmoid(x):
    return 1.0 / (1.0 + jnp.exp(-x))


def _silu(x):
    return x * _sigmoid(x)


def _matmul_kernel(x_ref, w_ref, o_ref):
    o_ref[...] = jnp.dot(x_ref[...].astype(BF16), w_ref[0],
                         preferred_element_type=F32).astype(o_ref.dtype)


def _in_proj(x, w_all, layer, tm, tn, out_dtype):
    m, k = x.shape
    n = w_all.shape[2]
    return pl.pallas_call(
        _matmul_kernel,
        grid=(n // tn, m // tm),
        in_specs=[pl.BlockSpec((tm, k), lambda j, i: (i, 0)),
                  pl.BlockSpec((1, k, tn), lambda j, i: (layer, 0, j))],
        out_specs=pl.BlockSpec((tm, tn), lambda j, i: (i, j)),
        out_shape=jax.ShapeDtypeStruct((m, n), out_dtype),
        compiler_params=_cparams(2),
        name="in_proj",
    )(x, w_all)


CONV_TS = 512
CONV_RC = 64
CONF_HIST = 32
SC_HIST = 8


def _conv_kernel(a_ref, gate_ref, sb_ref, sc_ref, sh_ref, cw_ref, cb_ref, lg_ref, lb_ref, sw_ref,
                 ya_ref, yd_ref, ha, hd, hs):
    s = pl.program_id(1)
    ts = a_ref.shape[1]
    n_ext = CONF_HIST + ts

    @pl.when(s == 0)
    def _():
        ha[0:CONF_HIST, :] = jnp.zeros((CONF_HIST, W_CONF), F32)
        hd[0:SC_HIST, :] = jnp.zeros((SC_HIST, W_SC), F32)

    ha[CONF_HIST:CONF_HIST + ts, :] = a_ref[0] * _sigmoid(gate_ref[0])
    hd[SC_HIST:SC_HIST + ts, :] = sc_ref[0] * sh_ref[0]

    cb = cb_ref[...]
    lg = lg_ref[...]
    lb = lb_ref[...]
    off_a = CONF_HIST - (CONF_KERNEL - 1)
    h_all = ha[...]
    for sh in range(1, SUBLANES):
        hs[sh - 1] = pltpu.roll(h_all, n_ext - sh, 0)
    for c in range(ts // CONV_RC):
        r0 = c * CONV_RC
        acc = jnp.zeros((CONV_RC, W_CONF), F32)
        for j in range(CONF_KERNEL):
            sh = (off_a + j) % SUBLANES
            lo = r0 + off_a + j - sh
            rows = ha[lo:lo + CONV_RC, :] if sh == 0 else hs[sh - 1, lo:lo + CONV_RC, :]
            acc = acc + cw_ref[j:j + 1, :] * rows
        hh = _layer_norm(acc + cb, lg, lb)
        ya_ref[0, r0:r0 + CONV_RC, :] = _silu(hh)

    off_d = SC_HIST - (SC_KERNEL - 1)
    accd = jnp.zeros((ts, W_SC), F32)
    for j in range(SC_KERNEL):
        accd = accd + sw_ref[j:j + 1, :] * hd[off_d + j:off_d + j + ts, :]
    yd_ref[0] = sb_ref[0] * accd

    ha[0:CONF_HIST, :] = ha[ts:ts + CONF_HIST, :]
    hd[0:SC_HIST, :] = hd[ts:ts + SC_HIST, :]


def _conv_mixers(u3, conf_w, conf_b, conf_lg, conf_lb, sc_w):
    bsz, seq, _ = u3.shape
    ts = CONV_TS
    wblk = W_CONF

    def col(c):
        return pl.BlockSpec((1, ts, wblk), lambda b, s, c=c: (b, s, c // wblk))

    def full(shape):
        return pl.BlockSpec(shape, lambda b, s: (0,) * len(shape))

    cw = jnp.zeros((32, W_CONF), F32).at[:CONF_KERNEL].set(conf_w)
    sw = jnp.zeros((8, W_SC), F32).at[:SC_KERNEL].set(sc_w)
    out_spec = pl.BlockSpec((1, ts, wblk), lambda b, s: (b, s, 0))
    return pl.pallas_call(
        _conv_kernel,
        grid=(bsz, seq // ts),
        in_specs=[col(COL_A), col(COL_A + W_CONF), col(COL_SB), col(COL_SC), col(COL_SH),
                  full((32, W_CONF)), full((1, W_CONF)), full((1, W_CONF)), full((1, W_CONF)), full((8, W_SC))],
        out_specs=[out_spec, out_spec],
        out_shape=[jax.ShapeDtypeStruct((bsz, seq, W_CONF), F32), jax.ShapeDtypeStruct((bsz, seq, W_SC), F32)],
        scratch_shapes=[pltpu.VMEM((CONF_HIST + ts, W_CONF), F32), pltpu.VMEM((SC_HIST + ts, W_SC), F32),
                        pltpu.VMEM((SUBLANES - 1, CONF_HIST + ts, W_CONF), F32)],
        compiler_params=_cparams(2),
        name="conv_mixers",
    )(u3, u3, u3, u3, u3, cw, conf_b.reshape(1, -1), conf_lg.reshape(1, -1), conf_lb.reshape(1, -1), sw)


GLA_BLK = 128
GLA_SUB = 64
GLA_UNROLL = 4


def _split3(x):
    hi = x.astype(BF16)
    r1 = x - hi.astype(F32)
    mid = r1.astype(BF16)
    lo = (r1 - mid.astype(F32)).astype(BF16)
    return hi, mid, lo


def _gla_kernel(q_ref, k_ref, v_ref, gl_ref, r_ref, wg2h_ref, wg2l_ref, bg2_ref, ng_ref, tril_ref, o_ref,
                st_ref, la_ref):
    seq = q_ref.shape[1]
    nblk = seq // GLA_BLK
    nsub = GLA_BLK // GLA_SUB
    assert nsub == 2
    st_ref[...] = jnp.zeros(st_ref.shape, F32)

    lane = lax.broadcasted_iota(I32, (1, LANES), 1)
    head_mask = [(lane < GLA_DK).astype(F32), (lane >= GLA_DK).astype(F32)]
    row = lax.broadcasted_iota(I32, (GLA_BLK, GLA_BLK), 0)
    col = lax.broadcasted_iota(I32, (GLA_BLK, GLA_BLK), 1)
    sub_shift = GLA_SUB.bit_length() - 1
    diag_mask = ((row >> sub_shift) == (col >> sub_shift)) & (col <= row)
    second_half = lax.broadcasted_iota(I32, (GLA_BLK, 1), 0) >= GLA_SUB
    scale = GLA_DK ** -0.5
    ng = ng_ref[...]
    tril = tril_ref[...]

    gl = gl_ref[0]
    gl_hi = gl.astype(BF16)
    gl_lo = (gl - gl_hi.astype(F32)).astype(BF16)
    wh = wg2h_ref[...]
    z = (jnp.dot(gl_hi, wh, preferred_element_type=F32) + jnp.dot(gl_lo, wh, preferred_element_type=F32)
         + jnp.dot(gl_hi, wg2l_ref[...], preferred_element_type=F32) + bg2_ref[...])
    la_ref[...] = (jnp.minimum(z, 0.0) - jnp.log(1.0 + jnp.exp(-jnp.abs(z)))) * (1.0 / GLA_TAU)

    def body(s, carry):
        r0 = pl.multiple_of(s * GLA_BLK, GLA_BLK)
        q = q_ref[0, pl.ds(r0, GLA_BLK), :] * scale
        k = k_ref[0, pl.ds(r0, GLA_BLK), :]
        v = v_ref[0, pl.ds(r0, GLA_BLK), :].astype(BF16)
        b = None
        for part in _split3(la_ref[pl.ds(r0, GLA_BLK), :]):
            t = jnp.dot(tril, part, preferred_element_type=F32)
            b = t if b is None else b + t

        b3 = b.reshape(nsub, GLA_SUB, LANES)
        mid = jnp.broadcast_to(b3[:, GLA_SUB // 2 - 1:GLA_SUB // 2, :], b3.shape).reshape(GLA_BLK, LANES)
        qd = q * jnp.exp(b - mid)
        kd = (k * jnp.exp(mid - b)).astype(BF16)
        e0 = b[GLA_SUB - 1:GLA_SUB, :]
        q_off = jnp.where(second_half, q * jnp.exp(jnp.minimum(b - e0, 0.0)), 0.0)
        k_off = jnp.where(second_half, 0.0, k * jnp.exp(jnp.minimum(e0 - b, 0.0))).astype(BF16)
        q_st = q * jnp.exp(b)
        b_last = b[GLA_BLK - 1:GLA_BLK, :]
        kk = (k * jnp.exp(b_last - b)).astype(BF16)
        st = st_ref[...]
        st_bf = st.astype(BF16)

        outs = []
        for h in range(2):
            hm = head_mask[h]
            s_d = lax.dot_general((qd * hm).astype(BF16), kd, (((1,), (1,)), ((), ())), preferred_element_type=F32)
            s_o = lax.dot_general((q_off * hm).astype(BF16), k_off, (((1,), (1,)), ((), ())),
                                  preferred_element_type=F32)
            scores = jnp.where(diag_mask, s_d, 0.0) + s_o
            lhs = jnp.concatenate([scores.astype(BF16), (q_st * hm).astype(BF16)], axis=1)
            rhs = jnp.concatenate([v[:, h * GLA_DV:(h + 1) * GLA_DV], st_bf[:, h * GLA_DV:(h + 1) * GLA_DV]], axis=0)
            o_h = jnp.dot(lhs, rhs, preferred_element_type=F32)
            o_h = o_h * lax.rsqrt(jnp.mean(o_h * o_h, axis=-1, keepdims=True) + RMS_EPS)
            outs.append(o_h)
        o = jnp.concatenate(outs, axis=1) * ng
        o_ref[0, pl.ds(r0, GLA_BLK), :] = o * _silu(r_ref[0, pl.ds(r0, GLA_BLK), :])

        upd = lax.dot_general(kk, v, (((0,), (0,)), ((), ())), preferred_element_type=F32)
        decay = jnp.transpose(jnp.broadcast_to(jnp.exp(b_last), (GLA_BLK, LANES)))
        st_ref[...] = st * jnp.concatenate([decay, decay], axis=1) + upd
        return carry

    lax.fori_loop(0, nblk, body, 0, unroll=GLA_UNROLL)


def _gla(u3, w_g2, b_g2, norm_g):
    bsz, seq, _ = u3.shape
    npair = GLA_HEADS // 2
    kw = 2 * GLA_DK
    vw = 2 * GLA_DV

    def col(c, w):
        return pl.BlockSpec((1, seq, w), lambda b, p, c=c, w=w: (b, 0, c // w + p))

    wg2 = jnp.zeros((LANES, GLA_HEADS * GLA_DK), F32).at[:GLA_RANK].set(w_g2)
    wg2_hi = wg2.astype(BF16)
    wg2_lo = (wg2 - wg2_hi.astype(F32)).astype(BF16)
    ng = jnp.concatenate([norm_g, norm_g]).reshape(1, vw)
    tril = jnp.tril(jnp.ones((GLA_BLK, GLA_BLK), BF16))
    return pl.pallas_call(
        _gla_kernel,
        grid=(bsz, npair),
        in_specs=[col(COL_GQ, kw), col(COL_GK, kw), col(COL_GV, vw),
                  pl.BlockSpec((1, seq, LANES), lambda b, p: (b, 0, COL_GL // LANES)),
                  col(COL_GR, vw),
                  pl.BlockSpec((LANES, kw), lambda b, p: (0, p)),
                  pl.BlockSpec((LANES, kw), lambda b, p: (0, p)),
                  pl.BlockSpec((1, kw), lambda b, p: (0, p)),
                  pl.BlockSpec((1, vw), lambda b, p: (0, 0)),
                  pl.BlockSpec((GLA_BLK, GLA_BLK), lambda b, p: (0, 0))],
        out_specs=pl.BlockSpec((1, seq, vw), lambda b, p: (b, 0, p)),
        out_shape=jax.ShapeDtypeStruct((bsz, seq, W_GLA), F32),
        scratch_shapes=[pltpu.VMEM((kw, vw), F32), pltpu.VMEM((seq, LANES), F32)],
        compiler_params=_cparams(2),
        name="gla",
    )(u3, u3, u3, u3, u3, wg2_hi, wg2_lo, b_g2.reshape(1, -1), ng, tril)


def _t5_bucket(dist):
    max_exact = REL_BUCKETS // 2
    large = max_exact + (jnp.log(jnp.maximum(dist, 1).astype(F32) / max_exact)
                         / math.log(REL_MAX_DIST / max_exact) * (REL_BUCKETS - max_exact)).astype(I32)
    large = jnp.minimum(large, REL_BUCKETS - 1)
    return jnp.where(dist < max_exact, dist, large)


LOG2E = math.log2(math.e)
DIL_GROUP = 16


def _bias_kernel(rel_ref, bucket_ref, o_ref):
    h = pl.program_id(1)
    blk = DIL_BLOCK
    bucket = bucket_ref[0]
    acc = jnp.zeros(bucket.shape, F32)
    for b in range(REL_BUCKETS):
        acc = jnp.where(bucket == b, rel_ref[b, h], acc)
    acc = acc * LOG2E
    qi = lax.broadcasted_iota(I32, (blk, 2 * blk), 0)
    kj = lax.broadcasted_iota(I32, (blk, 2 * blk), 1)
    steps = qi + blk - kj
    in_window = (steps >= 0) & (steps <= blk)
    neg = jnp.float32(-jnp.inf)
    o_ref[0, 0, 0] = jnp.where(in_window, acc, neg)
    o_ref[0, 0, 1] = jnp.where(in_window & (kj >= blk), acc, neg)


def _bias_tables(rel_bias):
    assert all(window // dil == DIL_BLOCK for window, dil in DIL_BRANCHES)
    qi = jnp.arange(DIL_BLOCK)[:, None]
    kj = jnp.arange(2 * DIL_BLOCK)[None, :]
    steps = jnp.maximum(qi + DIL_BLOCK - kj, 0)
    buckets = jnp.stack([_t5_bucket(steps * dil) for _, dil in DIL_BRANCHES]).astype(I32)
    nbr = len(DIL_BRANCHES)
    return pl.pallas_call(
        _bias_kernel,
        grid=(nbr, DIL_HEADS),
        in_specs=[pl.BlockSpec(memory_space=pltpu.SMEM),
                  pl.BlockSpec((1, DIL_BLOCK, 2 * DIL_BLOCK), lambda i, h: (i, 0, 0))],
        out_specs=pl.BlockSpec((1, 1, 2, DIL_BLOCK, 2 * DIL_BLOCK), lambda i, h: (i, h, 0, 0, 0)),
        out_shape=jax.ShapeDtypeStruct((nbr, DIL_HEADS, 2, DIL_BLOCK, 2 * DIL_BLOCK), F32),
        compiler_params=_cparams(2),
        name="rel_bias_tables",
    )(rel_bias, buckets)


def _dil_kernel(q_ref, k_ref, v_ref, bias_ref, o_ref, qs, ks, vs, ob, lb):
    seq = q_ref.shape[1]
    blk = DIL_BLOCK
    scale = DIL_HD ** -0.5 * LOG2E
    lane = lax.broadcasted_iota(I32, (1, LANES), 1)
    first = lane < DIL_HD
    head_mask = [first.astype(F32), 1.0 - first.astype(F32)]

    ks[0:blk, :] = jnp.zeros((blk, LANES), BF16)
    vs[0:blk, :] = jnp.zeros((blk, LANES), BF16)

    for bi, (window, dil) in enumerate(DIL_BRANCHES):
        sub_len = seq // dil
        nb = sub_len // blk
        nb_shift = nb.bit_length() - 1
        for r in range(dil):
            rows = pl.ds(r, sub_len, stride=dil) if dil > 1 else pl.ds(0, sub_len)
            q = q_ref[0, rows, :] * scale
            for h in range(2):
                qs[h, r * sub_len:(r + 1) * sub_len, :] = (q * head_mask[h]).astype(BF16)
            ks[blk + r * sub_len:blk + (r + 1) * sub_len, :] = k_ref[0, rows, :].astype(BF16)
            vs[blk + r * sub_len:blk + (r + 1) * sub_len, :] = v_ref[0, rows, :].astype(BF16)

        def body(it, carry, bi=bi, dil=dil, nb=nb, nb_shift=nb_shift):
            for gg in range(DIL_GROUP):
                g = it * DIL_GROUP + gg
                n = g & (nb - 1)
                r = g >> nb_shift
                g0 = pl.multiple_of(g * blk, blk)
                kb = ks[pl.ds(g0, 2 * blk), :]
                vb = vs[pl.ds(g0, 2 * blk), :]
                variant = jnp.where(n == 0, 1, 0)
                o_acc = None
                l_acc = None
                for h in range(2):
                    sc = lax.dot_general(qs[h, pl.ds(g0, blk), :], kb, (((1,), (1,)), ((), ())),
                                         preferred_element_type=F32)
                    sc = sc + bias_ref[bi, h, variant]
                    m = jnp.max(sc, axis=-1, keepdims=True)
                    p = jnp.exp2(sc - m)
                    l = jnp.sum(p, axis=-1, keepdims=True)
                    o_h = jnp.dot(p.astype(BF16), vb, preferred_element_type=F32) * (1.0 / l)
                    lse = jnp.broadcast_to(m + jnp.log2(l), (blk, LANES))
                    if h == 0:
                        o_acc, l_acc = o_h, lse
                    else:
                        o_acc = jnp.where(first, o_acc, o_h)
                        l_acc = jnp.where(first, l_acc, lse)
                if dil > 1:
                    dst = pl.ds(n * (blk * dil) + r, blk, stride=dil)
                else:
                    dst = pl.ds(g0, blk)
                ob[bi, dst, :] = o_acc
                lb[bi, dst, :] = l_acc
            return carry

        lax.fori_loop(0, seq // blk // DIL_GROUP, body, 0)

    nbr = len(DIL_BRANCHES)
    mc = 256
    for c in range(seq // mc):
        rs = slice(c * mc, (c + 1) * mc)
        ls = [lb[i, rs, :] for i in range(nbr)]
        m = functools.reduce(jnp.maximum, ls)
        ws = [jnp.exp2(l - m) for l in ls]
        num = functools.reduce(lambda a, b: a + b, [w * ob[i, rs, :] for i, w in enumerate(ws)])
        den = functools.reduce(lambda a, b: a + b, ws)
        o_ref[0, rs, :] = num / den


def _dilated(u3, bias_tab):
    bsz, seq, _ = u3.shape
    npair = DIL_HEADS // 2
    nbr = len(DIL_BRANCHES)

    def col(c):
        return pl.BlockSpec((1, seq, LANES), lambda b, p, c=c: (b, 0, c // LANES + p))

    return pl.pallas_call(
        _dil_kernel,
        grid=(bsz, npair),
        in_specs=[col(COL_CQ), col(COL_CK), col(COL_CV),
                  pl.BlockSpec((nbr, 2, 2, DIL_BLOCK, 2 * DIL_BLOCK), lambda b, p: (0, p, 0, 0, 0))],
        out_specs=pl.BlockSpec((1, seq, LANES), lambda b, p: (b, 0, p)),
        out_shape=jax.ShapeDtypeStruct((bsz, seq, W_DIL), F32),
        scratch_shapes=[pltpu.VMEM((2, seq, LANES), BF16),
                        pltpu.VMEM((DIL_BLOCK + seq, LANES), BF16),
                        pltpu.VMEM((DIL_BLOCK + seq, LANES), BF16),
                        pltpu.VMEM((nbr, seq, LANES), F32),
                        pltpu.VMEM((nbr, seq, LANES), F32)],
        compiler_params=_cparams(2),
        name="dilated_attn",
    )(u3, u3, u3, bias_tab)


TOK_TILE = 256
OUT_TILE = 512
PLAN_TILE = 1024


def _outproj_kernel(x_ref, ya_ref, yb_ref, yc_ref, yd_ref, ms_ref, wo_ref, g_ref, b_ref, rwh_ref, rwl_ref, rb_ref,
                    x1_ref, x1b_ref, x1p_ref, ids_ref, gates_ref):
    acc = None
    for gi, y_ref in enumerate((ya_ref, yb_ref, yc_ref, yd_ref)):
        lo = gi * W_CONF
        y = (y_ref[...] * ms_ref[0, :, lo:lo + W_CONF]).astype(BF16)
        part = jnp.dot(y, wo_ref[0, lo:lo + W_CONF, :], preferred_element_type=F32)
        acc = part if acc is None else acc + part
    x1 = _layer_norm(DN_ALPHA * x_ref[...] + acc, g_ref[0], b_ref[0])
    x1_ref[...] = x1
    x1_hi = x1.astype(BF16)
    x1b_ref[...] = x1_hi
    x1p_ref[...] = _pack_bf16_pairs(x1)

    x1_lo = (x1 - x1_hi.astype(F32)).astype(BF16)
    rwh = rwh_ref[...]
    logits = (jnp.dot(x1_hi, rwh, preferred_element_type=F32) + jnp.dot(x1_lo, rwh, preferred_element_type=F32)
              + jnp.dot(x1_hi, rwl_ref[...], preferred_element_type=F32) + rb_ref[...])
    tm = logits.shape[0]
    lane = lax.broadcasted_iota(I32, (tm, LANES), 1)
    neg = jnp.float32(-jnp.inf)
    gl = jnp.where(lane < N_GROUPS, logits, neg)
    gmax = jnp.max(gl, axis=-1, keepdims=True)
    g_top = 1.0 / jnp.sum(jnp.exp(gl - gmax), axis=-1, keepdims=True)
    g_idx = jnp.min(jnp.where(gl == gmax, lane, LANES), axis=-1, keepdims=True)
    e_lo = N_GROUPS + g_idx * EXPERTS_PER_GROUP
    el = jnp.where((lane >= e_lo) & (lane < e_lo + EXPERTS_PER_GROUP), logits, neg)
    m1 = jnp.max(el, axis=-1, keepdims=True)
    i1 = jnp.min(jnp.where(el == m1, lane, LANES), axis=-1, keepdims=True)
    el2 = jnp.where(lane == i1, neg, el)
    m2 = jnp.max(el2, axis=-1, keepdims=True)
    i2 = jnp.min(jnp.where(el2 == m2, lane, LANES), axis=-1, keepdims=True)
    e21 = jnp.exp(m2 - m1)
    w1 = g_top / (1.0 + e21)
    w2 = g_top * e21 / (1.0 + e21)
    ids_ref[...] = jnp.where(lane == 0, i1 - N_GROUPS, jnp.where(lane == 1, i2 - N_GROUPS, N_EXPERTS))
    gates_ref[...] = jnp.where(lane == 0, w1, jnp.where(lane == 1, w2, 0.0))


def _outproj(layer, x, ya, yb, yc, yd, mix_scale, w_out_bf, ln_g, ln_b, rw_hi, rw_lo, rb):
    t, d = x.shape
    tm = OUT_TILE

    def row(w):
        return pl.BlockSpec((tm, w), lambda i: (i, 0))

    def full(shape):
        return pl.BlockSpec(shape, lambda i: (0,) * len(shape))

    def per_layer(shape):
        return pl.BlockSpec((1,) + shape, lambda i: (layer,) + (0,) * len(shape))

    return pl.pallas_call(
        _outproj_kernel,
        grid=(t // tm,),
        in_specs=[row(d), row(W_CONF), row(W_GLA), row(W_DIL), row(W_SC), per_layer((1, d)), per_layer((d, d)),
                  per_layer((1, d)), per_layer((1, d)), full((d, LANES)), full((d, LANES)), full((1, LANES))],
        out_specs=[row(d), row(d), row(d // 2), row(LANES), row(LANES)],
        out_shape=[jax.ShapeDtypeStruct((t, d), F32), jax.ShapeDtypeStruct((t, d), BF16),
                   jax.ShapeDtypeStruct((t, d // 2), U32),
                   jax.ShapeDtypeStruct((t, LANES), I32), jax.ShapeDtypeStruct((t, LANES), F32)],
        compiler_params=_cparams(1),
        name="out_proj_ln_router",
    )(x, ya, yb, yc, yd, mix_scale, w_out_bf, ln_g, ln_b, rw_hi, rw_lo, rb)


def _plan_kernel(ids_ref, tril_ref, upper_ref, pos_ref, tile_ref, cnt, carry, start):
    phase = pl.program_id(0)
    i = pl.program_id(1)
    tm = ids_ref.shape[0]
    lane = lax.broadcasted_iota(I32, (tm, LANES), 1)
    ids = ids_ref[...]
    e1 = ids[:, 0:1]
    e2 = ids[:, 1:2]
    oh1 = (lane == e1).astype(F32)
    oh2 = (lane == e2).astype(F32)
    oh = oh1 + oh2

    @pl.when((phase == 0) & (i == 0))
    def _():
        cnt[...] = jnp.zeros(cnt.shape, F32)

    @pl.when(phase == 0)
    def _():
        cnt[...] = cnt[...] + jnp.sum(oh, axis=0, keepdims=True)

    @pl.when((phase == 1) & (i == 0))
    def _():
        padded = jnp.floor((cnt[...] + (MOE_TILE - 1)) * (1.0 / MOE_TILE)) * MOE_TILE
        p8 = jnp.broadcast_to(padded, (SUBLANES, LANES))
        st = jnp.dot(p8, upper_ref[...], precision=HIGHEST, preferred_element_type=F32)
        start[...] = st[0:1, :]
        carry[...] = jnp.zeros(carry.shape, F32)
        end = st[0:1, :] + padded
        trow = lax.broadcasted_iota(I32, (LANES, LANES), 0).astype(F32) * MOE_TILE
        tlane = lax.broadcasted_iota(I32, (LANES, LANES), 1)
        done = jnp.where((end <= trow) & (tlane < N_EXPERTS), 1.0, 0.0)
        tile_ref[...] = jnp.broadcast_to(jnp.sum(done, axis=-1, keepdims=True), (LANES, LANES)).astype(I32)

    @pl.when(phase == 1)
    def _():
        incl = jnp.dot(tril_ref[...], oh.astype(BF16), preferred_element_type=F32) + carry[...]
        base = start[...] + incl - oh
        p1 = jnp.sum(oh1 * base, axis=-1, keepdims=True)
        p2 = jnp.sum(oh2 * base, axis=-1, keepdims=True)
        pos_ref[...] = jnp.where(lane == 0, p1, jnp.where(lane == 1, p2, 0.0)).astype(I32)
        carry[...] = carry[...] + jnp.sum(oh, axis=0, keepdims=True)


def _plan(ids):
    t = ids.shape[0]
    tm = PLAN_TILE
    tril = jnp.tril(jnp.ones((tm, tm), BF16))
    upper = jnp.triu(jnp.ones((LANES, LANES), F32), k=1)
    return pl.pallas_call(
        _plan_kernel,
        grid=(2, t // tm),
        in_specs=[pl.BlockSpec((tm, LANES), lambda ph, i: (i, 0)),
                  pl.BlockSpec((tm, tm), lambda ph, i: (0, 0)),
                  pl.BlockSpec((LANES, LANES), lambda ph, i: (0, 0))],
        out_specs=[pl.BlockSpec((tm, LANES), lambda ph, i: (i * ph, 0)),
                   pl.BlockSpec((LANES, LANES), lambda ph, i: (0, 0))],
        out_shape=[jax.ShapeDtypeStruct((t, LANES), I32), jax.ShapeDtypeStruct((LANES, LANES), I32)],
        scratch_shapes=[pltpu.VMEM((1, LANES), F32), pltpu.VMEM((1, LANES), F32), pltpu.VMEM((1, LANES), F32)],
        compiler_params=_cparams(2),
        name="route_plan",
    )(ids, tril, upper)


def _dispatch_kernel(pos_ref, x_ref, xs_ref, sem):
    i = pl.program_id(0)
    tm = x_ref.shape[0]
    t = pl.num_programs(0) * tm

    def copy(m, k):
        dst = pos_ref[k * t + i * tm + m]
        return pltpu.make_async_copy(x_ref.at[pl.ds(m, 1)], xs_ref.at[pl.ds(dst, 1)], sem)

    def start(m, c):
        copy(m, 0).start(priority=0)
        copy(m, 1).start(priority=1)
        return c

    for m in range(tm):
        start(m, 0)
    for _ in range(TOP_K):
        pltpu.make_async_copy(x_ref, xs_ref.at[pl.ds(0, tm)], sem).wait()


def _dispatch(pos_flat, x1, n_rows):
    t, d = x1.shape
    tm = TOK_TILE
    return pl.pallas_call(
        _dispatch_kernel,
        grid_spec=pltpu.PrefetchScalarGridSpec(
            num_scalar_prefetch=1,
            grid=(t // tm,),
            in_specs=[pl.BlockSpec((tm, d), lambda i, pos: (i, 0))],
            out_specs=pl.BlockSpec(memory_space=pl.ANY),
            scratch_shapes=[pltpu.SemaphoreType.DMA],
        ),
        out_shape=jax.ShapeDtypeStruct((n_rows, d), x1.dtype),
        compiler_params=_cparams(1),
        name="moe_dispatch",
    )(pos_flat, x1)


def _expert_kernel(te_ref, na_ref, xs_ref, wg_ref, wu_ref, wd_ref, ys_ref):
    i = pl.program_id(0)

    @pl.when(i < na_ref[0])
    def _():
        xb = _unpack_bf16_pairs(xs_ref[...]).astype(BF16)
        g = jnp.dot(xb, wg_ref[0, 0].astype(BF16), preferred_element_type=F32)
        u = jnp.dot(xb, wu_ref[0, 0].astype(BF16), preferred_element_type=F32)
        h = (_silu(g) * u).astype(BF16)
        ys_ref[...] = _pack_bf16_pairs(jnp.dot(h, wd_ref[0, 0].astype(BF16), preferred_element_type=F32))


def _experts(layer, tile_expert, n_active, xs, w_gate, w_up, w_down):
    n_rows, d = xs.shape
    n_tiles = n_rows // MOE_TILE
    d_model = w_gate.shape[2]

    def rows(i, te, na):
        return (jnp.minimum(i, na[0] - 1), 0)

    def expert(i, te, na):
        return (layer, te[i], 0, 0)

    return pl.pallas_call(
        _expert_kernel,
        grid_spec=pltpu.PrefetchScalarGridSpec(
            num_scalar_prefetch=2,
            grid=(n_tiles,),
            in_specs=[pl.BlockSpec((MOE_TILE, d), rows),
                      pl.BlockSpec((1, 1, d_model, D_EXPERT), expert),
                      pl.BlockSpec((1, 1, d_model, D_EXPERT), expert),
                      pl.BlockSpec((1, 1, D_EXPERT, d_model), expert)],
            out_specs=pl.BlockSpec((MOE_TILE, d), rows),
        ),
        out_shape=jax.ShapeDtypeStruct((n_rows, d), U32),
        compiler_params=_cparams(1),
        name="moe_experts",
    )(tile_expert, n_active, xs, w_gate, w_up, w_down)


def _final_kernel(pos_ref, x1_ref, x1b_ref, gates_ref, p_ref, wg_ref, bg_ref, wp_ref, g_ref, b_ref, ys_ref,
                  x2_ref, x2b_ref, ybuf, sem):
    i = pl.program_id(0)
    tm = x1_ref.shape[0]
    t = pl.num_programs(0) * tm

    def copy(m, k):
        src = pos_ref[k * t + i * tm + m]
        return pltpu.make_async_copy(ys_ref.at[pl.ds(src, 1)], ybuf.at[k, pl.ds(m, 1)], sem)

    def start(m, c):
        copy(m, 0).start(priority=0)
        copy(m, 1).start(priority=1)
        return c

    for m in range(tm):
        start(m, 0)
    gate = _sigmoid(jnp.dot(x1b_ref[...], wg_ref[0], preferred_element_type=F32) + bg_ref[0])
    ple = gate * jnp.dot(p_ref[0].astype(BF16), wp_ref[0], preferred_element_type=F32)
    for k in range(TOP_K):
        pltpu.make_async_copy(ys_ref.at[pl.ds(0, tm)], ybuf.at[k], sem).wait()
    gates = gates_ref[...]
    ffn = gates[:, 0:1] * _unpack_bf16_pairs(ybuf[0]) + gates[:, 1:2] * _unpack_bf16_pairs(ybuf[1])
    x2 = _layer_norm(DN_ALPHA * x1_ref[...] + ffn + ple, g_ref[0], b_ref[0])
    x2_ref[...] = x2
    x2b_ref[...] = x2.astype(BF16)


def _final(layer, pos_flat, x1, x1b, gates, p, wg_bf, bg, wp_bf, ln_g, ln_b, ys):
    t, d = x1.shape
    tm = TOK_TILE

    def row(w):
        return pl.BlockSpec((tm, w), lambda i, pos: (i, 0))

    def per_layer(shape):
        return pl.BlockSpec((1,) + shape, lambda i, pos: (layer,) + (0,) * len(shape))

    return pl.pallas_call(
        _final_kernel,
        grid_spec=pltpu.PrefetchScalarGridSpec(
            num_scalar_prefetch=1,
            grid=(t // tm,),
            in_specs=[row(d), row(d), row(LANES), pl.BlockSpec((1, tm, PLE_DIM), lambda i, pos: (layer, i, 0)),
                      per_layer((d, d)), per_layer((1, d)), per_layer((PLE_DIM, d)),
                      per_layer((1, d)), per_layer((1, d)), pl.BlockSpec(memory_space=pl.ANY)],
            out_specs=[row(d), row(d)],
            scratch_shapes=[pltpu.VMEM((TOP_K, tm, d // 2), U32), pltpu.SemaphoreType.DMA],
        ),
        out_shape=[jax.ShapeDtypeStruct((t, d), F32), jax.ShapeDtypeStruct((t, d), BF16)],
        compiler_params=_cparams(1),
        name="combine_ple_ln",
    )(pos_flat, x1, x1b, gates, p, wg_bf, bg, wp_bf, ln_g, ln_b, ys)


W_PREP_ROWS = 256


def _arrange_kernel(w_ref, o_ref):
    cut = COL_GL + GLA_RANK
    rows = w_ref.shape[1]
    o_ref[0, :, :COL_GL] = w_ref[0, :, :COL_GL].astype(BF16)
    o_ref[0, :, COL_GL:COL_GR] = jnp.zeros((rows, GL_PAD), BF16)
    o_ref[0, :, COL_GL:cut] = w_ref[0, :, COL_GL:cut].astype(BF16)
    o_ref[0, :, COL_GR:] = w_ref[0, :, cut:].astype(BF16)


def _arrange_w_in(w):
    n_layer, k, n_in = w.shape
    assert sum(IN_SIZES[:5]) == COL_GL + GLA_RANK and n_in + GL_PAD - GLA_RANK == U_COLS
    return pl.pallas_call(
        _arrange_kernel,
        grid=(n_layer, k // W_PREP_ROWS),
        in_specs=[pl.BlockSpec((1, W_PREP_ROWS, n_in), lambda l, r: (l, r, 0))],
        out_specs=pl.BlockSpec((1, W_PREP_ROWS, U_COLS), lambda l, r: (l, r, 0)),
        out_shape=jax.ShapeDtypeStruct((n_layer, k, U_COLS), BF16),
        compiler_params=_cparams(2),
        name="arrange_w_in",
    )(w)


def kernel(x, p, w_in, conf_dw_w, conf_dw_b, conf_ln_g, conf_ln_b, gla_w_g2, gla_b_g2, gla_norm_g, sc_conv_w, mix_scale, w_out, rel_bias, ln1_g, ln1_b, router_g_w, router_g_b, router_e_w, router_e_b, exp_w_gate, exp_w_up, exp_w_down, ple_w_gate, ple_b_gate, ple_w_proj, ln2_g, ln2_b):
    bsz, seq, d = x.shape
    t = bsz * seq
    n_rows = TOP_K * t + N_EXPERTS * MOE_TILE
    n_tiles = n_rows // MOE_TILE
    bias_tab = _bias_tables(rel_bias)
    w_in_bf = _arrange_w_in(w_in)
    w_out_bf = w_out.astype(BF16)
    ple_wg_bf = ple_w_gate.astype(BF16)
    ple_wp_bf = ple_w_proj.astype(BF16)
    p3 = p.reshape(DEPTH, t, PLE_DIM)

    def vec(a):
        return a.reshape(DEPTH, 1, -1)

    xf = x.reshape(t, d)
    xb = xf
    for i in range(DEPTH):
        u = _in_proj(xb, w_in_bf, i, 512, 2048, F32)
        u3 = u.reshape(bsz, seq, U_COLS)
        ya, yd = _conv_mixers(u3, conf_dw_w[i], conf_dw_b[i], conf_ln_g[i], conf_ln_b[i], sc_conv_w[i])
        yb = _gla(u3, gla_w_g2[i], gla_b_g2[i], gla_norm_g[i])
        yc = _dilated(u3, bias_tab)
        rw = jnp.zeros((d, LANES), F32).at[:, :N_GROUPS].set(router_g_w[i]).at[:, N_GROUPS:N_GROUPS + N_EXPERTS].set(router_e_w[i])
        rb = jnp.zeros((1, LANES), F32).at[0, :N_GROUPS].set(router_g_b[i]).at[0, N_GROUPS:N_GROUPS + N_EXPERTS].set(router_e_b[i])
        rw_hi = rw.astype(BF16)
        rw_lo = (rw - rw_hi.astype(F32)).astype(BF16)
        x1, x1b, x1p, ids, gates = _outproj(i, xf,ya.reshape(t, -1), yb.reshape(t, -1), yc.reshape(t, -1), yd.reshape(t, -1),
                                       vec(mix_scale), w_out_bf, vec(ln1_g), vec(ln1_b), rw_hi, rw_lo, rb)
        pos, tile_tab = _plan(ids)
        pos_flat = pos[:, :TOP_K].T.reshape(-1)
        tile_end = tile_tab[:n_tiles, 0]
        n_active = jnp.sum((tile_end < N_EXPERTS).astype(I32)).reshape(1)
        tile_expert = jnp.minimum(tile_end, N_EXPERTS - 1)
        xs = _dispatch(pos_flat, x1p, n_rows)
        ys = _experts(i, tile_expert, n_active, xs, exp_w_gate, exp_w_up, exp_w_down)
        xf, xb = _final(i, pos_flat, x1, x1b, gates, p3, ple_wg_bf, vec(ple_b_gate), ple_wp_bf, vec(ln2_g), vec(ln2_b), ys)
    return xf.reshape(bsz, seq, d)
```

```python
import functools
import math

import jax
import jax.numpy as jnp
from jax import lax
from jax.experimental import pallas as pl
from jax.experimental.pallas import tpu as pltpu

F32 = jnp.float32
BF16 = jnp.bfloat16
I32 = jnp.int32
HIGHEST = lax.Precision.HIGHEST

D_MODEL = 2048
DEPTH = 2
PLE_DIM = 256
W_CONF = 512
W_GLA = 512
W_DIL = 512
W_SC = 512
CONF_KERNEL = 31
GLA_HEADS = 4
GLA_DV = 128
GLA_DK = 64
GLA_RANK = 16
GLA_TAU = 16.0
DIL_HEADS = 8
DIL_HD = 64
DIL_BRANCHES = ((128, 1), (512, 4), (2048, 16))
DIL_BLOCK = 128
REL_BUCKETS = 32
REL_MAX_DIST = 2048
SC_KERNEL = 3
N_GROUPS = 4
EXPERTS_PER_GROUP = 8
N_EXPERTS = 32
TOP_K = 2
D_EXPERT = 256
DN_ALPHA = (2 * DEPTH) ** 0.25
LN_EPS = 1e-5
RMS_EPS = 1e-6
IN_SIZES = (2 * W_CONF, GLA_HEADS * GLA_DK, GLA_HEADS * GLA_DK, W_GLA, GLA_RANK, W_GLA,
            W_DIL, W_DIL, W_DIL, W_SC, W_SC, W_SC)

LANES = 128
SUBLANES = 8
VMEM_LIMIT = 56 * 1024 * 1024

GL_PAD = 512
COL_A = 0
COL_GQ = 1024
COL_GK = 1280
COL_GV = 1536
COL_GL = 2048
COL_GR = 2560
COL_CQ = 3072
COL_CK = 3584
COL_CV = 4096
COL_SB = 4608
COL_SC = 5120
COL_SH = 5632
U_COLS = 6144

MOE_TILE = 512


def _cparams(n_axes):
    return pltpu.CompilerParams(dimension_semantics=("arbitrary",) * n_axes, vmem_limit_bytes=VMEM_LIMIT)


def _layer_norm(x, g, b):
    mu = jnp.mean(x, axis=-1, keepdims=True)
    xc = x - mu
    var = jnp.mean(xc * xc, axis=-1, keepdims=True)
    return xc * lax.rsqrt(var + LN_EPS) * g + b


U32 = jnp.uint32


def _pack_bf16_pairs(x):
    n = x.shape[1] // 2
    bits = lax.bitcast_convert_type(x.astype(BF16).astype(F32), U32)
    return (bits[:, :n] >> 16) | (bits[:, n:] & jnp.uint32(0xFFFF0000))


def _unpack_bf16_pairs(w):
    lo = lax.bitcast_convert_type(w << 16, F32)
    hi = lax.bitcast_convert_type(w & jnp.uint32(0xFFFF0000), F32)
    return jnp.concatenate([lo, hi], axis=1)


def _sigmoid(x):
    return 1.0 / (1.0 + jnp.exp(-x))


def _silu(x):
    return x * _sigmoid(x)


def _matmul_kernel(x_ref, w_ref, o_ref):
    o_ref[...] = jnp.dot(x_ref[...].astype(BF16), w_ref[0],
                         preferred_element_type=F32).astype(o_ref.dtype)


def _in_proj(x, w_all, layer, tm, tn, out_dtype):
    m, k = x.shape
    n = w_all.shape[2]
    return pl.pallas_call(
        _matmul_kernel,
        grid=(n // tn, m // tm),
        in_specs=[pl.BlockSpec((tm, k), lambda j, i: (i, 0)),
                  pl.BlockSpec((1, k, tn), lambda j, i: (layer, 0, j))],
        out_specs=pl.BlockSpec((tm, tn), lambda j, i: (i, j)),
        out_shape=jax.ShapeDtypeStruct((m, n), out_dtype),
        compiler_params=_cparams(2),
        name="in_proj",
    )(x, w_all)


CONV_TS = 512
CONV_RC = 64
CONF_HIST = 32
SC_HIST = 8


def _conv_kernel(a_ref, gate_ref, sb_ref, sc_ref, sh_ref, cw_ref, cb_ref, lg_ref, lb_ref, sw_ref,
                 ya_ref, yd_ref, ha, hd, hs):
    s = pl.program_id(1)
    ts = a_ref.shape[1]
    n_ext = CONF_HIST + ts

    @pl.when(s == 0)
    def _():
        ha[0:CONF_HIST, :] = jnp.zeros((CONF_HIST, W_CONF), F32)
        hd[0:SC_HIST, :] = jnp.zeros((SC_HIST, W_SC), F32)

    ha[CONF_HIST:CONF_HIST + ts, :] = a_ref[0] * _sigmoid(gate_ref[0])
    hd[SC_HIST:SC_HIST + ts, :] = sc_ref[0] * sh_ref[0]

    cb = cb_ref[...]
    lg = lg_ref[...]
    lb = lb_ref[...]
    off_a = CONF_HIST - (CONF_KERNEL - 1)
    h_all = ha[...]
    for sh in range(1, SUBLANES):
        hs[sh - 1] = pltpu.roll(h_all, n_ext - sh, 0)
    for c in range(ts // CONV_RC):
        r0 = c * CONV_RC
        acc = jnp.zeros((CONV_RC, W_CONF), F32)
        for j in range(CONF_KERNEL):
            sh = (off_a + j) % SUBLANES
            lo = r0 + off_a + j - sh
            rows = ha[lo:lo + CONV_RC, :] if sh == 0 else hs[sh - 1, lo:lo + CONV_RC, :]
            acc = acc + cw_ref[j:j + 1, :] * rows
        hh = _layer_norm(acc + cb, lg, lb)
        ya_ref[0, r0:r0 + CONV_RC, :] = _silu(hh)

    off_d = SC_HIST - (SC_KERNEL - 1)
    accd = jnp.zeros((ts, W_SC), F32)
    for j in range(SC_KERNEL):
        accd = accd + sw_ref[j:j + 1, :] * hd[off_d + j:off_d + j + ts, :]
    yd_ref[0] = sb_ref[0] * accd

    ha[0:CONF_HIST, :] = ha[ts:ts + CONF_HIST, :]
    hd[0:SC_HIST, :] = hd[ts:ts + SC_HIST, :]


def _conv_mixers(u3, conf_w, conf_b, conf_lg, conf_lb, sc_w):
    bsz, seq, _ = u3.shape
    ts = CONV_TS
    wblk = W_CONF

    def col(c):
        return pl.BlockSpec((1, ts, wblk), lambda b, s, c=c: (b, s, c // wblk))

    def full(shape):
        return pl.BlockSpec(shape, lambda b, s: (0,) * len(shape))

    cw = jnp.zeros((32, W_CONF), F32).at[:CONF_KERNEL].set(conf_w)
    sw = jnp.zeros((8, W_SC), F32).at[:SC_KERNEL].set(sc_w)
    out_spec = pl.BlockSpec((1, ts, wblk), lambda b, s: (b, s, 0))
    return pl.pallas_call(
        _conv_kernel,
        grid=(bsz, seq // ts),
        in_specs=[col(COL_A), col(COL_A + W_CONF), col(COL_SB), col(COL_SC), col(COL_SH),
                  full((32, W_CONF)), full((1, W_CONF)), full((1, W_CONF)), full((1, W_CONF)), full((8, W_SC))],
        out_specs=[out_spec, out_spec],
        out_shape=[jax.ShapeDtypeStruct((bsz, seq, W_CONF), F32), jax.ShapeDtypeStruct((bsz, seq, W_SC), F32)],
        scratch_shapes=[pltpu.VMEM((CONF_HIST + ts, W_CONF), F32), pltpu.VMEM((SC_HIST + ts, W_SC), F32),
                        pltpu.VMEM((SUBLANES - 1, CONF_HIST + ts, W_CONF), F32)],
        compiler_params=_cparams(2),
        name="conv_mixers",
    )(u3, u3, u3, u3, u3, cw, conf_b.reshape(1, -1), conf_lg.reshape(1, -1), conf_lb.reshape(1, -1), sw)


GLA_BLK = 128
GLA_SUB = 64
GLA_UNROLL = 4


def _split3(x):
    hi = x.astype(BF16)
    r1 = x - hi.astype(F32)
    mid = r1.astype(BF16)
    lo = (r1 - mid.astype(F32)).astype(BF16)
    return hi, mid, lo


def _gla_kernel(q_ref, k_ref, v_ref, gl_ref, r_ref, wg2h_ref, wg2l_ref, bg2_ref, ng_ref, tril_ref, o_ref,
                st_ref, la_ref):
    seq = q_ref.shape[1]
    nblk = seq // GLA_BLK
    nsub = GLA_BLK // GLA_SUB
    assert nsub == 2
    st_ref[...] = jnp.zeros(st_ref.shape, F32)

    lane = lax.broadcasted_iota(I32, (1, LANES), 1)
    head_mask = [(lane < GLA_DK).astype(F32), (lane >= GLA_DK).astype(F32)]
    row = lax.broadcasted_iota(I32, (GLA_BLK, GLA_BLK), 0)
    col = lax.broadcasted_iota(I32, (GLA_BLK, GLA_BLK), 1)
    sub_shift = GLA_SUB.bit_length() - 1
    diag_mask = ((row >> sub_shift) == (col >> sub_shift)) & (col <= row)
    second_half = lax.broadcasted_iota(I32, (GLA_BLK, 1), 0) >= GLA_SUB
    scale = GLA_DK ** -0.5
    ng = ng_ref[...]
    tril = tril_ref[...]

    gl = gl_ref[0]
    gl_hi = gl.astype(BF16)
    gl_lo = (gl - gl_hi.astype(F32)).astype(BF16)
    wh = wg2h_ref[...]
    z = (jnp.dot(gl_hi, wh, preferred_element_type=F32) + jnp.dot(gl_lo, wh, preferred_element_type=F32)
         + jnp.dot(gl_hi, wg2l_ref[...], preferred_element_type=F32) + bg2_ref[...])
    la_ref[...] = (jnp.minimum(z, 0.0) - jnp.log(1.0 + jnp.exp(-jnp.abs(z)))) * (1.0 / GLA_TAU)

    def body(s, carry):
        r0 = pl.multiple_of(s * GLA_BLK, GLA_BLK)
        q = q_ref[0, pl.ds(r0, GLA_BLK), :] * scale
        k = k_ref[0, pl.ds(r0, GLA_BLK), :]
        v = v_ref[0, pl.ds(r0, GLA_BLK), :].astype(BF16)
        b = None
        for part in _split3(la_ref[pl.ds(r0, GLA_BLK), :]):
            t = jnp.dot(tril, part, preferred_element_type=F32)
            b = t if b is None else b + t

        b3 = b.reshape(nsub, GLA_SUB, LANES)
        mid = jnp.broadcast_to(b3[:, GLA_SUB // 2 - 1:GLA_SUB // 2, :], b3.shape).reshape(GLA_BLK, LANES)
        qd = q * jnp.exp(b - mid)
        kd = (k * jnp.exp(mid - b)).astype(BF16)
        e0 = b[GLA_SUB - 1:GLA_SUB, :]
        q_off = jnp.where(second_half, q * jnp.exp(jnp.minimum(b - e0, 0.0)), 0.0)
        k_off = jnp.where(second_half, 0.0, k * jnp.exp(jnp.minimum(e0 - b, 0.0))).astype(BF16)
        q_st = q * jnp.exp(b)
        b_last = b[GLA_BLK - 1:GLA_BLK, :]
        kk = (k * jnp.exp(b_last - b)).astype(BF16)
        st = st_ref[...]
        st_bf = st.astype(BF16)

        outs = []
        for h in range(2):
            hm = head_mask[h]
            s_d = lax.dot_general((qd * hm).astype(BF16), kd, (((1,), (1,)), ((), ())), preferred_element_type=F32)
            s_o = lax.dot_general((q_off * hm).astype(BF16), k_off, (((1,), (1,)), ((), ())),
                                  preferred_element_type=F32)
            scores = jnp.where(diag_mask, s_d, 0.0) + s_o
            lhs = jnp.concatenate([scores.astype(BF16), (q_st * hm).astype(BF16)], axis=1)
            rhs = jnp.concatenate([v[:, h * GLA_DV:(h + 1) * GLA_DV], st_bf[:, h * GLA_DV:(h + 1) * GLA_DV]], axis=0)
            o_h = jnp.dot(lhs, rhs, preferred_element_type=F32)
            o_h = o_h * lax.rsqrt(jnp.mean(o_h * o_h, axis=-1, keepdims=True) + RMS_EPS)
            outs.append(o_h)
        o = jnp.concatenate(outs, axis=1) * ng
        o_ref[0, pl.ds(r0, GLA_BLK), :] = o * _silu(r_ref[0, pl.ds(r0, GLA_BLK), :])

        upd = lax.dot_general(kk, v, (((0,), (0,)), ((), ())), preferred_element_type=F32)
        decay = jnp.transpose(jnp.broadcast_to(jnp.exp(b_last), (GLA_BLK, LANES)))
        st_ref[...] = st * jnp.concatenate([decay, decay], axis=1) + upd
        return carry

    lax.fori_loop(0, nblk, body, 0, unroll=GLA_UNROLL)


def _gla(u3, w_g2, b_g2, norm_g):
    bsz, seq, _ = u3.shape
    npair = GLA_HEADS // 2
    kw = 2 * GLA_DK
    vw = 2 * GLA_DV

    def col(c, w):
        return pl.BlockSpec((1, seq, w), lambda b, p, c=c, w=w: (b, 0, c // w + p))

    wg2 = jnp.zeros((LANES, GLA_HEADS * GLA_DK), F32).at[:GLA_RANK].set(w_g2)
    wg2_hi = wg2.astype(BF16)
    wg2_lo = (wg2 - wg2_hi.astype(F32)).astype(BF16)
    ng = jnp.concatenate([norm_g, norm_g]).reshape(1, vw)
    tril = jnp.tril(jnp.ones((GLA_BLK, GLA_BLK), BF16))
    return pl.pallas_call(
        _gla_kernel,
        grid=(bsz, npair),
        in_specs=[col(COL_GQ, kw), col(COL_GK, kw), col(COL_GV, vw),
                  pl.BlockSpec((1, seq, LANES), lambda b, p: (b, 0, COL_GL // LANES)),
                  col(COL_GR, vw),
                  pl.BlockSpec((LANES, kw), lambda b, p: (0, p)),
                  pl.BlockSpec((LANES, kw), lambda b, p: (0, p)),
                  pl.BlockSpec((1, kw), lambda b, p: (0, p)),
                  pl.BlockSpec((1, vw), lambda b, p: (0, 0)),
                  pl.BlockSpec((GLA_BLK, GLA_BLK), lambda b, p: (0, 0))],
        out_specs=pl.BlockSpec((1, seq, vw), lambda b, p: (b, 0, p)),
        out_shape=jax.ShapeDtypeStruct((bsz, seq, W_GLA), F32),
        scratch_shapes=[pltpu.VMEM((kw, vw), F32), pltpu.VMEM((seq, LANES), F32)],
        compiler_params=_cparams(2),
        name="gla",
    )(u3, u3, u3, u3, u3, wg2_hi, wg2_lo, b_g2.reshape(1, -1), ng, tril)


def _t5_bucket(dist):
    max_exact = REL_BUCKETS // 2
    large = max_exact + (jnp.log(jnp.maximum(dist, 1).astype(F32) / max_exact)
                         / math.log(REL_MAX_DIST / max_exact) * (REL_BUCKETS - max_exact)).astype(I32)
    large = jnp.minimum(large, REL_BUCKETS - 1)
    return jnp.where(dist < max_exact, dist, large)


LOG2E = math.log2(math.e)
DIL_GROUP = 16


def _bias_kernel(rel_ref, bucket_ref, o_ref):
    h = pl.program_id(1)
    blk = DIL_BLOCK
    bucket = bucket_ref[0]
    acc = jnp.zeros(bucket.shape, F32)
    for b in range(REL_BUCKETS):
        acc = jnp.where(bucket == b, rel_ref[b, h], acc)
    acc = acc * LOG2E
    qi = lax.broadcasted_iota(I32, (blk, 2 * blk), 0)
    kj = lax.broadcasted_iota(I32, (blk, 2 * blk), 1)
    steps = qi + blk - kj
    in_window = (steps >= 0) & (steps <= blk)
    neg = jnp.float32(-jnp.inf)
    o_ref[0, 0, 0] = jnp.where(in_window, acc, neg)
    o_ref[0, 0, 1] = jnp.where(in_window & (kj >= blk), acc, neg)


def _bias_tables(rel_bias):
    assert all(window // dil == DIL_BLOCK for window, dil in DIL_BRANCHES)
    qi = jnp.arange(DIL_BLOCK)[:, None]
    kj = jnp.arange(2 * DIL_BLOCK)[None, :]
    steps = jnp.maximum(qi + DIL_BLOCK - kj, 0)
    buckets = jnp.stack([_t5_bucket(steps * dil) for _, dil in DIL_BRANCHES]).astype(I32)
    nbr = len(DIL_BRANCHES)
    return pl.pallas_call(
        _bias_kernel,
        grid=(nbr, DIL_HEADS),
        in_specs=[pl.BlockSpec(memory_space=pltpu.SMEM),
                  pl.BlockSpec((1, DIL_BLOCK, 2 * DIL_BLOCK), lambda i, h: (i, 0, 0))],
        out_specs=pl.BlockSpec((1, 1, 2, DIL_BLOCK, 2 * DIL_BLOCK), lambda i, h: (i, h, 0, 0, 0)),
        out_shape=jax.ShapeDtypeStruct((nbr, DIL_HEADS, 2, DIL_BLOCK, 2 * DIL_BLOCK), F32),
        compiler_params=_cparams(2),
        name="rel_bias_tables",
    )(rel_bias, buckets)


def _dil_kernel(q_ref, k_ref, v_ref, bias_ref, o_ref, qs, ks, vs, ob, lb):
    seq = q_ref.shape[1]
    blk = DIL_BLOCK
    scale = DIL_HD ** -0.5 * LOG2E
    lane = lax.broadcasted_iota(I32, (1, LANES), 1)
    first = lane < DIL_HD
    head_mask = [first.astype(F32), 1.0 - first.astype(F32)]

    ks[0:blk, :] = jnp.zeros((blk, LANES), BF16)
    vs[0:blk, :] = jnp.zeros((blk, LANES), BF16)

    for bi, (window, dil) in enumerate(DIL_BRANCHES):
        sub_len = seq // dil
        nb = sub_len // blk
        nb_shift = nb.bit_length() - 1
        for r in range(dil):
            rows = pl.ds(r, sub_len, stride=dil) if dil > 1 else pl.ds(0, sub_len)
            q = q_ref[0, rows, :] * scale
            for h in range(2):
                qs[h, r * sub_len:(r + 1) * sub_len, :] = (q * head_mask[h]).astype(BF16)
            ks[blk + r * sub_len:blk + (r + 1) * sub_len, :] = k_ref[0, rows, :].astype(BF16)
            vs[blk + r * sub_len:blk + (r + 1) * sub_len, :] = v_ref[0, rows, :].astype(BF16)

        def body(it, carry, bi=bi, dil=dil, nb=nb, nb_shift=nb_shift):
            for gg in range(DIL_GROUP):
                g = it * DIL_GROUP + gg
                n = g & (nb - 1)
                r = g >> nb_shift
                g0 = pl.multiple_of(g * blk, blk)
                kb = ks[pl.ds(g0, 2 * blk), :]
                vb = vs[pl.ds(g0, 2 * blk), :]
                variant = jnp.where(n == 0, 1, 0)
                o_acc = None
                l_acc = None
                for h in range(2):
                    sc = lax.dot_general(qs[h, pl.ds(g0, blk), :], kb, (((1,), (1,)), ((), ())),
                                         preferred_element_type=F32)
                    sc = sc + bias_ref[bi, h, variant]
                    m = jnp.max(sc, axis=-1, keepdims=True)
                    p = jnp.exp2(sc - m)
                    l = jnp.sum(p, axis=-1, keepdims=True)
                    o_h = jnp.dot(p.astype(BF16), vb, preferred_element_type=F32) * (1.0 / l)
                    lse = jnp.broadcast_to(m + jnp.log2(l), (blk, LANES))
                    if h == 0:
                        o_acc, l_acc = o_h, lse
                    else:
                        o_acc = jnp.where(first, o_acc, o_h)
                        l_acc = jnp.where(first, l_acc, lse)
                if dil > 1:
                    dst = pl.ds(n * (blk * dil) + r, blk, stride=dil)
                else:
                    dst = pl.ds(g0, blk)
                ob[bi, dst, :] = o_acc
                lb[bi, dst, :] = l_acc
            return carry

        lax.fori_loop(0, seq // blk // DIL_GROUP, body, 0)

    nbr = len(DIL_BRANCHES)
    mc = 256
    for c in range(seq // mc):
        rs = slice(c * mc, (c + 1) * mc)
        ls = [lb[i, rs, :] for i in range(nbr)]
        m = functools.reduce(jnp.maximum, ls)
        ws = [jnp.exp2(l - m) for l in ls]
        num = functools.reduce(lambda a, b: a + b, [w * ob[i, rs, :] for i, w in enumerate(ws)])
        den = functools.reduce(lambda a, b: a + b, ws)
        o_ref[0, rs, :] = num / den


def _dilated(u3, bias_tab):
    bsz, seq, _ = u3.shape
    npair = DIL_HEADS // 2
    nbr = len(DIL_BRANCHES)

    def col(c):
        return pl.BlockSpec((1, seq, LANES), lambda b, p, c=c: (b, 0, c // LANES + p))

    return pl.pallas_call(
        _dil_kernel,
        grid=(bsz, npair),
        in_specs=[col(COL_CQ), col(COL_CK), col(COL_CV),
                  pl.BlockSpec((nbr, 2, 2, DIL_BLOCK, 2 * DIL_BLOCK), lambda b, p: (0, p, 0, 0, 0))],
        out_specs=pl.BlockSpec((1, seq, LANES), lambda b, p: (b, 0, p)),
        out_shape=jax.ShapeDtypeStruct((bsz, seq, W_DIL), F32),
        scratch_shapes=[pltpu.VMEM((2, seq, LANES), BF16),
                        pltpu.VMEM((DIL_BLOCK + seq, LANES), BF16),
                        pltpu.VMEM((DIL_BLOCK + seq, LANES), BF16),
                        pltpu.VMEM((nbr, seq, LANES), F32),
                        pltpu.VMEM((nbr, seq, LANES), F32)],
        compiler_params=_cparams(2),
        name="dilated_attn",
    )(u3, u3, u3, bias_tab)


TOK_TILE = 256
OUT_TILE = 512
PLAN_TILE = 1024


def _outproj_kernel(x_ref, ya_ref, yb_ref, yc_ref, yd_ref, ms_ref, wo_ref, g_ref, b_ref, rwh_ref, rwl_ref, rb_ref,
                    x1_ref, x1b_ref, x1p_ref, ids_ref, gates_ref):
    acc = None
    for gi, y_ref in enumerate((ya_ref, yb_ref, yc_ref, yd_ref)):
        lo = gi * W_CONF
        y = (y_ref[...] * ms_ref[0, :, lo:lo + W_CONF]).astype(BF16)
        part = jnp.dot(y, wo_ref[0, lo:lo + W_CONF, :], preferred_element_type=F32)
        acc = part if acc is None else acc + part
    x1 = _layer_norm(DN_ALPHA * x_ref[...] + acc, g_ref[0], b_ref[0])
    x1_ref[...] = x1
    x1_hi = x1.astype(BF16)
    x1b_ref[...] = x1_hi
    x1p_ref[...] = _pack_bf16_pairs(x1)

    x1_lo = (x1 - x1_hi.astype(F32)).astype(BF16)
    rwh = rwh_ref[...]
    logits = (jnp.dot(x1_hi, rwh, preferred_element_type=F32) + jnp.dot(x1_lo, rwh, preferred_element_type=F32)
              + jnp.dot(x1_hi, rwl_ref[...], preferred_element_type=F32) + rb_ref[...])
    tm = logits.shape[0]
    lane = lax.broadcasted_iota(I32, (tm, LANES), 1)
    neg = jnp.float32(-jnp.inf)
    gl = jnp.where(lane < N_GROUPS, logits, neg)
    gmax = jnp.max(gl, axis=-1, keepdims=True)
    g_top = 1.0 / jnp.sum(jnp.exp(gl - gmax), axis=-1, keepdims=True)
    g_idx = jnp.min(jnp.where(gl == gmax, lane, LANES), axis=-1, keepdims=True)
    e_lo = N_GROUPS + g_idx * EXPERTS_PER_GROUP
    el = jnp.where((lane >= e_lo) & (lane < e_lo + EXPERTS_PER_GROUP), logits, neg)
    m1 = jnp.max(el, axis=-1, keepdims=True)
    i1 = jnp.min(jnp.where(el == m1, lane, LANES), axis=-1, keepdims=True)
    el2 = jnp.where(lane == i1, neg, el)
    m2 = jnp.max(el2, axis=-1, keepdims=True)
    i2 = jnp.min(jnp.where(el2 == m2, lane, LANES), axis=-1, keepdims=True)
    e21 = jnp.exp(m2 - m1)
    w1 = g_top / (1.0 + e21)
    w2 = g_top * e21 / (1.0 + e21)
    ids_ref[...] = jnp.where(lane == 0, i1 - N_GROUPS, jnp.where(lane == 1, i2 - N_GROUPS, N_EXPERTS))
    gates_ref[...] = jnp.where(lane == 0, w1, jnp.where(lane == 1, w2, 0.0))


def _outproj(layer, x, ya, yb, yc, yd, mix_scale, w_out_bf, ln_g, ln_b, rw_hi, rw_lo, rb):
    t, d = x.shape
    tm = OUT_TILE

    def row(w):
        return pl.BlockSpec((tm, w), lambda i: (i, 0))

    def full(shape):
        return pl.BlockSpec(shape, lambda i: (0,) * len(shape))

    def per_layer(shape):
        return pl.BlockSpec((1,) + shape, lambda i: (layer,) + (0,) * len(shape))

    return pl.pallas_call(
        _outproj_kernel,
        grid=(t // tm,),
        in_specs=[row(d), row(W_CONF), row(W_GLA), row(W_DIL), row(W_SC), per_layer((1, d)), per_layer((d, d)),
                  per_layer((1, d)), per_layer((1, d)), full((d, LANES)), full((d, LANES)), full((1, LANES))],
        out_specs=[row(d), row(d), row(d // 2), row(LANES), row(LANES)],
        out_shape=[jax.ShapeDtypeStruct((t, d), F32), jax.ShapeDtypeStruct((t, d), BF16),
                   jax.ShapeDtypeStruct((t, d // 2), U32),
                   jax.ShapeDtypeStruct((t, LANES), I32), jax.ShapeDtypeStruct((t, LANES), F32)],
        compiler_params=_cparams(1),
        name="out_proj_ln_router",
    )(x, ya, yb, yc, yd, mix_scale, w_out_bf, ln_g, ln_b, rw_hi, rw_lo, rb)


def _plan_kernel(ids_ref, tril_ref, upper_ref, pos_ref, tile_ref, cnt, carry, start):
    phase = pl.program_id(0)
    i = pl.program_id(1)
    tm = ids_ref.shape[0]
    lane = lax.broadcasted_iota(I32, (tm, LANES), 1)
    ids = ids_ref[...]
    e1 = ids[:, 0:1]
    e2 = ids[:, 1:2]
    oh1 = (lane == e1).astype(F32)
    oh2 = (lane == e2).astype(F32)
    oh = oh1 + oh2

    @pl.when((phase == 0) & (i == 0))
    def _():
        cnt[...] = jnp.zeros(cnt.shape, F32)

    @pl.when(phase == 0)
    def _():
        cnt[...] = cnt[...] + jnp.sum(oh, axis=0, keepdims=True)

    @pl.when((phase == 1) & (i == 0))
    def _():
        padded = jnp.floor((cnt[...] + (MOE_TILE - 1)) * (1.0 / MOE_TILE)) * MOE_TILE
        p8 = jnp.broadcast_to(padded, (SUBLANES, LANES))
        st = jnp.dot(p8, upper_ref[...], precision=HIGHEST, preferred_element_type=F32)
        start[...] = st[0:1, :]
        carry[...] = jnp.zeros(carry.shape, F32)
        end = st[0:1, :] + padded
        trow = lax.broadcasted_iota(I32, (LANES, LANES), 0).astype(F32) * MOE_TILE
        tlane = lax.broadcasted_iota(I32, (LANES, LANES), 1)
        done = jnp.where((end <= trow) & (tlane < N_EXPERTS), 1.0, 0.0)
        tile_ref[...] = jnp.broadcast_to(jnp.sum(done, axis=-1, keepdims=True), (LANES, LANES)).astype(I32)

    @pl.when(phase == 1)
    def _():
        incl = jnp.dot(tril_ref[...], oh.astype(BF16), preferred_element_type=F32) + carry[...]
        base = start[...] + incl - oh
        p1 = jnp.sum(oh1 * base, axis=-1, keepdims=True)
        p2 = jnp.sum(oh2 * base, axis=-1, keepdims=True)
        pos_ref[...] = jnp.where(lane == 0, p1, jnp.where(lane == 1, p2, 0.0)).astype(I32)
        carry[...] = carry[...] + jnp.sum(oh, axis=0, keepdims=True)


def _plan(ids):
    t = ids.shape[0]
    tm = PLAN_TILE
    tril = jnp.tril(jnp.ones((tm, tm), BF16))
    upper = jnp.triu(jnp.ones((LANES, LANES), F32), k=1)
    return pl.pallas_call(
        _plan_kernel,
        grid=(2, t // tm),
        in_specs=[pl.BlockSpec((tm, LANES), lambda ph, i: (i, 0)),
                  pl.BlockSpec((tm, tm), lambda ph, i: (0, 0)),
                  pl.BlockSpec((LANES, LANES), lambda ph, i: (0, 0))],
        out_specs=[pl.BlockSpec((tm, LANES), lambda ph, i: (i * ph, 0)),
                   pl.BlockSpec((LANES, LANES), lambda ph, i: (0, 0))],
        out_shape=[jax.ShapeDtypeStruct((t, LANES), I32), jax.ShapeDtypeStruct((LANES, LANES), I32)],
        scratch_shapes=[pltpu.VMEM((1, LANES), F32), pltpu.VMEM((1, LANES), F32), pltpu.VMEM((1, LANES), F32)],
        compiler_params=_cparams(2),
        name="route_plan",
    )(ids, tril, upper)


def _dispatch_kernel(pos_ref, x_ref, xs_ref, sem):
    i = pl.program_id(0)
    tm = x_ref.shape[0]
    t = pl.num_programs(0) * tm

    def copy(m, k):
        dst = pos_ref[k * t + i * tm + m]
        return pltpu.make_async_copy(x_ref.at[pl.ds(m, 1)], xs_ref.at[pl.ds(dst, 1)], sem)

    def start(m, c):
        copy(m, 0).start(priority=0)
        copy(m, 1).start(priority=1)
        return c

    for m in range(tm):
        start(m, 0)
    for _ in range(TOP_K):
        pltpu.make_async_copy(x_ref, xs_ref.at[pl.ds(0, tm)], sem).wait()


def _dispatch(pos_flat, x1, n_rows):
    t, d = x1.shape
    tm = TOK_TILE
    return pl.pallas_call(
        _dispatch_kernel,
        grid_spec=pltpu.PrefetchScalarGridSpec(
            num_scalar_prefetch=1,
            grid=(t // tm,),
            in_specs=[pl.BlockSpec((tm, d), lambda i, pos: (i, 0))],
            out_specs=pl.BlockSpec(memory_space=pl.ANY),
            scratch_shapes=[pltpu.SemaphoreType.DMA],
        ),
        out_shape=jax.ShapeDtypeStruct((n_rows, d), x1.dtype),
        compiler_params=_cparams(1),
        name="moe_dispatch",
    )(pos_flat, x1)


def _expert_kernel(te_ref, na_ref, xs_ref, wg_ref, wu_ref, wd_ref, ys_ref):
    i = pl.program_id(0)

    @pl.when(i < na_ref[0])
    def _():
        xb = _unpack_bf16_pairs(xs_ref[...]).astype(BF16)
        g = jnp.dot(xb, wg_ref[0, 0].astype(BF16), preferred_element_type=F32)
        u = jnp.dot(xb, wu_ref[0, 0].astype(BF16), preferred_element_type=F32)
        h = (_silu(g) * u).astype(BF16)
        ys_ref[...] = _pack_bf16_pairs(jnp.dot(h, wd_ref[0, 0].astype(BF16), preferred_element_type=F32))


def _experts(layer, tile_expert, n_active, xs, w_gate, w_up, w_down):
    n_rows, d = xs.shape
    n_tiles = n_rows // MOE_TILE
    d_model = w_gate.shape[2]

    def rows(i, te, na):
        return (jnp.minimum(i, na[0] - 1), 0)

    def expert(i, te, na):
        return (layer, te[i], 0, 0)

    return pl.pallas_call(
        _expert_kernel,
        grid_spec=pltpu.PrefetchScalarGridSpec(
            num_scalar_prefetch=2,
            grid=(n_tiles,),
            in_specs=[pl.BlockSpec((MOE_TILE, d), rows),
                      pl.BlockSpec((1, 1, d_model, D_EXPERT), expert),
                      pl.BlockSpec((1, 1, d_model, D_EXPERT), expert),
                      pl.BlockSpec((1, 1, D_EXPERT, d_model), expert)],
            out_specs=pl.BlockSpec((MOE_TILE, d), rows),
        ),
        out_shape=jax.ShapeDtypeStruct((n_rows, d), U32),
        compiler_params=_cparams(1),
        name="moe_experts",
    )(tile_expert, n_active, xs, w_gate, w_up, w_down)


def _final_kernel(pos_ref, x1_ref, x1b_ref, gates_ref, p_ref, wg_ref, bg_ref, wp_ref, g_ref, b_ref, ys_ref,
                  x2_ref, x2b_ref, ybuf, sem):
    i = pl.program_id(0)
    n_steps = pl.num_programs(0)
    tm = x1_ref.shape[0]
    t = n_steps * tm

    def gather(step, slot):
        for m in range(tm):
            for k in range(TOP_K):
                src = pos_ref[k * t + step * tm + m]
                pltpu.make_async_copy(ys_ref.at[pl.ds(src, 1)], ybuf.at[slot, k, pl.ds(m, 1)],
                                      sem.at[slot]).start(priority=k)

    def wait_slot(slot):
        for k in range(TOP_K):
            pltpu.make_async_copy(ys_ref.at[pl.ds(0, tm)], ybuf.at[slot, k], sem.at[slot]).wait()

    @pl.when(i == 0)
    def _():
        gather(0, 0)

    slot = i % 2
    gather(jnp.minimum(i + 1, n_steps - 1), 1 - slot)
    gate = _sigmoid(jnp.dot(x1b_ref[...], wg_ref[0], preferred_element_type=F32) + bg_ref[0])
    ple = gate * jnp.dot(p_ref[0].astype(BF16), wp_ref[0], preferred_element_type=F32)
    wait_slot(slot)
    gates = gates_ref[...]
    ffn = (gates[:, 0:1] * _unpack_bf16_pairs(ybuf[slot, 0])
           + gates[:, 1:2] * _unpack_bf16_pairs(ybuf[slot, 1]))

    @pl.when(i == n_steps - 1)
    def _():
        wait_slot(1 - slot)

    x2 = _layer_norm(DN_ALPHA * x1_ref[...] + ffn + ple, g_ref[0], b_ref[0])
    x2_ref[...] = x2
    x2b_ref[...] = x2.astype(BF16)


def _final(layer, pos_flat, x1, x1b, gates, p, wg_bf, bg, wp_bf, ln_g, ln_b, ys):
    t, d = x1.shape
    tm = TOK_TILE

    def row(w):
        return pl.BlockSpec((tm, w), lambda i, pos: (i, 0))

    def per_layer(shape):
        return pl.BlockSpec((1,) + shape, lambda i, pos: (layer,) + (0,) * len(shape))

    return pl.pallas_call(
        _final_kernel,
        grid_spec=pltpu.PrefetchScalarGridSpec(
            num_scalar_prefetch=1,
            grid=(t // tm,),
            in_specs=[row(d), row(d), row(LANES), pl.BlockSpec((1, tm, PLE_DIM), lambda i, pos: (layer, i, 0)),
                      per_layer((d, d)), per_layer((1, d)), per_layer((PLE_DIM, d)),
                      per_layer((1, d)), per_layer((1, d)), pl.BlockSpec(memory_space=pl.ANY)],
            out_specs=[row(d), row(d)],
            scratch_shapes=[pltpu.VMEM((2, TOP_K, tm, d // 2), U32), pltpu.SemaphoreType.DMA((2,))],
        ),
        out_shape=[jax.ShapeDtypeStruct((t, d), F32), jax.ShapeDtypeStruct((t, d), BF16)],
        compiler_params=_cparams(1),
        name="combine_ple_ln",
    )(pos_flat, x1, x1b, gates, p, wg_bf, bg, wp_bf, ln_g, ln_b, ys)


W_PREP_COLS = 512


def _arrange_kernel(wt_ref, o_ref):
    j = pl.program_id(1)
    wt = wt_ref[0]
    row = lax.broadcasted_iota(I32, (wt.shape[0], 1), 0)
    n_valid = jnp.where(j == COL_GL // W_PREP_COLS, GLA_RANK, W_PREP_COLS)
    o_ref[0] = jnp.where(row < n_valid, wt, 0.0).T.astype(BF16)


def _arrange_w_in(w):
    n_layer, k, n_in = w.shape
    assert sum(IN_SIZES[:5]) == COL_GL + GLA_RANK and n_in + GL_PAD - GLA_RANK == U_COLS
    assert COL_GL % W_PREP_COLS == 0 and GL_PAD == W_PREP_COLS
    gl_blk = COL_GL // W_PREP_COLS

    def src_row(l, j):
        row = jnp.where(j <= gl_blk, j * W_PREP_COLS, j * W_PREP_COLS - (GL_PAD - GLA_RANK))
        return (l, pl.multiple_of(row, GLA_RANK), 0)

    return pl.pallas_call(
        _arrange_kernel,
        grid=(n_layer, U_COLS // W_PREP_COLS),
        in_specs=[pl.BlockSpec((pl.Element(1), pl.Element(W_PREP_COLS), pl.Element(k)), src_row)],
        out_specs=pl.BlockSpec((1, k, W_PREP_COLS), lambda l, j: (l, 0, j)),
        out_shape=jax.ShapeDtypeStruct((n_layer, k, U_COLS), BF16),
        compiler_params=_cparams(2),
        name="arrange_w_in",
    )(jnp.swapaxes(w, 1, 2))


def kernel(x, p, w_in, conf_dw_w, conf_dw_b, conf_ln_g, conf_ln_b, gla_w_g2, gla_b_g2, gla_norm_g, sc_conv_w, mix_scale, w_out, rel_bias, ln1_g, ln1_b, router_g_w, router_g_b, router_e_w, router_e_b, exp_w_gate, exp_w_up, exp_w_down, ple_w_gate, ple_b_gate, ple_w_proj, ln2_g, ln2_b):
    bsz, seq, d = x.shape
    t = bsz * seq
    n_rows = TOP_K * t + N_EXPERTS * MOE_TILE
    n_tiles = n_rows // MOE_TILE
    bias_tab = _bias_tables(rel_bias)
    w_in_bf = _arrange_w_in(w_in)
    w_out_bf = w_out.astype(BF16)
    ple_wg_bf = ple_w_gate.astype(BF16)
    ple_wp_bf = ple_w_proj.astype(BF16)
    p3 = p.reshape(DEPTH, t, PLE_DIM)

    def vec(a):
        return a.reshape(DEPTH, 1, -1)

    xf = x.reshape(t, d)
    xb = xf
    for i in range(DEPTH):
        u = _in_proj(xb, w_in_bf, i, 512, 2048, F32)
        u3 = u.reshape(bsz, seq, U_COLS)
        ya, yd = _conv_mixers(u3, conf_dw_w[i], conf_dw_b[i], conf_ln_g[i], conf_ln_b[i], sc_conv_w[i])
        yb = _gla(u3, gla_w_g2[i], gla_b_g2[i], gla_norm_g[i])
        yc = _dilated(u3, bias_tab)
        rw = jnp.zeros((d, LANES), F32).at[:, :N_GROUPS].set(router_g_w[i]).at[:, N_GROUPS:N_GROUPS + N_EXPERTS].set(router_e_w[i])
        rb = jnp.zeros((1, LANES), F32).at[0, :N_GROUPS].set(router_g_b[i]).at[0, N_GROUPS:N_GROUPS + N_EXPERTS].set(router_e_b[i])
        rw_hi = rw.astype(BF16)
        rw_lo = (rw - rw_hi.astype(F32)).astype(BF16)
        x1, x1b, x1p, ids, gates = _outproj(i, xf,ya.reshape(t, -1), yb.reshape(t, -1), yc.reshape(t, -1), yd.reshape(t, -1),
                                       vec(mix_scale), w_out_bf, vec(ln1_g), vec(ln1_b), rw_hi, rw_lo, rb)
        pos, tile_tab = _plan(ids)
        pos_flat = pos[:, :TOP_K].T.reshape(-1)
        tile_end = tile_tab[:n_tiles, 0]
        n_active = jnp.sum((tile_end < N_EXPERTS).astype(I32)).reshape(1)
        tile_expert = jnp.minimum(tile_end, N_EXPERTS - 1)
        xs = _dispatch(pos_flat, x1p, n_rows)
        ys = _experts(i, tile_expert, n_active, xs, exp_w_gate, exp_w_up, exp_w_down)
        xf, xb = _final(i, pos_flat, x1, x1b, gates, p3, ple_wg_bf, vec(ple_b_gate), ple_wp_bf, vec(ln2_g), vec(ln2_b), ys)
    return xf.reshape(bsz, seq, d)
```

```python
import functools
import math

import jax
import jax.numpy as jnp
from jax import lax
from jax.experimental import pallas as pl
from jax.experimental.pallas import tpu as pltpu

F32 = jnp.float32
BF16 = jnp.bfloat16
I32 = jnp.int32
HIGHEST = lax.Precision.HIGHEST

D_MODEL = 2048
DEPTH = 2
PLE_DIM = 256
W_CONF = 512
W_GLA = 512
W_DIL = 512
W_SC = 512
CONF_KERNEL = 31
GLA_HEADS = 4
GLA_DV = 128
GLA_DK = 64
GLA_RANK = 16
GLA_TAU = 16.0
DIL_HEADS = 8
DIL_HD = 64
DIL_BRANCHES = ((128, 1), (512, 4), (2048, 16))
DIL_BLOCK = 128
REL_BUCKETS = 32
REL_MAX_DIST = 2048
SC_KERNEL = 3
N_GROUPS = 4
EXPERTS_PER_GROUP = 8
N_EXPERTS = 32
TOP_K = 2
D_EXPERT = 256
DN_ALPHA = (2 * DEPTH) ** 0.25
LN_EPS = 1e-5
RMS_EPS = 1e-6
IN_SIZES = (2 * W_CONF, GLA_HEADS * GLA_DK, GLA_HEADS * GLA_DK, W_GLA, GLA_RANK, W_GLA,
            W_DIL, W_DIL, W_DIL, W_SC, W_SC, W_SC)

LANES = 128
SUBLANES = 8
VMEM_LIMIT = 56 * 1024 * 1024

GL_PAD = 512
COL_A = 0
COL_GQ = 1024
COL_GK = 1280
COL_GV = 1536
COL_GL = 2048
COL_GR = 2560
COL_CQ = 3072
COL_CK = 3584
COL_CV = 4096
COL_SB = 4608
COL_SC = 5120
COL_SH = 5632
U_COLS = 6144

MOE_TILE = 512


def _cparams(n_axes):
    return pltpu.CompilerParams(dimension_semantics=("arbitrary",) * n_axes, vmem_limit_bytes=VMEM_LIMIT)


def _layer_norm(x, g, b):
    mu = jnp.mean(x, axis=-1, keepdims=True)
    xc = x - mu
    var = jnp.mean(xc * xc, axis=-1, keepdims=True)
    return xc * lax.rsqrt(var + LN_EPS) * g + b


U32 = jnp.uint32


def _pack_bf16_pairs(x):
    n = x.shape[1] // 2
    bits = lax.bitcast_convert_type(x.astype(BF16).astype(F32), U32)
    return (bits[:, :n] >> 16) | (bits[:, n:] & jnp.uint32(0xFFFF0000))


def _unpack_bf16_pairs(w):
    lo = lax.bitcast_convert_type(w << 16, F32)
    hi = lax.bitcast_convert_type(w & jnp.uint32(0xFFFF0000), F32)
    return jnp.concatenate([lo, hi], axis=1)


def _store_row_tiles(ref, packed):
    rows, width = packed.shape
    assert width == SUBLANES * LANES
    for s in range(SUBLANES):
        ref[pl.ds(s, rows, stride=SUBLANES), :] = packed[:, s * LANES:(s + 1) * LANES]


def _load_row_tiles(ref, rows):
    return jnp.concatenate([ref[pl.ds(s, rows, stride=SUBLANES), :] for s in range(SUBLANES)], axis=1)


def _sigmoid(x):
    return 1.0 / (1.0 + jnp.exp(-x))


def _silu(x):
    return x * _sigmoid(x)


def _matmul_kernel(x_ref, w_ref, o_ref):
    o_ref[...] = jnp.dot(x_ref[...].astype(BF16), w_ref[0],
                         preferred_element_type=F32).astype(o_ref.dtype)


def _in_proj(x, w_all, layer, tm, tn, out_dtype):
    m, k = x.shape
    n = w_all.shape[2]
    return pl.pallas_call(
        _matmul_kernel,
        grid=(n // tn, m // tm),
        in_specs=[pl.BlockSpec((tm, k), lambda j, i: (i, 0)),
                  pl.BlockSpec((1, k, tn), lambda j, i: (layer, 0, j))],
        out_specs=pl.BlockSpec((tm, tn), lambda j, i: (i, j)),
        out_shape=jax.ShapeDtypeStruct((m, n), out_dtype),
        compiler_params=_cparams(2),
        name="in_proj",
    )(x, w_all)


CONV_TS = 512
CONV_RC = 64
CONF_HIST = 32
SC_HIST = 8


def _conv_kernel(a_ref, gate_ref, sb_ref, sc_ref, sh_ref, cw_ref, cb_ref, lg_ref, lb_ref, sw_ref,
                 ya_ref, yd_ref, ha, hd, hs):
    s = pl.program_id(1)
    ts = a_ref.shape[1]
    n_ext = CONF_HIST + ts

    @pl.when(s == 0)
    def _():
        ha[0:CONF_HIST, :] = jnp.zeros((CONF_HIST, W_CONF), F32)
        hd[0:SC_HIST, :] = jnp.zeros((SC_HIST, W_SC), F32)

    ha[CONF_HIST:CONF_HIST + ts, :] = a_ref[0] * _sigmoid(gate_ref[0])
    hd[SC_HIST:SC_HIST + ts, :] = sc_ref[0] * sh_ref[0]

    cb = cb_ref[...]
    lg = lg_ref[...]
    lb = lb_ref[...]
    off_a = CONF_HIST - (CONF_KERNEL - 1)
    h_all = ha[...]
    for sh in range(1, SUBLANES):
        hs[sh - 1] = pltpu.roll(h_all, n_ext - sh, 0)
    for c in range(ts // CONV_RC):
        r0 = c * CONV_RC
        acc = jnp.zeros((CONV_RC, W_CONF), F32)
        for j in range(CONF_KERNEL):
            sh = (off_a + j) % SUBLANES
            lo = r0 + off_a + j - sh
            rows = ha[lo:lo + CONV_RC, :] if sh == 0 else hs[sh - 1, lo:lo + CONV_RC, :]
            acc = acc + cw_ref[j:j + 1, :] * rows
        hh = _layer_norm(acc + cb, lg, lb)
        ya_ref[0, r0:r0 + CONV_RC, :] = _silu(hh)

    off_d = SC_HIST - (SC_KERNEL - 1)
    accd = jnp.zeros((ts, W_SC), F32)
    for j in range(SC_KERNEL):
        accd = accd + sw_ref[j:j + 1, :] * hd[off_d + j:off_d + j + ts, :]
    yd_ref[0] = sb_ref[0] * accd

    ha[0:CONF_HIST, :] = ha[ts:ts + CONF_HIST, :]
    hd[0:SC_HIST, :] = hd[ts:ts + SC_HIST, :]


def _conv_mixers(u3, conf_w, conf_b, conf_lg, conf_lb, sc_w):
    bsz, seq, _ = u3.shape
    ts = CONV_TS
    wblk = W_CONF

    def col(c):
        return pl.BlockSpec((1, ts, wblk), lambda b, s, c=c: (b, s, c // wblk))

    def full(shape):
        return pl.BlockSpec(shape, lambda b, s: (0,) * len(shape))

    cw = jnp.zeros((32, W_CONF), F32).at[:CONF_KERNEL].set(conf_w)
    sw = jnp.zeros((8, W_SC), F32).at[:SC_KERNEL].set(sc_w)
    out_spec = pl.BlockSpec((1, ts, wblk), lambda b, s: (b, s, 0))
    return pl.pallas_call(
        _conv_kernel,
        grid=(bsz, seq // ts),
        in_specs=[col(COL_A), col(COL_A + W_CONF), col(COL_SB), col(COL_SC), col(COL_SH),
                  full((32, W_CONF)), full((1, W_CONF)), full((1, W_CONF)), full((1, W_CONF)), full((8, W_SC))],
        out_specs=[out_spec, out_spec],
        out_shape=[jax.ShapeDtypeStruct((bsz, seq, W_CONF), F32), jax.ShapeDtypeStruct((bsz, seq, W_SC), F32)],
        scratch_shapes=[pltpu.VMEM((CONF_HIST + ts, W_CONF), F32), pltpu.VMEM((SC_HIST + ts, W_SC), F32),
                        pltpu.VMEM((SUBLANES - 1, CONF_HIST + ts, W_CONF), F32)],
        compiler_params=_cparams(2),
        name="conv_mixers",
    )(u3, u3, u3, u3, u3, cw, conf_b.reshape(1, -1), conf_lg.reshape(1, -1), conf_lb.reshape(1, -1), sw)


GLA_BLK = 128
GLA_SUB = 64
GLA_UNROLL = 4


def _split3(x):
    hi = x.astype(BF16)
    r1 = x - hi.astype(F32)
    mid = r1.astype(BF16)
    lo = (r1 - mid.astype(F32)).astype(BF16)
    return hi, mid, lo


def _gla_kernel(q_ref, k_ref, v_ref, gl_ref, r_ref, wg2h_ref, wg2l_ref, bg2_ref, ng_ref, tril_ref, o_ref, la_ref):
    seq = q_ref.shape[1]
    nblk = seq // GLA_BLK
    nsub = GLA_BLK // GLA_SUB
    assert nsub == 2 and nblk % GLA_UNROLL == 0

    lane = lax.broadcasted_iota(I32, (1, LANES), 1)
    head_mask = [(lane < GLA_DK).astype(F32), (lane >= GLA_DK).astype(F32)]
    row = lax.broadcasted_iota(I32, (GLA_BLK, GLA_BLK), 0)
    col = lax.broadcasted_iota(I32, (GLA_BLK, GLA_BLK), 1)
    sub_shift = GLA_SUB.bit_length() - 1
    diag_mask = ((row >> sub_shift) == (col >> sub_shift)) & (col <= row)
    second_half = lax.broadcasted_iota(I32, (GLA_BLK, 1), 0) >= GLA_SUB
    scale = GLA_DK ** -0.5
    ng = ng_ref[...]
    tril = tril_ref[...]

    gl = gl_ref[0]
    gl_hi = gl.astype(BF16)
    gl_lo = (gl - gl_hi.astype(F32)).astype(BF16)
    wh = wg2h_ref[...]
    z = (jnp.dot(gl_hi, wh, preferred_element_type=F32) + jnp.dot(gl_lo, wh, preferred_element_type=F32)
         + jnp.dot(gl_hi, wg2l_ref[...], preferred_element_type=F32) + bg2_ref[...])
    la_ref[...] = (jnp.minimum(z, 0.0) - jnp.log(1.0 + jnp.exp(-jnp.abs(z)))) * (1.0 / GLA_TAU)

    nt = (((1,), (1,)), ((), ()))

    def body(it, carry):
        grp = range(GLA_UNROLL)
        r0 = [pl.multiple_of((it * GLA_UNROLL + j) * GLA_BLK, GLA_BLK) for j in grp]
        q = [q_ref[0, pl.ds(r, GLA_BLK), :] * scale for r in r0]
        k = [k_ref[0, pl.ds(r, GLA_BLK), :] for r in r0]
        v = [v_ref[0, pl.ds(r, GLA_BLK), :].astype(BF16) for r in r0]
        b = []
        for r in r0:
            acc = None
            for part in _split3(la_ref[pl.ds(r, GLA_BLK), :]):
                t = jnp.dot(tril, part, preferred_element_type=F32)
                acc = t if acc is None else acc + t
            b.append(acc)

        s_in, q_st, kk, b_last = [], [], [], []
        for j in grp:
            b3 = b[j].reshape(nsub, GLA_SUB, LANES)
            mid = jnp.broadcast_to(b3[:, GLA_SUB // 2 - 1:GLA_SUB // 2, :], b3.shape).reshape(GLA_BLK, LANES)
            qd = q[j] * jnp.exp(b[j] - mid)
            kd = (k[j] * jnp.exp(mid - b[j])).astype(BF16)
            e0 = b[j][GLA_SUB - 1:GLA_SUB, :]
            q_off = jnp.where(second_half, q[j] * jnp.exp(jnp.minimum(b[j] - e0, 0.0)), 0.0)
            k_off = jnp.where(second_half, 0.0, k[j] * jnp.exp(jnp.minimum(e0 - b[j], 0.0))).astype(BF16)
            s_in.append((qd, kd, q_off, k_off))
            q_st.append(q[j] * jnp.exp(b[j]))
            b_last.append(b[j][GLA_BLK - 1:GLA_BLK, :])
            kk.append((k[j] * jnp.exp(b_last[j] - b[j])).astype(BF16))

        scores = []
        for j in grp:
            qd, kd, q_off, k_off = s_in[j]
            per_head = []
            for h in range(2):
                hm = head_mask[h]
                s_d = lax.dot_general((qd * hm).astype(BF16), kd, nt, preferred_element_type=F32)
                s_o = lax.dot_general((q_off * hm).astype(BF16), k_off, nt, preferred_element_type=F32)
                per_head.append((jnp.where(diag_mask, s_d, 0.0) + s_o).astype(BF16))
            scores.append(per_head)

        states = []
        st = carry
        for j in grp:
            states.append(st.astype(BF16))
            upd = lax.dot_general(kk[j], v[j], (((0,), (0,)), ((), ())), preferred_element_type=F32)
            decay = jnp.transpose(jnp.broadcast_to(jnp.exp(b_last[j]), (GLA_BLK, LANES)))
            st = st * jnp.concatenate([decay, decay], axis=1) + upd

        for j in grp:
            outs = []
            for h in range(2):
                lhs = jnp.concatenate([scores[j][h], (q_st[j] * head_mask[h]).astype(BF16)], axis=1)
                rhs = jnp.concatenate([v[j][:, h * GLA_DV:(h + 1) * GLA_DV],
                                       states[j][:, h * GLA_DV:(h + 1) * GLA_DV]], axis=0)
                o_h = jnp.dot(lhs, rhs, preferred_element_type=F32)
                outs.append(o_h * lax.rsqrt(jnp.mean(o_h * o_h, axis=-1, keepdims=True) + RMS_EPS))
            o = jnp.concatenate(outs, axis=1) * ng
            o_ref[0, pl.ds(r0[j], GLA_BLK), :] = o * _silu(r_ref[0, pl.ds(r0[j], GLA_BLK), :])
        return st

    lax.fori_loop(0, nblk // GLA_UNROLL, body, jnp.zeros((2 * GLA_DK, 2 * GLA_DV), F32))


def _gla(u3, w_g2, b_g2, norm_g):
    bsz, seq, _ = u3.shape
    npair = GLA_HEADS // 2
    kw = 2 * GLA_DK
    vw = 2 * GLA_DV

    def col(c, w):
        return pl.BlockSpec((1, seq, w), lambda b, p, c=c, w=w: (b, 0, c // w + p))

    wg2 = jnp.zeros((LANES, GLA_HEADS * GLA_DK), F32).at[:GLA_RANK].set(w_g2)
    wg2_hi = wg2.astype(BF16)
    wg2_lo = (wg2 - wg2_hi.astype(F32)).astype(BF16)
    ng = jnp.concatenate([norm_g, norm_g]).reshape(1, vw)
    tril = jnp.tril(jnp.ones((GLA_BLK, GLA_BLK), BF16))
    return pl.pallas_call(
        _gla_kernel,
        grid=(bsz, npair),
        in_specs=[col(COL_GQ, kw), col(COL_GK, kw), col(COL_GV, vw),
                  pl.BlockSpec((1, seq, LANES), lambda b, p: (b, 0, COL_GL // LANES)),
                  col(COL_GR, vw),
                  pl.BlockSpec((LANES, kw), lambda b, p: (0, p)),
                  pl.BlockSpec((LANES, kw), lambda b, p: (0, p)),
                  pl.BlockSpec((1, kw), lambda b, p: (0, p)),
                  pl.BlockSpec((1, vw), lambda b, p: (0, 0)),
                  pl.BlockSpec((GLA_BLK, GLA_BLK), lambda b, p: (0, 0))],
        out_specs=pl.BlockSpec((1, seq, vw), lambda b, p: (b, 0, p)),
        out_shape=jax.ShapeDtypeStruct((bsz, seq, W_GLA), F32),
        scratch_shapes=[pltpu.VMEM((seq, LANES), F32)],
        compiler_params=_cparams(2),
        name="gla",
    )(u3, u3, u3, u3, u3, wg2_hi, wg2_lo, b_g2.reshape(1, -1), ng, tril)


def _t5_bucket(dist):
    max_exact = REL_BUCKETS // 2
    large = max_exact + (jnp.log(jnp.maximum(dist, 1).astype(F32) / max_exact)
                         / math.log(REL_MAX_DIST / max_exact) * (REL_BUCKETS - max_exact)).astype(I32)
    large = jnp.minimum(large, REL_BUCKETS - 1)
    return jnp.where(dist < max_exact, dist, large)


LOG2E = math.log2(math.e)
DIL_GROUP = 8


def _bias_kernel(rel_ref, bucket_ref, o_ref):
    h = pl.program_id(1)
    blk = DIL_BLOCK
    bucket = bucket_ref[0]
    acc = jnp.zeros(bucket.shape, F32)
    for b in range(REL_BUCKETS):
        acc = jnp.where(bucket == b, rel_ref[b, h], acc)
    acc = acc * LOG2E
    qi = lax.broadcasted_iota(I32, (blk, 2 * blk), 0)
    kj = lax.broadcasted_iota(I32, (blk, 2 * blk), 1)
    steps = qi + blk - kj
    in_window = (steps >= 0) & (steps <= blk)
    neg = jnp.float32(-jnp.inf)
    o_ref[0, 0, 0] = jnp.where(in_window, acc, neg)
    o_ref[0, 0, 1] = jnp.where(in_window & (kj >= blk), acc, neg)


def _bias_tables(rel_bias):
    assert all(window // dil == DIL_BLOCK for window, dil in DIL_BRANCHES)
    qi = jnp.arange(DIL_BLOCK)[:, None]
    kj = jnp.arange(2 * DIL_BLOCK)[None, :]
    steps = jnp.maximum(qi + DIL_BLOCK - kj, 0)
    buckets = jnp.stack([_t5_bucket(steps * dil) for _, dil in DIL_BRANCHES]).astype(I32)
    nbr = len(DIL_BRANCHES)
    return pl.pallas_call(
        _bias_kernel,
        grid=(nbr, DIL_HEADS),
        in_specs=[pl.BlockSpec(memory_space=pltpu.SMEM),
                  pl.BlockSpec((1, DIL_BLOCK, 2 * DIL_BLOCK), lambda i, h: (i, 0, 0))],
        out_specs=pl.BlockSpec((1, 1, 2, DIL_BLOCK, 2 * DIL_BLOCK), lambda i, h: (i, h, 0, 0, 0)),
        out_shape=jax.ShapeDtypeStruct((nbr, DIL_HEADS, 2, DIL_BLOCK, 2 * DIL_BLOCK), F32),
        compiler_params=_cparams(2),
        name="rel_bias_tables",
    )(rel_bias, buckets)


def _dil_kernel(q_ref, k_ref, v_ref, bias_ref, o_ref, qs, ks, vs, ob, lb):
    seq = q_ref.shape[1]
    blk = DIL_BLOCK
    scale = DIL_HD ** -0.5 * LOG2E
    lane = lax.broadcasted_iota(I32, (1, LANES), 1)
    first = lane < DIL_HD
    head_mask = [first.astype(F32), 1.0 - first.astype(F32)]

    ks[0:blk, :] = jnp.zeros((blk, LANES), BF16)
    vs[0:blk, :] = jnp.zeros((blk, LANES), BF16)

    for bi, (window, dil) in enumerate(DIL_BRANCHES):
        sub_len = seq // dil
        nb = sub_len // blk
        nb_shift = nb.bit_length() - 1
        for r in range(dil):
            rows = pl.ds(r, sub_len, stride=dil) if dil > 1 else pl.ds(0, sub_len)
            q = q_ref[0, rows, :] * scale
            for h in range(2):
                qs[h, r * sub_len:(r + 1) * sub_len, :] = (q * head_mask[h]).astype(BF16)
            ks[blk + r * sub_len:blk + (r + 1) * sub_len, :] = k_ref[0, rows, :].astype(BF16)
            vs[blk + r * sub_len:blk + (r + 1) * sub_len, :] = v_ref[0, rows, :].astype(BF16)

        def body(it, carry, bi=bi, dil=dil, nb=nb, nb_shift=nb_shift):
            grp = range(DIL_GROUP)
            g = [it * DIL_GROUP + gg for gg in grp]
            n = [gi & (nb - 1) for gi in g]
            g0 = [pl.multiple_of(gi * blk, blk) for gi in g]
            kb = [ks[pl.ds(s0, 2 * blk), :] for s0 in g0]
            sc = [[lax.dot_general(qs[h, pl.ds(g0[gg], blk), :], kb[gg], (((1,), (1,)), ((), ())),
                                   preferred_element_type=F32) + bias_ref[bi, h, jnp.where(n[gg] == 0, 1, 0)]
                   for h in range(2)] for gg in grp]
            stats = []
            for gg in grp:
                per_head = []
                for h in range(2):
                    m = jnp.max(sc[gg][h], axis=-1, keepdims=True)
                    p = jnp.exp2(sc[gg][h] - m)
                    l = jnp.sum(p, axis=-1, keepdims=True)
                    per_head.append((p.astype(BF16), m, l))
                stats.append(per_head)
            for gg in grp:
                vb = vs[pl.ds(g0[gg], 2 * blk), :]
                o_acc = None
                l_acc = None
                for h in range(2):
                    p, m, l = stats[gg][h]
                    o_h = jnp.dot(p, vb, preferred_element_type=F32) * (1.0 / l)
                    lse = jnp.broadcast_to(m + jnp.log2(l), (blk, LANES))
                    if h == 0:
                        o_acc, l_acc = o_h, lse
                    else:
                        o_acc = jnp.where(first, o_acc, o_h)
                        l_acc = jnp.where(first, l_acc, lse)
                if dil > 1:
                    dst = pl.ds(n[gg] * (blk * dil) + (g[gg] >> nb_shift), blk, stride=dil)
                else:
                    dst = pl.ds(g0[gg], blk)
                ob[bi, dst, :] = o_acc
                lb[bi, dst, :] = l_acc
            return carry

        lax.fori_loop(0, seq // blk // DIL_GROUP, body, 0)

    nbr = len(DIL_BRANCHES)
    mc = 256
    for c in range(seq // mc):
        rs = slice(c * mc, (c + 1) * mc)
        ls = [lb[i, rs, :] for i in range(nbr)]
        m = functools.reduce(jnp.maximum, ls)
        ws = [jnp.exp2(l - m) for l in ls]
        num = functools.reduce(lambda a, b: a + b, [w * ob[i, rs, :] for i, w in enumerate(ws)])
        den = functools.reduce(lambda a, b: a + b, ws)
        o_ref[0, rs, :] = num / den


def _dilated(u3, bias_tab):
    bsz, seq, _ = u3.shape
    npair = DIL_HEADS // 2
    nbr = len(DIL_BRANCHES)

    def col(c):
        return pl.BlockSpec((1, seq, LANES), lambda b, p, c=c: (b, 0, c // LANES + p))

    return pl.pallas_call(
        _dil_kernel,
        grid=(bsz, npair),
        in_specs=[col(COL_CQ), col(COL_CK), col(COL_CV),
                  pl.BlockSpec((nbr, 2, 2, DIL_BLOCK, 2 * DIL_BLOCK), lambda b, p: (0, p, 0, 0, 0))],
        out_specs=pl.BlockSpec((1, seq, LANES), lambda b, p: (b, 0, p)),
        out_shape=jax.ShapeDtypeStruct((bsz, seq, W_DIL), F32),
        scratch_shapes=[pltpu.VMEM((2, seq, LANES), BF16),
                        pltpu.VMEM((DIL_BLOCK + seq, LANES), BF16),
                        pltpu.VMEM((DIL_BLOCK + seq, LANES), BF16),
                        pltpu.VMEM((nbr, seq, LANES), F32),
                        pltpu.VMEM((nbr, seq, LANES), F32)],
        compiler_params=_cparams(2),
        name="dilated_attn",
    )(u3, u3, u3, bias_tab)


TOK_TILE = 256
OUT_TILE = 512
PLAN_TILE = 1024


def _outproj_kernel(x_ref, ya_ref, yb_ref, yc_ref, yd_ref, ms_ref, wo_ref, g_ref, b_ref, rwh_ref, rwl_ref, rb_ref,
                    x1_ref, x1b_ref, x1p_ref, ids_ref, gates_ref):
    acc = None
    for gi, y_ref in enumerate((ya_ref, yb_ref, yc_ref, yd_ref)):
        lo = gi * W_CONF
        y = (y_ref[...] * ms_ref[0, :, lo:lo + W_CONF]).astype(BF16)
        part = jnp.dot(y, wo_ref[0, lo:lo + W_CONF, :], preferred_element_type=F32)
        acc = part if acc is None else acc + part
    x1 = _layer_norm(DN_ALPHA * x_ref[...] + acc, g_ref[0], b_ref[0])
    x1_ref[...] = x1
    x1_hi = x1.astype(BF16)
    x1b_ref[...] = x1_hi
    _store_row_tiles(x1p_ref, _pack_bf16_pairs(x1))

    x1_lo = (x1 - x1_hi.astype(F32)).astype(BF16)
    rwh = rwh_ref[...]
    logits = (jnp.dot(x1_hi, rwh, preferred_element_type=F32) + jnp.dot(x1_lo, rwh, preferred_element_type=F32)
              + jnp.dot(x1_hi, rwl_ref[...], preferred_element_type=F32) + rb_ref[...])
    tm = logits.shape[0]
    lane = lax.broadcasted_iota(I32, (tm, LANES), 1)
    neg = jnp.float32(-jnp.inf)
    gl = jnp.where(lane < N_GROUPS, logits, neg)
    gmax = jnp.max(gl, axis=-1, keepdims=True)
    g_top = 1.0 / jnp.sum(jnp.exp(gl - gmax), axis=-1, keepdims=True)
    g_idx = jnp.min(jnp.where(gl == gmax, lane, LANES), axis=-1, keepdims=True)
    e_lo = N_GROUPS + g_idx * EXPERTS_PER_GROUP
    el = jnp.where((lane >= e_lo) & (lane < e_lo + EXPERTS_PER_GROUP), logits, neg)
    m1 = jnp.max(el, axis=-1, keepdims=True)
    i1 = jnp.min(jnp.where(el == m1, lane, LANES), axis=-1, keepdims=True)
    el2 = jnp.where(lane == i1, neg, el)
    m2 = jnp.max(el2, axis=-1, keepdims=True)
    i2 = jnp.min(jnp.where(el2 == m2, lane, LANES), axis=-1, keepdims=True)
    e21 = jnp.exp(m2 - m1)
    w1 = g_top / (1.0 + e21)
    w2 = g_top * e21 / (1.0 + e21)
    ids_ref[...] = jnp.where(lane == 0, i1 - N_GROUPS, jnp.where(lane == 1, i2 - N_GROUPS, N_EXPERTS))
    gates_ref[...] = jnp.where(lane == 0, w1, jnp.where(lane == 1, w2, 0.0))


def _outproj(layer, x, ya, yb, yc, yd, mix_scale, w_out_bf, ln_g, ln_b, rw_hi, rw_lo, rb):
    t, d = x.shape
    tm = OUT_TILE

    def row(w):
        return pl.BlockSpec((tm, w), lambda i: (i, 0))

    def full(shape):
        return pl.BlockSpec(shape, lambda i: (0,) * len(shape))

    def per_layer(shape):
        return pl.BlockSpec((1,) + shape, lambda i: (layer,) + (0,) * len(shape))

    return pl.pallas_call(
        _outproj_kernel,
        grid=(t // tm,),
        in_specs=[row(d), row(W_CONF), row(W_GLA), row(W_DIL), row(W_SC), per_layer((1, d)), per_layer((d, d)),
                  per_layer((1, d)), per_layer((1, d)), full((d, LANES)), full((d, LANES)), full((1, LANES))],
        out_specs=[row(d), row(d), pl.BlockSpec((tm * SUBLANES, LANES), lambda i: (i, 0)), row(LANES), row(LANES)],
        out_shape=[jax.ShapeDtypeStruct((t, d), F32), jax.ShapeDtypeStruct((t, d), BF16),
                   jax.ShapeDtypeStruct((t * SUBLANES, LANES), U32),
                   jax.ShapeDtypeStruct((t, LANES), I32), jax.ShapeDtypeStruct((t, LANES), F32)],
        compiler_params=_cparams(1),
        name="out_proj_ln_router",
    )(x, ya, yb, yc, yd, mix_scale, w_out_bf, ln_g, ln_b, rw_hi, rw_lo, rb)


def _plan_kernel(ids_ref, tril_ref, upper_ref, pos_ref, tile_ref, cnt, carry, start):
    phase = pl.program_id(0)
    i = pl.program_id(1)
    tm = ids_ref.shape[0]
    lane = lax.broadcasted_iota(I32, (tm, LANES), 1)
    ids = ids_ref[...]
    e1 = ids[:, 0:1]
    e2 = ids[:, 1:2]
    oh1 = (lane == e1).astype(F32)
    oh2 = (lane == e2).astype(F32)
    oh = oh1 + oh2

    @pl.when((phase == 0) & (i == 0))
    def _():
        cnt[...] = jnp.zeros(cnt.shape, F32)

    @pl.when(phase == 0)
    def _():
        cnt[...] = cnt[...] + jnp.sum(oh, axis=0, keepdims=True)

    @pl.when((phase == 1) & (i == 0))
    def _():
        padded = jnp.floor((cnt[...] + (MOE_TILE - 1)) * (1.0 / MOE_TILE)) * MOE_TILE
        p8 = jnp.broadcast_to(padded, (SUBLANES, LANES))
        st = jnp.dot(p8, upper_ref[...], precision=HIGHEST, preferred_element_type=F32)
        start[...] = st[0:1, :]
        carry[...] = jnp.zeros(carry.shape, F32)
        end = st[0:1, :] + padded
        trow = lax.broadcasted_iota(I32, (LANES, LANES), 0).astype(F32) * MOE_TILE
        tlane = lax.broadcasted_iota(I32, (LANES, LANES), 1)
        done = jnp.where((end <= trow) & (tlane < N_EXPERTS), 1.0, 0.0)
        tile_ref[...] = jnp.broadcast_to(jnp.sum(done, axis=-1, keepdims=True), (LANES, LANES)).astype(I32)

    @pl.when(phase == 1)
    def _():
        incl = jnp.dot(tril_ref[...], oh.astype(BF16), preferred_element_type=F32) + carry[...]
        base = start[...] + incl - oh
        p1 = jnp.sum(oh1 * base, axis=-1, keepdims=True)
        p2 = jnp.sum(oh2 * base, axis=-1, keepdims=True)
        pos_ref[...] = jnp.where(lane == 0, p1, jnp.where(lane == 1, p2, 0.0)).astype(I32)
        carry[...] = carry[...] + jnp.sum(oh, axis=0, keepdims=True)


def _plan(ids):
    t = ids.shape[0]
    tm = PLAN_TILE
    tril = jnp.tril(jnp.ones((tm, tm), BF16))
    upper = jnp.triu(jnp.ones((LANES, LANES), F32), k=1)
    return pl.pallas_call(
        _plan_kernel,
        grid=(2, t // tm),
        in_specs=[pl.BlockSpec((tm, LANES), lambda ph, i: (i, 0)),
                  pl.BlockSpec((tm, tm), lambda ph, i: (0, 0)),
                  pl.BlockSpec((LANES, LANES), lambda ph, i: (0, 0))],
        out_specs=[pl.BlockSpec((tm, LANES), lambda ph, i: (i * ph, 0)),
                   pl.BlockSpec((LANES, LANES), lambda ph, i: (0, 0))],
        out_shape=[jax.ShapeDtypeStruct((t, LANES), I32), jax.ShapeDtypeStruct((LANES, LANES), I32)],
        scratch_shapes=[pltpu.VMEM((1, LANES), F32), pltpu.VMEM((1, LANES), F32), pltpu.VMEM((1, LANES), F32)],
        compiler_params=_cparams(2),
        name="route_plan",
    )(ids, tril, upper)


def _dispatch_kernel(pos_ref, x_ref, xs_ref, sem):
    i = pl.program_id(0)
    tm = x_ref.shape[0] // SUBLANES
    t = pl.num_programs(0) * tm

    for m in range(tm):
        for k in range(TOP_K):
            dst = pl.multiple_of(pos_ref[k * t + i * tm + m] * SUBLANES, SUBLANES)
            pltpu.make_async_copy(x_ref.at[pl.ds(m * SUBLANES, SUBLANES)], xs_ref.at[pl.ds(dst, SUBLANES)],
                                  sem).start(priority=k)
    for _ in range(TOP_K):
        pltpu.make_async_copy(x_ref, xs_ref.at[pl.ds(0, tm * SUBLANES)], sem).wait()


def _dispatch(pos_flat, x1p, n_rows):
    t = x1p.shape[0] // SUBLANES
    tm = TOK_TILE
    return pl.pallas_call(
        _dispatch_kernel,
        grid_spec=pltpu.PrefetchScalarGridSpec(
            num_scalar_prefetch=1,
            grid=(t // tm,),
            in_specs=[pl.BlockSpec((tm * SUBLANES, LANES), lambda i, pos: (i, 0))],
            out_specs=pl.BlockSpec(memory_space=pl.ANY),
            scratch_shapes=[pltpu.SemaphoreType.DMA],
        ),
        out_shape=jax.ShapeDtypeStruct((n_rows * SUBLANES, LANES), x1p.dtype),
        compiler_params=_cparams(1),
        name="moe_dispatch",
    )(pos_flat, x1p)


def _expert_kernel(te_ref, na_ref, xs_ref, wg_ref, wu_ref, wd_ref, ys_ref):
    i = pl.program_id(0)

    @pl.when(i < na_ref[0])
    def _():
        xb = _unpack_bf16_pairs(_load_row_tiles(xs_ref, MOE_TILE)).astype(BF16)
        g = jnp.dot(xb, wg_ref[0, 0].astype(BF16), preferred_element_type=F32)
        u = jnp.dot(xb, wu_ref[0, 0].astype(BF16), preferred_element_type=F32)
        h = (_silu(g) * u).astype(BF16)
        y = jnp.dot(h, wd_ref[0, 0].astype(BF16), preferred_element_type=F32)
        _store_row_tiles(ys_ref, _pack_bf16_pairs(y))


def _experts(layer, tile_expert, n_active, xs, w_gate, w_up, w_down):
    n_rows = xs.shape[0] // SUBLANES
    d = LANES
    n_tiles = n_rows // MOE_TILE
    d_model = w_gate.shape[2]

    def rows(i, te, na):
        return (jnp.minimum(i, na[0] - 1), 0)

    def expert(i, te, na):
        return (layer, te[i], 0, 0)

    return pl.pallas_call(
        _expert_kernel,
        grid_spec=pltpu.PrefetchScalarGridSpec(
            num_scalar_prefetch=2,
            grid=(n_tiles,),
            in_specs=[pl.BlockSpec((MOE_TILE * SUBLANES, d), rows),
                      pl.BlockSpec((1, 1, d_model, D_EXPERT), expert),
                      pl.BlockSpec((1, 1, d_model, D_EXPERT), expert),
                      pl.BlockSpec((1, 1, D_EXPERT, d_model), expert)],
            out_specs=pl.BlockSpec((MOE_TILE * SUBLANES, d), rows),
        ),
        out_shape=jax.ShapeDtypeStruct((n_rows * SUBLANES, d), U32),
        compiler_params=_cparams(1),
        name="moe_experts",
    )(tile_expert, n_active, xs, w_gate, w_up, w_down)


def _final_kernel(pos_ref, x1_ref, x1b_ref, gates_ref, p_ref, wg_ref, bg_ref, wp_ref, g_ref, b_ref, ys_ref,
                  x2_ref, x2b_ref, ybuf, sem):
    i = pl.program_id(0)
    n_steps = pl.num_programs(0)
    tm = x1_ref.shape[0]
    t = n_steps * tm

    def gather(step, slot):
        for m in range(tm):
            for k in range(TOP_K):
                src = pl.multiple_of(pos_ref[k * t + step * tm + m] * SUBLANES, SUBLANES)
                pltpu.make_async_copy(ys_ref.at[pl.ds(src, SUBLANES)],
                                      ybuf.at[slot, k, pl.ds(m * SUBLANES, SUBLANES)],
                                      sem.at[slot]).start(priority=k)

    def wait_slot(slot):
        for k in range(TOP_K):
            pltpu.make_async_copy(ys_ref.at[pl.ds(0, tm * SUBLANES)], ybuf.at[slot, k], sem.at[slot]).wait()

    @pl.when(i == 0)
    def _():
        gather(0, 0)

    slot = i % 2
    gather(jnp.minimum(i + 1, n_steps - 1), 1 - slot)
    gate = _sigmoid(jnp.dot(x1b_ref[...], wg_ref[0], preferred_element_type=F32) + bg_ref[0])
    ple = gate * jnp.dot(p_ref[0].astype(BF16), wp_ref[0], preferred_element_type=F32)
    wait_slot(slot)
    gates = gates_ref[...]
    ffn = (gates[:, 0:1] * _unpack_bf16_pairs(_load_row_tiles(ybuf.at[slot, 0], tm))
           + gates[:, 1:2] * _unpack_bf16_pairs(_load_row_tiles(ybuf.at[slot, 1], tm)))

    @pl.when(i == n_steps - 1)
    def _():
        wait_slot(1 - slot)

    x2 = _layer_norm(DN_ALPHA * x1_ref[...] + ffn + ple, g_ref[0], b_ref[0])
    x2_ref[...] = x2
    x2b_ref[...] = x2.astype(BF16)


def _final(layer, pos_flat, x1, x1b, gates, p, wg_bf, bg, wp_bf, ln_g, ln_b, ys):
    t, d = x1.shape
    tm = TOK_TILE

    def row(w):
        return pl.BlockSpec((tm, w), lambda i, pos: (i, 0))

    def per_layer(shape):
        return pl.BlockSpec((1,) + shape, lambda i, pos: (layer,) + (0,) * len(shape))

    return pl.pallas_call(
        _final_kernel,
        grid_spec=pltpu.PrefetchScalarGridSpec(
            num_scalar_prefetch=1,
            grid=(t // tm,),
            in_specs=[row(d), row(d), row(LANES), pl.BlockSpec((1, tm, PLE_DIM), lambda i, pos: (layer, i, 0)),
                      per_layer((d, d)), per_layer((1, d)), per_layer((PLE_DIM, d)),
                      per_layer((1, d)), per_layer((1, d)), pl.BlockSpec(memory_space=pl.ANY)],
            out_specs=[row(d), row(d)],
            scratch_shapes=[pltpu.VMEM((2, TOP_K, tm * SUBLANES, LANES), U32), pltpu.SemaphoreType.DMA((2,))],
        ),
        out_shape=[jax.ShapeDtypeStruct((t, d), F32), jax.ShapeDtypeStruct((t, d), BF16)],
        compiler_params=_cparams(1),
        name="combine_ple_ln",
    )(pos_flat, x1, x1b, gates, p, wg_bf, bg, wp_bf, ln_g, ln_b, ys)


W_PREP_COLS = 512


def _arrange_kernel(wt_ref, o_ref):
    j = pl.program_id(1)
    wt = wt_ref[0]
    row = lax.broadcasted_iota(I32, (wt.shape[0], 1), 0)
    n_valid = jnp.where(j == COL_GL // W_PREP_COLS, GLA_RANK, W_PREP_COLS)
    o_ref[0] = jnp.where(row < n_valid, wt, 0.0).T.astype(BF16)


def _arrange_w_in(w):
    n_layer, k, n_in = w.shape
    assert sum(IN_SIZES[:5]) == COL_GL + GLA_RANK and n_in + GL_PAD - GLA_RANK == U_COLS
    assert COL_GL % W_PREP_COLS == 0 and GL_PAD == W_PREP_COLS
    gl_blk = COL_GL // W_PREP_COLS

    def src_row(l, j):
        row = jnp.where(j <= gl_blk, j * W_PREP_COLS, j * W_PREP_COLS - (GL_PAD - GLA_RANK))
        return (l, pl.multiple_of(row, GLA_RANK), 0)

    return pl.pallas_call(
        _arrange_kernel,
        grid=(n_layer, U_COLS // W_PREP_COLS),
        in_specs=[pl.BlockSpec((pl.Element(1), pl.Element(W_PREP_COLS), pl.Element(k)), src_row)],
        out_specs=pl.BlockSpec((1, k, W_PREP_COLS), lambda l, j: (l, 0, j)),
        out_shape=jax.ShapeDtypeStruct((n_layer, k, U_COLS), BF16),
        compiler_params=_cparams(2),
        name="arrange_w_in",
    )(jnp.swapaxes(w, 1, 2))


def kernel(x, p, w_in, conf_dw_w, conf_dw_b, conf_ln_g, conf_ln_b, gla_w_g2, gla_b_g2, gla_norm_g, sc_conv_w, mix_scale, w_out, rel_bias, ln1_g, ln1_b, router_g_w, router_g_b, router_e_w, router_e_b, exp_w_gate, exp_w_up, exp_w_down, ple_w_gate, ple_b_gate, ple_w_proj, ln2_g, ln2_b):
    bsz, seq, d = x.shape
    t = bsz * seq
    n_rows = TOP_K * t + N_EXPERTS * MOE_TILE
    n_tiles = n_rows // MOE_TILE
    bias_tab = _bias_tables(rel_bias)
    w_in_bf = _arrange_w_in(w_in)
    w_out_bf = w_out.astype(BF16)
    ple_wg_bf = ple_w_gate.astype(BF16)
    ple_wp_bf = ple_w_proj.astype(BF16)
    p3 = p.reshape(DEPTH, t, PLE_DIM)

    def vec(a):
        return a.reshape(DEPTH, 1, -1)

    xf = x.reshape(t, d)
    xb = xf
    for i in range(DEPTH):
        u = _in_proj(xb, w_in_bf, i, 512, 2048, F32)
        u3 = u.reshape(bsz, seq, U_COLS)
        ya, yd = _conv_mixers(u3, conf_dw_w[i], conf_dw_b[i], conf_ln_g[i], conf_ln_b[i], sc_conv_w[i])
        yb = _gla(u3, gla_w_g2[i], gla_b_g2[i], gla_norm_g[i])
        yc = _dilated(u3, bias_tab)
        rw = jnp.zeros((d, LANES), F32).at[:, :N_GROUPS].set(router_g_w[i]).at[:, N_GROUPS:N_GROUPS + N_EXPERTS].set(router_e_w[i])
        rb = jnp.zeros((1, LANES), F32).at[0, :N_GROUPS].set(router_g_b[i]).at[0, N_GROUPS:N_GROUPS + N_EXPERTS].set(router_e_b[i])
        rw_hi = rw.astype(BF16)
        rw_lo = (rw - rw_hi.astype(F32)).astype(BF16)
        x1, x1b, x1p, ids, gates = _outproj(i, xf,ya.reshape(t, -1), yb.reshape(t, -1), yc.reshape(t, -1), yd.reshape(t, -1),
                                       vec(mix_scale), w_out_bf, vec(ln1_g), vec(ln1_b), rw_hi, rw_lo, rb)
        pos, tile_tab = _plan(ids)
        pos_flat = pos[:, :TOP_K].T.reshape(-1)
        tile_end = tile_tab[:n_tiles, 0]
        n_active = jnp.sum((tile_end < N_EXPERTS).astype(I32)).reshape(1)
        tile_expert = jnp.minimum(tile_end, N_EXPERTS - 1)
        xs = _dispatch(pos_flat, x1p, n_rows)
        ys = _experts(i, tile_expert, n_active, xs, exp_w_gate, exp_w_up, exp_w_down)
        xf, xb = _final(i, pos_flat, x1, x1b, gates, p3, ple_wg_bf, vec(ple_b_gate), ple_wp_bf, vec(ln2_g), vec(ln2_b), ys)
    return xf.reshape(bsz, seq, d)
```

```python
import functools
import math

import jax
import jax.numpy as jnp
from jax import lax
from jax.experimental import pallas as pl
from jax.experimental.pallas import tpu as pltpu

F32 = jnp.float32
BF16 = jnp.bfloat16
I32 = jnp.int32
HIGHEST = lax.Precision.HIGHEST

D_MODEL = 2048
DEPTH = 2
PLE_DIM = 256
W_CONF = 512
W_GLA = 512
W_DIL = 512
W_SC = 512
CONF_KERNEL = 31
GLA_HEADS = 4
GLA_DV = 128
GLA_DK = 64
GLA_RANK = 16
GLA_TAU = 16.0
DIL_HEADS = 8
DIL_HD = 64
DIL_BRANCHES = ((128, 1), (512, 4), (2048, 16))
DIL_BLOCK = 128
REL_BUCKETS = 32
REL_MAX_DIST = 2048
SC_KERNEL = 3
N_GROUPS = 4
EXPERTS_PER_GROUP = 8
N_EXPERTS = 32
TOP_K = 2
D_EXPERT = 256
DN_ALPHA = (2 * DEPTH) ** 0.25
LN_EPS = 1e-5
RMS_EPS = 1e-6
IN_SIZES = (2 * W_CONF, GLA_HEADS * GLA_DK, GLA_HEADS * GLA_DK, W_GLA, GLA_RANK, W_GLA,
            W_DIL, W_DIL, W_DIL, W_SC, W_SC, W_SC)

LANES = 128
SUBLANES = 8
VMEM_LIMIT = 56 * 1024 * 1024

GL_PAD = 512
COL_A = 0
COL_GQ = 1024
COL_GK = 1280
COL_GV = 1536
COL_GL = 2048
COL_GR = 2560
COL_CQ = 3072
COL_CK = 3584
COL_CV = 4096
COL_SB = 4608
COL_SC = 5120
COL_SH = 5632
U_COLS = 6144

MOE_TILE = 512


def _cparams(n_axes):
    return pltpu.CompilerParams(dimension_semantics=("arbitrary",) * n_axes, vmem_limit_bytes=VMEM_LIMIT)


def _layer_norm(x, g, b):
    mu = jnp.mean(x, axis=-1, keepdims=True)
    xc = x - mu
    var = jnp.mean(xc * xc, axis=-1, keepdims=True)
    return xc * lax.rsqrt(var + LN_EPS) * g + b


U32 = jnp.uint32


def _pack_bf16_pairs(x):
    n = x.shape[1] // 2
    bits = lax.bitcast_convert_type(x.astype(BF16).astype(F32), U32)
    return (bits[:, :n] >> 16) | (bits[:, n:] & jnp.uint32(0xFFFF0000))


def _unpack_bf16_pairs(w):
    lo = lax.bitcast_convert_type(w << 16, F32)
    hi = lax.bitcast_convert_type(w & jnp.uint32(0xFFFF0000), F32)
    return jnp.concatenate([lo, hi], axis=1)


def _store_row_tiles(ref, packed):
    rows, width = packed.shape
    assert width == SUBLANES * LANES
    for s in range(SUBLANES):
        ref[pl.ds(s, rows, stride=SUBLANES), :] = packed[:, s * LANES:(s + 1) * LANES]


def _load_row_tiles(ref, rows):
    return jnp.concatenate([ref[pl.ds(s, rows, stride=SUBLANES), :] for s in range(SUBLANES)], axis=1)


def _sigmoid(x):
    return 1.0 / (1.0 + jnp.exp(-x))


def _silu(x):
    return x * _sigmoid(x)


def _matmul_kernel(x_ref, w_ref, o_ref):
    o_ref[...] = jnp.dot(x_ref[...].astype(BF16), w_ref[0],
                         preferred_element_type=F32).astype(o_ref.dtype)


def _in_proj(x, w_all, layer, tm, tn, out_dtype):
    m, k = x.shape
    n = w_all.shape[2]
    return pl.pallas_call(
        _matmul_kernel,
        grid=(n // tn, m // tm),
        in_specs=[pl.BlockSpec((tm, k), lambda j, i: (i, 0)),
                  pl.BlockSpec((1, k, tn), lambda j, i: (layer, 0, j))],
        out_specs=pl.BlockSpec((tm, tn), lambda j, i: (i, j)),
        out_shape=jax.ShapeDtypeStruct((m, n), out_dtype),
        compiler_params=_cparams(2),
        name="in_proj",
    )(x, w_all)


CONV_TS = 512
CONV_RC = 64
CONF_HIST = 32
SC_HIST = 8


def _conv_kernel(a_ref, gate_ref, sb_ref, sc_ref, sh_ref, cw_ref, cb_ref, lg_ref, lb_ref, sw_ref,
                 ya_ref, yd_ref, ha, hd, hs):
    s = pl.program_id(1)
    ts = a_ref.shape[1]
    n_ext = CONF_HIST + ts

    @pl.when(s == 0)
    def _():
        ha[0:CONF_HIST, :] = jnp.zeros((CONF_HIST, W_CONF), F32)
        hd[0:SC_HIST, :] = jnp.zeros((SC_HIST, W_SC), F32)

    ha[CONF_HIST:CONF_HIST + ts, :] = a_ref[0] * _sigmoid(gate_ref[0])
    hd[SC_HIST:SC_HIST + ts, :] = sc_ref[0] * sh_ref[0]

    cb = cb_ref[...]
    lg = lg_ref[...]
    lb = lb_ref[...]
    off_a = CONF_HIST - (CONF_KERNEL - 1)
    h_all = ha[...]
    for sh in range(1, SUBLANES):
        h_all = pltpu.roll(h_all, n_ext - 1, 0)
        hs[sh - 1] = h_all
    for c in range(ts // CONV_RC):
        r0 = c * CONV_RC
        acc = jnp.zeros((CONV_RC, W_CONF), F32)
        for j in range(CONF_KERNEL):
            sh = (off_a + j) % SUBLANES
            lo = r0 + off_a + j - sh
            rows = ha[lo:lo + CONV_RC, :] if sh == 0 else hs[sh - 1, lo:lo + CONV_RC, :]
            acc = acc + cw_ref[j:j + 1, :] * rows
        hh = _layer_norm(acc + cb, lg, lb)
        ya_ref[0, r0:r0 + CONV_RC, :] = _silu(hh)

    off_d = SC_HIST - (SC_KERNEL - 1)
    accd = jnp.zeros((ts, W_SC), F32)
    for j in range(SC_KERNEL):
        accd = accd + sw_ref[j:j + 1, :] * hd[off_d + j:off_d + j + ts, :]
    yd_ref[0] = sb_ref[0] * accd

    ha[0:CONF_HIST, :] = ha[ts:ts + CONF_HIST, :]
    hd[0:SC_HIST, :] = hd[ts:ts + SC_HIST, :]


def _conv_mixers(u3, conf_w, conf_b, conf_lg, conf_lb, sc_w):
    bsz, seq, _ = u3.shape
    ts = CONV_TS
    wblk = W_CONF

    def col(c):
        return pl.BlockSpec((1, ts, wblk), lambda b, s, c=c: (b, s, c // wblk))

    def full(shape):
        return pl.BlockSpec(shape, lambda b, s: (0,) * len(shape))

    cw = jnp.zeros((32, W_CONF), F32).at[:CONF_KERNEL].set(conf_w)
    sw = jnp.zeros((8, W_SC), F32).at[:SC_KERNEL].set(sc_w)
    out_spec = pl.BlockSpec((1, ts, wblk), lambda b, s: (b, s, 0))
    return pl.pallas_call(
        _conv_kernel,
        grid=(bsz, seq // ts),
        in_specs=[col(COL_A), col(COL_A + W_CONF), col(COL_SB), col(COL_SC), col(COL_SH),
                  full((32, W_CONF)), full((1, W_CONF)), full((1, W_CONF)), full((1, W_CONF)), full((8, W_SC))],
        out_specs=[out_spec, out_spec],
        out_shape=[jax.ShapeDtypeStruct((bsz, seq, W_CONF), F32), jax.ShapeDtypeStruct((bsz, seq, W_SC), F32)],
        scratch_shapes=[pltpu.VMEM((CONF_HIST + ts, W_CONF), F32), pltpu.VMEM((SC_HIST + ts, W_SC), F32),
                        pltpu.VMEM((SUBLANES - 1, CONF_HIST + ts, W_CONF), F32)],
        compiler_params=_cparams(2),
        name="conv_mixers",
    )(u3, u3, u3, u3, u3, cw, conf_b.reshape(1, -1), conf_lg.reshape(1, -1), conf_lb.reshape(1, -1), sw)


GLA_BLK = 128
GLA_SUB = 64
GLA_UNROLL = 4


def _split3(x):
    hi = x.astype(BF16)
    r1 = x - hi.astype(F32)
    mid = r1.astype(BF16)
    lo = (r1 - mid.astype(F32)).astype(BF16)
    return hi, mid, lo


def _gla_kernel(q_ref, k_ref, v_ref, gl_ref, r_ref, wg2h_ref, wg2l_ref, bg2_ref, ng_ref, tril_ref, o_ref, la_ref):
    seq = q_ref.shape[1]
    nblk = seq // GLA_BLK
    nsub = GLA_BLK // GLA_SUB
    assert nsub == 2 and nblk % GLA_UNROLL == 0

    lane = lax.broadcasted_iota(I32, (1, LANES), 1)
    head_mask = [(lane < GLA_DK).astype(F32), (lane >= GLA_DK).astype(F32)]
    row = lax.broadcasted_iota(I32, (GLA_BLK, GLA_BLK), 0)
    col = lax.broadcasted_iota(I32, (GLA_BLK, GLA_BLK), 1)
    sub_shift = GLA_SUB.bit_length() - 1
    diag_mask = ((row >> sub_shift) == (col >> sub_shift)) & (col <= row)
    second_half = lax.broadcasted_iota(I32, (GLA_BLK, 1), 0) >= GLA_SUB
    scale = GLA_DK ** -0.5
    ng = ng_ref[...]
    tril = tril_ref[...]

    gl = gl_ref[0]
    gl_hi = gl.astype(BF16)
    gl_lo = (gl - gl_hi.astype(F32)).astype(BF16)
    wh = wg2h_ref[...]
    z = (jnp.dot(gl_hi, wh, preferred_element_type=F32) + jnp.dot(gl_lo, wh, preferred_element_type=F32)
         + jnp.dot(gl_hi, wg2l_ref[...], preferred_element_type=F32) + bg2_ref[...])
    la_ref[...] = (jnp.minimum(z, 0.0) - jnp.log(1.0 + jnp.exp(-jnp.abs(z)))) * (1.0 / GLA_TAU)

    nt = (((1,), (1,)), ((), ()))

    def body(it, carry):
        grp = range(GLA_UNROLL)
        r0 = [pl.multiple_of((it * GLA_UNROLL + j) * GLA_BLK, GLA_BLK) for j in grp]
        q = [q_ref[0, pl.ds(r, GLA_BLK), :] * scale for r in r0]
        k = [k_ref[0, pl.ds(r, GLA_BLK), :] for r in r0]
        v = [v_ref[0, pl.ds(r, GLA_BLK), :].astype(BF16) for r in r0]
        b = []
        for r in r0:
            acc = None
            for part in _split3(la_ref[pl.ds(r, GLA_BLK), :]):
                t = jnp.dot(tril, part, preferred_element_type=F32)
                acc = t if acc is None else acc + t
            b.append(acc)

        s_in, q_st, kk, b_last = [], [], [], []
        for j in grp:
            b3 = b[j].reshape(nsub, GLA_SUB, LANES)
            mid = jnp.broadcast_to(b3[:, GLA_SUB // 2 - 1:GLA_SUB // 2, :], b3.shape).reshape(GLA_BLK, LANES)
            qd = q[j] * jnp.exp(b[j] - mid)
            kd = (k[j] * jnp.exp(mid - b[j])).astype(BF16)
            e0 = b[j][GLA_SUB - 1:GLA_SUB, :]
            q_off = jnp.where(second_half, q[j] * jnp.exp(jnp.minimum(b[j] - e0, 0.0)), 0.0)
            k_off = jnp.where(second_half, 0.0, k[j] * jnp.exp(jnp.minimum(e0 - b[j], 0.0))).astype(BF16)
            s_in.append((qd, kd, q_off, k_off))
            q_st.append(q[j] * jnp.exp(b[j]))
            b_last.append(b[j][GLA_BLK - 1:GLA_BLK, :])
            kk.append((k[j] * jnp.exp(b_last[j] - b[j])).astype(BF16))

        scores = []
        for j in grp:
            qd, kd, q_off, k_off = s_in[j]
            per_head = []
            for h in range(2):
                hm = head_mask[h]
                s_d = lax.dot_general((qd * hm).astype(BF16), kd, nt, preferred_element_type=F32)
                s_o = lax.dot_general((q_off * hm).astype(BF16), k_off, nt, preferred_element_type=F32)
                per_head.append((jnp.where(diag_mask, s_d, 0.0) + s_o).astype(BF16))
            scores.append(per_head)

        states = []
        st = carry
        for j in grp:
            states.append(st.astype(BF16))
            upd = lax.dot_general(kk[j], v[j], (((0,), (0,)), ((), ())), preferred_element_type=F32)
            decay = jnp.transpose(jnp.broadcast_to(jnp.exp(b_last[j]), (GLA_BLK, LANES)))
            st = st * jnp.concatenate([decay, decay], axis=1) + upd

        for j in grp:
            outs = []
            for h in range(2):
                lhs = jnp.concatenate([scores[j][h], (q_st[j] * head_mask[h]).astype(BF16)], axis=1)
                rhs = jnp.concatenate([v[j][:, h * GLA_DV:(h + 1) * GLA_DV],
                                       states[j][:, h * GLA_DV:(h + 1) * GLA_DV]], axis=0)
                o_h = jnp.dot(lhs, rhs, preferred_element_type=F32)
                outs.append(o_h * lax.rsqrt(jnp.mean(o_h * o_h, axis=-1, keepdims=True) + RMS_EPS))
            o = jnp.concatenate(outs, axis=1) * ng
            o_ref[0, pl.ds(r0[j], GLA_BLK), :] = o * _silu(r_ref[0, pl.ds(r0[j], GLA_BLK), :])
        return st

    lax.fori_loop(0, nblk // GLA_UNROLL, body, jnp.zeros((2 * GLA_DK, 2 * GLA_DV), F32))


def _gla(u3, w_g2, b_g2, norm_g):
    bsz, seq, _ = u3.shape
    npair = GLA_HEADS // 2
    kw = 2 * GLA_DK
    vw = 2 * GLA_DV

    def col(c, w):
        return pl.BlockSpec((1, seq, w), lambda b, p, c=c, w=w: (b, 0, c // w + p))

    wg2 = jnp.zeros((LANES, GLA_HEADS * GLA_DK), F32).at[:GLA_RANK].set(w_g2)
    wg2_hi = wg2.astype(BF16)
    wg2_lo = (wg2 - wg2_hi.astype(F32)).astype(BF16)
    ng = jnp.concatenate([norm_g, norm_g]).reshape(1, vw)
    tril = jnp.tril(jnp.ones((GLA_BLK, GLA_BLK), BF16))
    return pl.pallas_call(
        _gla_kernel,
        grid=(bsz, npair),
        in_specs=[col(COL_GQ, kw), col(COL_GK, kw), col(COL_GV, vw),
                  pl.BlockSpec((1, seq, LANES), lambda b, p: (b, 0, COL_GL // LANES)),
                  col(COL_GR, vw),
                  pl.BlockSpec((LANES, kw), lambda b, p: (0, p)),
                  pl.BlockSpec((LANES, kw), lambda b, p: (0, p)),
                  pl.BlockSpec((1, kw), lambda b, p: (0, p)),
                  pl.BlockSpec((1, vw), lambda b, p: (0, 0)),
                  pl.BlockSpec((GLA_BLK, GLA_BLK), lambda b, p: (0, 0))],
        out_specs=pl.BlockSpec((1, seq, vw), lambda b, p: (b, 0, p)),
        out_shape=jax.ShapeDtypeStruct((bsz, seq, W_GLA), F32),
        scratch_shapes=[pltpu.VMEM((seq, LANES), F32)],
        compiler_params=_cparams(2),
        name="gla",
    )(u3, u3, u3, u3, u3, wg2_hi, wg2_lo, b_g2.reshape(1, -1), ng, tril)


def _t5_bucket(dist):
    max_exact = REL_BUCKETS // 2
    large = max_exact + (jnp.log(jnp.maximum(dist, 1).astype(F32) / max_exact)
                         / math.log(REL_MAX_DIST / max_exact) * (REL_BUCKETS - max_exact)).astype(I32)
    large = jnp.minimum(large, REL_BUCKETS - 1)
    return jnp.where(dist < max_exact, dist, large)


LOG2E = math.log2(math.e)
DIL_GROUP = 8


def _bias_kernel(rel_ref, bucket_ref, o_ref):
    h = pl.program_id(1)
    blk = DIL_BLOCK
    bucket = bucket_ref[0]
    acc = jnp.zeros(bucket.shape, F32)
    for b in range(REL_BUCKETS):
        acc = jnp.where(bucket == b, rel_ref[b, h], acc)
    acc = acc * LOG2E
    qi = lax.broadcasted_iota(I32, (blk, 2 * blk), 0)
    kj = lax.broadcasted_iota(I32, (blk, 2 * blk), 1)
    steps = qi + blk - kj
    in_window = (steps >= 0) & (steps <= blk)
    neg = jnp.float32(-jnp.inf)
    o_ref[0, 0, 0] = jnp.where(in_window, acc, neg)
    o_ref[0, 0, 1] = jnp.where(in_window & (kj >= blk), acc, neg)


def _bias_tables(rel_bias):
    assert all(window // dil == DIL_BLOCK for window, dil in DIL_BRANCHES)
    qi = jnp.arange(DIL_BLOCK)[:, None]
    kj = jnp.arange(2 * DIL_BLOCK)[None, :]
    steps = jnp.maximum(qi + DIL_BLOCK - kj, 0)
    buckets = jnp.stack([_t5_bucket(steps * dil) for _, dil in DIL_BRANCHES]).astype(I32)
    nbr = len(DIL_BRANCHES)
    return pl.pallas_call(
        _bias_kernel,
        grid=(nbr, DIL_HEADS),
        in_specs=[pl.BlockSpec(memory_space=pltpu.SMEM),
                  pl.BlockSpec((1, DIL_BLOCK, 2 * DIL_BLOCK), lambda i, h: (i, 0, 0))],
        out_specs=pl.BlockSpec((1, 1, 2, DIL_BLOCK, 2 * DIL_BLOCK), lambda i, h: (i, h, 0, 0, 0)),
        out_shape=jax.ShapeDtypeStruct((nbr, DIL_HEADS, 2, DIL_BLOCK, 2 * DIL_BLOCK), F32),
        compiler_params=_cparams(2),
        name="rel_bias_tables",
    )(rel_bias, buckets)


def _dil_kernel(q_ref, k_ref, v_ref, bias_ref, o_ref, qs, ks, vs, ob, lb):
    seq = q_ref.shape[1]
    blk = DIL_BLOCK
    scale = DIL_HD ** -0.5 * LOG2E
    lane = lax.broadcasted_iota(I32, (1, LANES), 1)
    first = lane < DIL_HD
    head_mask = [first.astype(F32), 1.0 - first.astype(F32)]

    ks[0:blk, :] = jnp.zeros((blk, LANES), BF16)
    vs[0:blk, :] = jnp.zeros((blk, LANES), BF16)

    for bi, (window, dil) in enumerate(DIL_BRANCHES):
        sub_len = seq // dil
        nb = sub_len // blk
        nb_shift = nb.bit_length() - 1
        for r in range(dil):
            rows = pl.ds(r, sub_len, stride=dil) if dil > 1 else pl.ds(0, sub_len)
            q = q_ref[0, rows, :] * scale
            for h in range(2):
                qs[h, r * sub_len:(r + 1) * sub_len, :] = (q * head_mask[h]).astype(BF16)
            ks[blk + r * sub_len:blk + (r + 1) * sub_len, :] = k_ref[0, rows, :].astype(BF16)
            vs[blk + r * sub_len:blk + (r + 1) * sub_len, :] = v_ref[0, rows, :].astype(BF16)

        def body(it, carry, bi=bi, dil=dil, nb=nb, nb_shift=nb_shift):
            grp = range(DIL_GROUP)
            g = [it * DIL_GROUP + gg for gg in grp]
            n = [gi & (nb - 1) for gi in g]
            g0 = [pl.multiple_of(gi * blk, blk) for gi in g]
            kb = [ks[pl.ds(s0, 2 * blk), :] for s0 in g0]
            sc = [[lax.dot_general(qs[h, pl.ds(g0[gg], blk), :], kb[gg], (((1,), (1,)), ((), ())),
                                   preferred_element_type=F32) + bias_ref[bi, h, jnp.where(n[gg] == 0, 1, 0)]
                   for h in range(2)] for gg in grp]
            stats = []
            for gg in grp:
                per_head = []
                for h in range(2):
                    m = jnp.max(sc[gg][h], axis=-1, keepdims=True)
                    p = jnp.exp2(sc[gg][h] - m)
                    l = jnp.sum(p, axis=-1, keepdims=True)
                    per_head.append((p.astype(BF16), m, l))
                stats.append(per_head)
            for gg in grp:
                vb = vs[pl.ds(g0[gg], 2 * blk), :]
                o_acc = None
                l_acc = None
                for h in range(2):
                    p, m, l = stats[gg][h]
                    o_h = jnp.dot(p, vb, preferred_element_type=F32) * (1.0 / l)
                    lse = jnp.broadcast_to(m + jnp.log2(l), (blk, LANES))
                    if h == 0:
                        o_acc, l_acc = o_h, lse
                    else:
                        o_acc = jnp.where(first, o_acc, o_h)
                        l_acc = jnp.where(first, l_acc, lse)
                if dil > 1:
                    dst = pl.ds(n[gg] * (blk * dil) + (g[gg] >> nb_shift), blk, stride=dil)
                else:
                    dst = pl.ds(g0[gg], blk)
                ob[bi, dst, :] = o_acc
                lb[bi, dst, :] = l_acc
            return carry

        lax.fori_loop(0, seq // blk // DIL_GROUP, body, 0)

    nbr = len(DIL_BRANCHES)
    mc = 256
    for c in range(seq // mc):
        rs = slice(c * mc, (c + 1) * mc)
        ls = [lb[i, rs, :] for i in range(nbr)]
        m = functools.reduce(jnp.maximum, ls)
        ws = [jnp.exp2(l - m) for l in ls]
        num = functools.reduce(lambda a, b: a + b, [w * ob[i, rs, :] for i, w in enumerate(ws)])
        den = functools.reduce(lambda a, b: a + b, ws)
        o_ref[0, rs, :] = num / den


def _dilated(u3, bias_tab):
    bsz, seq, _ = u3.shape
    npair = DIL_HEADS // 2
    nbr = len(DIL_BRANCHES)

    def col(c):
        return pl.BlockSpec((1, seq, LANES), lambda b, p, c=c: (b, 0, c // LANES + p))

    return pl.pallas_call(
        _dil_kernel,
        grid=(bsz, npair),
        in_specs=[col(COL_CQ), col(COL_CK), col(COL_CV),
                  pl.BlockSpec((nbr, 2, 2, DIL_BLOCK, 2 * DIL_BLOCK), lambda b, p: (0, p, 0, 0, 0))],
        out_specs=pl.BlockSpec((1, seq, LANES), lambda b, p: (b, 0, p)),
        out_shape=jax.ShapeDtypeStruct((bsz, seq, W_DIL), F32),
        scratch_shapes=[pltpu.VMEM((2, seq, LANES), BF16),
                        pltpu.VMEM((DIL_BLOCK + seq, LANES), BF16),
                        pltpu.VMEM((DIL_BLOCK + seq, LANES), BF16),
                        pltpu.VMEM((nbr, seq, LANES), F32),
                        pltpu.VMEM((nbr, seq, LANES), F32)],
        compiler_params=_cparams(2),
        name="dilated_attn",
    )(u3, u3, u3, bias_tab)


TOK_TILE = 256
OUT_TILE = 512
OUT_SPLIT = 2
FINAL_SPLIT = 2
PLAN_TILE = 1024


def _outproj_kernel(x_ref, ya_ref, yb_ref, yc_ref, yd_ref, ms_ref, wo_ref, g_ref, b_ref, rw_ref, rb_ref,
                    x1_ref, x1b_ref, x1p_ref, ids_ref, gates_ref):
    tm = x_ref.shape[0]
    rows = tm // OUT_SPLIT
    parts = [slice(pi * rows, (pi + 1) * rows) for pi in range(OUT_SPLIT)]
    accs = []
    for rs in parts:
        acc = None
        for gi, y_ref in enumerate((ya_ref, yb_ref, yc_ref, yd_ref)):
            lo = gi * W_CONF
            y = (y_ref[rs, :] * ms_ref[0, :, lo:lo + W_CONF]).astype(BF16)
            part = jnp.dot(y, wo_ref[0, lo:lo + W_CONF, :], preferred_element_type=F32)
            acc = part if acc is None else acc + part
        accs.append(acc)

    logits = []
    for rs, acc in zip(parts, accs):
        x1 = _layer_norm(DN_ALPHA * x_ref[rs, :] + acc, g_ref[0], b_ref[0])
        x1_ref[rs, :] = x1
        x1_hi = x1.astype(BF16)
        x1b_ref[rs, :] = x1_hi
        _store_row_tiles(x1p_ref.at[pl.ds(rs.start * SUBLANES, rows * SUBLANES)], _pack_bf16_pairs(x1))
        x1_lo = (x1 - x1_hi.astype(F32)).astype(BF16)
        hh_hl = jnp.dot(x1_hi, rw_ref[...], preferred_element_type=F32)
        logits.append(hh_hl[:, :LANES] + hh_hl[:, LANES:]
                      + jnp.dot(x1_lo, rw_ref[:, :LANES], preferred_element_type=F32) + rb_ref[...])

    lane = lax.broadcasted_iota(I32, (rows, LANES), 1)
    neg = jnp.float32(-jnp.inf)
    for rs, lg in zip(parts, logits):
        gl = jnp.where(lane < N_GROUPS, lg, neg)
        gmax = jnp.max(gl, axis=-1, keepdims=True)
        g_top = 1.0 / jnp.sum(jnp.exp(gl - gmax), axis=-1, keepdims=True)
        g_idx = jnp.min(jnp.where(gl == gmax, lane, LANES), axis=-1, keepdims=True)
        e_lo = N_GROUPS + g_idx * EXPERTS_PER_GROUP
        el = jnp.where((lane >= e_lo) & (lane < e_lo + EXPERTS_PER_GROUP), lg, neg)
        m1 = jnp.max(el, axis=-1, keepdims=True)
        i1 = jnp.min(jnp.where(el == m1, lane, LANES), axis=-1, keepdims=True)
        el2 = jnp.where(lane == i1, neg, el)
        m2 = jnp.max(el2, axis=-1, keepdims=True)
        i2 = jnp.min(jnp.where(el2 == m2, lane, LANES), axis=-1, keepdims=True)
        e21 = jnp.exp(m2 - m1)
        w1 = g_top / (1.0 + e21)
        w2 = g_top * e21 / (1.0 + e21)
        ids_ref[rs, :] = jnp.where(lane == 0, i1 - N_GROUPS, jnp.where(lane == 1, i2 - N_GROUPS, N_EXPERTS))
        gates_ref[rs, :] = jnp.where(lane == 0, w1, jnp.where(lane == 1, w2, 0.0))


def _outproj(layer, x, ya, yb, yc, yd, mix_scale, w_out_bf, ln_g, ln_b, rw_cat, rb):
    t, d = x.shape
    tm = OUT_TILE

    def row(w):
        return pl.BlockSpec((tm, w), lambda i: (i, 0))

    def full(shape):
        return pl.BlockSpec(shape, lambda i: (0,) * len(shape))

    def per_layer(shape):
        return pl.BlockSpec((1,) + shape, lambda i: (layer,) + (0,) * len(shape))

    return pl.pallas_call(
        _outproj_kernel,
        grid=(t // tm,),
        in_specs=[row(d), row(W_CONF), row(W_GLA), row(W_DIL), row(W_SC), per_layer((1, d)), per_layer((d, d)),
                  per_layer((1, d)), per_layer((1, d)), full((d, 2 * LANES)), full((1, LANES))],
        out_specs=[row(d), row(d), pl.BlockSpec((tm * SUBLANES, LANES), lambda i: (i, 0)), row(LANES), row(LANES)],
        out_shape=[jax.ShapeDtypeStruct((t, d), F32), jax.ShapeDtypeStruct((t, d), BF16),
                   jax.ShapeDtypeStruct((t * SUBLANES, LANES), U32),
                   jax.ShapeDtypeStruct((t, LANES), I32), jax.ShapeDtypeStruct((t, LANES), F32)],
        compiler_params=_cparams(1),
        name="out_proj_ln_router",
    )(x, ya, yb, yc, yd, mix_scale, w_out_bf, ln_g, ln_b, rw_cat, rb)


def _plan_kernel(ids_ref, tril_ref, upper_ref, pos_ref, tile_ref, cnt, carry, start):
    phase = pl.program_id(0)
    i = pl.program_id(1)
    tm = ids_ref.shape[0]
    lane = lax.broadcasted_iota(I32, (tm, LANES), 1)
    ids = ids_ref[...]
    e1 = ids[:, 0:1]
    e2 = ids[:, 1:2]
    oh1 = (lane == e1).astype(F32)
    oh2 = (lane == e2).astype(F32)
    oh = oh1 + oh2

    @pl.when((phase == 0) & (i == 0))
    def _():
        cnt[...] = jnp.zeros(cnt.shape, F32)

    @pl.when(phase == 0)
    def _():
        cnt[...] = cnt[...] + jnp.sum(oh, axis=0, keepdims=True)

    @pl.when((phase == 1) & (i == 0))
    def _():
        padded = jnp.floor((cnt[...] + (MOE_TILE - 1)) * (1.0 / MOE_TILE)) * MOE_TILE
        p8 = jnp.broadcast_to(padded, (SUBLANES, LANES))
        st = jnp.dot(p8, upper_ref[...], precision=HIGHEST, preferred_element_type=F32)
        start[...] = st[0:1, :]
        carry[...] = jnp.zeros(carry.shape, F32)
        end = st[0:1, :] + padded
        trow = lax.broadcasted_iota(I32, (LANES, LANES), 0).astype(F32) * MOE_TILE
        tlane = lax.broadcasted_iota(I32, (LANES, LANES), 1)
        done = jnp.where((end <= trow) & (tlane < N_EXPERTS), 1.0, 0.0)
        tile_ref[...] = jnp.broadcast_to(jnp.sum(done, axis=-1, keepdims=True), (LANES, LANES)).astype(I32)

    @pl.when(phase == 1)
    def _():
        incl = jnp.dot(tril_ref[...], oh.astype(BF16), preferred_element_type=F32) + carry[...]
        base = start[...] + incl - oh
        p1 = jnp.sum(oh1 * base, axis=-1, keepdims=True)
        p2 = jnp.sum(oh2 * base, axis=-1, keepdims=True)
        pos_ref[...] = jnp.where(lane == 0, p1, jnp.where(lane == 1, p2, 0.0)).astype(I32)
        carry[...] = carry[...] + jnp.sum(oh, axis=0, keepdims=True)


def _plan(ids):
    t = ids.shape[0]
    tm = PLAN_TILE
    tril = jnp.tril(jnp.ones((tm, tm), BF16))
    upper = jnp.triu(jnp.ones((LANES, LANES), F32), k=1)
    return pl.pallas_call(
        _plan_kernel,
        grid=(2, t // tm),
        in_specs=[pl.BlockSpec((tm, LANES), lambda ph, i: (i, 0)),
                  pl.BlockSpec((tm, tm), lambda ph, i: (0, 0)),
                  pl.BlockSpec((LANES, LANES), lambda ph, i: (0, 0))],
        out_specs=[pl.BlockSpec((tm, LANES), lambda ph, i: (i * ph, 0)),
                   pl.BlockSpec((LANES, LANES), lambda ph, i: (0, 0))],
        out_shape=[jax.ShapeDtypeStruct((t, LANES), I32), jax.ShapeDtypeStruct((LANES, LANES), I32)],
        scratch_shapes=[pltpu.VMEM((1, LANES), F32), pltpu.VMEM((1, LANES), F32), pltpu.VMEM((1, LANES), F32)],
        compiler_params=_cparams(2),
        name="route_plan",
    )(ids, tril, upper)


def _dispatch_kernel(pos_ref, x_ref, xs_ref, sem):
    i = pl.program_id(0)
    tm = x_ref.shape[0] // SUBLANES
    t = pl.num_programs(0) * tm

    for m in range(tm):
        for k in range(TOP_K):
            dst = pl.multiple_of(pos_ref[k * t + i * tm + m] * SUBLANES, SUBLANES)
            pltpu.make_async_copy(x_ref.at[pl.ds(m * SUBLANES, SUBLANES)], xs_ref.at[pl.ds(dst, SUBLANES)],
                                  sem).start(priority=k)
    for _ in range(TOP_K):
        pltpu.make_async_copy(x_ref, xs_ref.at[pl.ds(0, tm * SUBLANES)], sem).wait()


def _dispatch(pos_flat, x1p, n_rows):
    t = x1p.shape[0] // SUBLANES
    tm = TOK_TILE
    return pl.pallas_call(
        _dispatch_kernel,
        grid_spec=pltpu.PrefetchScalarGridSpec(
            num_scalar_prefetch=1,
            grid=(t // tm,),
            in_specs=[pl.BlockSpec((tm * SUBLANES, LANES), lambda i, pos: (i, 0))],
            out_specs=pl.BlockSpec(memory_space=pl.ANY),
            scratch_shapes=[pltpu.SemaphoreType.DMA],
        ),
        out_shape=jax.ShapeDtypeStruct((n_rows * SUBLANES, LANES), x1p.dtype),
        compiler_params=_cparams(1),
        name="moe_dispatch",
    )(pos_flat, x1p)


def _expert_kernel(te_ref, na_ref, xs_ref, wg_ref, wu_ref, wd_ref, ys_ref):
    i = pl.program_id(0)

    @pl.when(i < na_ref[0])
    def _():
        xb = _unpack_bf16_pairs(_load_row_tiles(xs_ref, MOE_TILE)).astype(BF16)
        g = jnp.dot(xb, wg_ref[0, 0].astype(BF16), preferred_element_type=F32)
        u = jnp.dot(xb, wu_ref[0, 0].astype(BF16), preferred_element_type=F32)
        h = (_silu(g) * u).astype(BF16)
        y = jnp.dot(h, wd_ref[0, 0].astype(BF16), preferred_element_type=F32)
        _store_row_tiles(ys_ref, _pack_bf16_pairs(y))


def _experts(layer, tile_expert, n_active, xs, w_gate, w_up, w_down):
    n_rows = xs.shape[0] // SUBLANES
    d = LANES
    n_tiles = n_rows // MOE_TILE
    d_model = w_gate.shape[2]

    def rows(i, te, na):
        return (jnp.minimum(i, na[0] - 1), 0)

    def expert(i, te, na):
        return (layer, te[i], 0, 0)

    return pl.pallas_call(
        _expert_kernel,
        grid_spec=pltpu.PrefetchScalarGridSpec(
            num_scalar_prefetch=2,
            grid=(n_tiles,),
            in_specs=[pl.BlockSpec((MOE_TILE * SUBLANES, d), rows),
                      pl.BlockSpec((1, 1, d_model, D_EXPERT), expert),
                      pl.BlockSpec((1, 1, d_model, D_EXPERT), expert),
                      pl.BlockSpec((1, 1, D_EXPERT, d_model), expert)],
            out_specs=pl.BlockSpec((MOE_TILE * SUBLANES, d), rows),
        ),
        out_shape=jax.ShapeDtypeStruct((n_rows * SUBLANES, d), U32),
        compiler_params=_cparams(1),
        name="moe_experts",
    )(tile_expert, n_active, xs, w_gate, w_up, w_down)


def _final_kernel(pos_ref, x1_ref, x1b_ref, gates_ref, p_ref, wg_ref, bg_ref, wp_ref, g_ref, b_ref, ys_ref,
                  x2_ref, x2b_ref, ybuf, sem):
    i = pl.program_id(0)
    n_steps = pl.num_programs(0)
    tm = x1_ref.shape[0]
    t = n_steps * tm

    def gather(step, slot):
        for m in range(tm):
            for k in range(TOP_K):
                src = pl.multiple_of(pos_ref[k * t + step * tm + m] * SUBLANES, SUBLANES)
                pltpu.make_async_copy(ys_ref.at[pl.ds(src, SUBLANES)],
                                      ybuf.at[slot, k, pl.ds(m * SUBLANES, SUBLANES)],
                                      sem.at[slot]).start(priority=k)

    def wait_slot(slot):
        for k in range(TOP_K):
            pltpu.make_async_copy(ys_ref.at[pl.ds(0, tm * SUBLANES)], ybuf.at[slot, k], sem.at[slot]).wait()

    @pl.when(i == 0)
    def _():
        gather(0, 0)

    slot = i % 2
    wait_slot(slot)
    gather(jnp.minimum(i + 1, n_steps - 1), 1 - slot)

    rows = tm // FINAL_SPLIT
    parts = [slice(pi * rows, (pi + 1) * rows) for pi in range(FINAL_SPLIT)]
    ples = []
    for rs in parts:
        gate = _sigmoid(jnp.dot(x1b_ref[rs, :], wg_ref[0], preferred_element_type=F32) + bg_ref[0])
        ples.append(gate * jnp.dot(p_ref[0, rs, :].astype(BF16), wp_ref[0], preferred_element_type=F32))
    for rs, ple in zip(parts, ples):
        gates = gates_ref[rs, :]
        tiles = pl.ds(rs.start * SUBLANES, rows * SUBLANES)
        ffn = (gates[:, 0:1] * _unpack_bf16_pairs(_load_row_tiles(ybuf.at[slot, 0, tiles], rows))
               + gates[:, 1:2] * _unpack_bf16_pairs(_load_row_tiles(ybuf.at[slot, 1, tiles], rows)))
        x2 = _layer_norm(DN_ALPHA * x1_ref[rs, :] + ffn + ple, g_ref[0], b_ref[0])
        x2_ref[rs, :] = x2
        x2b_ref[rs, :] = x2.astype(BF16)

    @pl.when(i == n_steps - 1)
    def _():
        wait_slot(1 - slot)


def _final(layer, pos_flat, x1, x1b, gates, p, wg_bf, bg, wp_bf, ln_g, ln_b, ys):
    t, d = x1.shape
    tm = TOK_TILE

    def row(w):
        return pl.BlockSpec((tm, w), lambda i, pos: (i, 0))

    def per_layer(shape):
        return pl.BlockSpec((1,) + shape, lambda i, pos: (layer,) + (0,) * len(shape))

    return pl.pallas_call(
        _final_kernel,
        grid_spec=pltpu.PrefetchScalarGridSpec(
            num_scalar_prefetch=1,
            grid=(t // tm,),
            in_specs=[row(d), row(d), row(LANES), pl.BlockSpec((1, tm, PLE_DIM), lambda i, pos: (layer, i, 0)),
                      per_layer((d, d)), per_layer((1, d)), per_layer((PLE_DIM, d)),
                      per_layer((1, d)), per_layer((1, d)), pl.BlockSpec(memory_space=pl.ANY)],
            out_specs=[row(d), row(d)],
            scratch_shapes=[pltpu.VMEM((2, TOP_K, tm * SUBLANES, LANES), U32), pltpu.SemaphoreType.DMA((2,))],
        ),
        out_shape=[jax.ShapeDtypeStruct((t, d), F32), jax.ShapeDtypeStruct((t, d), BF16)],
        compiler_params=_cparams(1),
        name="combine_ple_ln",
    )(pos_flat, x1, x1b, gates, p, wg_bf, bg, wp_bf, ln_g, ln_b, ys)


W_PREP_COLS = 512


def _arrange_kernel(wt_ref, o_ref):
    j = pl.program_id(1)
    wt = wt_ref[0]
    row = lax.broadcasted_iota(I32, (wt.shape[0], 1), 0)
    n_valid = jnp.where(j == COL_GL // W_PREP_COLS, GLA_RANK, W_PREP_COLS)
    o_ref[0] = jnp.where(row < n_valid, wt, 0.0).T.astype(BF16)


def _arrange_w_in(w):
    n_layer, k, n_in = w.shape
    assert sum(IN_SIZES[:5]) == COL_GL + GLA_RANK and n_in + GL_PAD - GLA_RANK == U_COLS
    assert COL_GL % W_PREP_COLS == 0 and GL_PAD == W_PREP_COLS
    gl_blk = COL_GL // W_PREP_COLS

    def src_row(l, j):
        row = jnp.where(j <= gl_blk, j * W_PREP_COLS, j * W_PREP_COLS - (GL_PAD - GLA_RANK))
        return (l, pl.multiple_of(row, GLA_RANK), 0)

    return pl.pallas_call(
        _arrange_kernel,
        grid=(n_layer, U_COLS // W_PREP_COLS),
        in_specs=[pl.BlockSpec((pl.Element(1), pl.Element(W_PREP_COLS), pl.Element(k)), src_row)],
        out_specs=pl.BlockSpec((1, k, W_PREP_COLS), lambda l, j: (l, 0, j)),
        out_shape=jax.ShapeDtypeStruct((n_layer, k, U_COLS), BF16),
        compiler_params=_cparams(2),
        name="arrange_w_in",
    )(jnp.swapaxes(w, 1, 2))


def kernel(x, p, w_in, conf_dw_w, conf_dw_b, conf_ln_g, conf_ln_b, gla_w_g2, gla_b_g2, gla_norm_g, sc_conv_w, mix_scale, w_out, rel_bias, ln1_g, ln1_b, router_g_w, router_g_b, router_e_w, router_e_b, exp_w_gate, exp_w_up, exp_w_down, ple_w_gate, ple_b_gate, ple_w_proj, ln2_g, ln2_b):
    bsz, seq, d = x.shape
    t = bsz * seq
    n_rows = TOP_K * t + N_EXPERTS * MOE_TILE
    n_tiles = n_rows // MOE_TILE
    bias_tab = _bias_tables(rel_bias)
    w_in_bf = _arrange_w_in(w_in)
    w_out_bf = w_out.astype(BF16)
    ple_wg_bf = ple_w_gate.astype(BF16)
    ple_wp_bf = ple_w_proj.astype(BF16)
    p3 = p.reshape(DEPTH, t, PLE_DIM)

    def vec(a):
        return a.reshape(DEPTH, 1, -1)

    xf = x.reshape(t, d)
    xb = xf
    for i in range(DEPTH):
        u = _in_proj(xb, w_in_bf, i, 512, 2048, F32)
        u3 = u.reshape(bsz, seq, U_COLS)
        ya, yd = _conv_mixers(u3, conf_dw_w[i], conf_dw_b[i], conf_ln_g[i], conf_ln_b[i], sc_conv_w[i])
        yb = _gla(u3, gla_w_g2[i], gla_b_g2[i], gla_norm_g[i])
        yc = _dilated(u3, bias_tab)
        rw = jnp.zeros((d, LANES), F32).at[:, :N_GROUPS].set(router_g_w[i]).at[:, N_GROUPS:N_GROUPS + N_EXPERTS].set(router_e_w[i])
        rb = jnp.zeros((1, LANES), F32).at[0, :N_GROUPS].set(router_g_b[i]).at[0, N_GROUPS:N_GROUPS + N_EXPERTS].set(router_e_b[i])
        rw_hi = rw.astype(BF16)
        rw_lo = (rw - rw_hi.astype(F32)).astype(BF16)
        rw_cat = jnp.concatenate([rw_hi, rw_lo], axis=1)
        x1, x1b, x1p, ids, gates = _outproj(i, xf, ya.reshape(t, -1), yb.reshape(t, -1), yc.reshape(t, -1),
                                            yd.reshape(t, -1), vec(mix_scale), w_out_bf, vec(ln1_g), vec(ln1_b),
                                            rw_cat, rb)
        pos, tile_tab = _plan(ids)
        pos_flat = pos[:, :TOP_K].T.reshape(-1)
        tile_end = tile_tab[:n_tiles, 0]
        n_active = jnp.sum((tile_end < N_EXPERTS).astype(I32)).reshape(1)
        tile_expert = jnp.minimum(tile_end, N_EXPERTS - 1)
        xs = _dispatch(pos_flat, x1p, n_rows)
        ys = _experts(i, tile_expert, n_active, xs, exp_w_gate, exp_w_up, exp_w_down)
        xf, xb = _final(i, pos_flat, x1, x1b, gates, p3, ple_wg_bf, vec(ple_b_gate), ple_wp_bf, vec(ln2_g), vec(ln2_b), ys)
    return xf.reshape(bsz, seq, d)
```

```python
import functools
import math

import jax
import jax.numpy as jnp
from jax import lax
from jax.experimental import pallas as pl
from jax.experimental.pallas import tpu as pltpu

F32 = jnp.float32
BF16 = jnp.bfloat16
I32 = jnp.int32
HIGHEST = lax.Precision.HIGHEST

D_MODEL = 2048
DEPTH = 2
PLE_DIM = 256
W_CONF = 512
W_GLA = 512
W_DIL = 512
W_SC = 512
CONF_KERNEL = 31
GLA_HEADS = 4
GLA_DV = 128
GLA_DK = 64
GLA_RANK = 16
GLA_TAU = 16.0
DIL_HEADS = 8
DIL_HD = 64
DIL_BRANCHES = ((128, 1), (512, 4), (2048, 16))
DIL_BLOCK = 128
REL_BUCKETS = 32
REL_MAX_DIST = 2048
SC_KERNEL = 3
N_GROUPS = 4
EXPERTS_PER_GROUP = 8
N_EXPERTS = 32
TOP_K = 2
D_EXPERT = 256
DN_ALPHA = (2 * DEPTH) ** 0.25
LN_EPS = 1e-5
RMS_EPS = 1e-6
IN_SIZES = (2 * W_CONF, GLA_HEADS * GLA_DK, GLA_HEADS * GLA_DK, W_GLA, GLA_RANK, W_GLA,
            W_DIL, W_DIL, W_DIL, W_SC, W_SC, W_SC)

LANES = 128
SUBLANES = 8
VMEM_LIMIT = 56 * 1024 * 1024

GL_PAD = 512
COL_A = 0
COL_GQ = 1024
COL_GK = 1280
COL_GV = 1536
COL_GL = 2048
COL_GR = 2560
COL_CQ = 3072
COL_CK = 3584
COL_CV = 4096
COL_SB = 4608
COL_SC = 5120
COL_SH = 5632
U_COLS = 6144

MOE_TILE = 512


def _cparams(n_axes):
    return pltpu.CompilerParams(dimension_semantics=("arbitrary",) * n_axes, vmem_limit_bytes=VMEM_LIMIT)


def _layer_norm(x, g, b):
    mu = jnp.mean(x, axis=-1, keepdims=True)
    xc = x - mu
    var = jnp.mean(xc * xc, axis=-1, keepdims=True)
    return xc * lax.rsqrt(var + LN_EPS) * g + b


U32 = jnp.uint32


def _pack_bf16_pairs(x):
    n = x.shape[1] // 2
    bits = lax.bitcast_convert_type(x.astype(BF16).astype(F32), U32)
    return (bits[:, :n] >> 16) | (bits[:, n:] & jnp.uint32(0xFFFF0000))


def _unpack_bf16_pairs(w):
    lo = lax.bitcast_convert_type(w << 16, F32)
    hi = lax.bitcast_convert_type(w & jnp.uint32(0xFFFF0000), F32)
    return jnp.concatenate([lo, hi], axis=1)


def _store_row_tiles(ref, packed):
    rows, width = packed.shape
    assert width == SUBLANES * LANES
    for s in range(SUBLANES):
        ref[pl.ds(s, rows, stride=SUBLANES), :] = packed[:, s * LANES:(s + 1) * LANES]


def _load_row_tiles(ref, rows):
    return jnp.concatenate([ref[pl.ds(s, rows, stride=SUBLANES), :] for s in range(SUBLANES)], axis=1)


def _sigmoid(x):
    return 1.0 / (1.0 + jnp.exp(-x))


def _silu(x):
    return x * _sigmoid(x)


def _matmul_kernel(x_ref, w_ref, o_ref):
    o_ref[...] = jnp.dot(x_ref[...].astype(BF16), w_ref[0],
                         preferred_element_type=F32).astype(o_ref.dtype)


def _in_proj(x, w_all, layer, tm, tn, out_dtype):
    m, k = x.shape
    n = w_all.shape[2]
    return pl.pallas_call(
        _matmul_kernel,
        grid=(n // tn, m // tm),
        in_specs=[pl.BlockSpec((tm, k), lambda j, i: (i, 0)),
                  pl.BlockSpec((1, k, tn), lambda j, i: (layer, 0, j))],
        out_specs=pl.BlockSpec((tm, tn), lambda j, i: (i, j)),
        out_shape=jax.ShapeDtypeStruct((m, n), out_dtype),
        compiler_params=_cparams(2),
        name="in_proj",
    )(x, w_all)


CONV_TS = 512
CONV_RC = 64
CONF_HIST = 32
SC_HIST = 8


def _conv_kernel(a_ref, gate_ref, sb_ref, sc_ref, sh_ref, cw_ref, cb_ref, lg_ref, lb_ref, sw_ref,
                 ya_ref, yd_ref, ha, hd, hs):
    s = pl.program_id(1)
    ts = a_ref.shape[1]
    n_ext = CONF_HIST + ts

    @pl.when(s == 0)
    def _():
        ha[0:CONF_HIST, :] = jnp.zeros((CONF_HIST, W_CONF), F32)
        hd[0:SC_HIST, :] = jnp.zeros((SC_HIST, W_SC), F32)

    ha[CONF_HIST:CONF_HIST + ts, :] = a_ref[0] * _sigmoid(gate_ref[0])
    hd[SC_HIST:SC_HIST + ts, :] = sc_ref[0] * sh_ref[0]

    cb = cb_ref[...]
    lg = lg_ref[...]
    lb = lb_ref[...]
    off_a = CONF_HIST - (CONF_KERNEL - 1)
    h_all = ha[...]
    for sh in range(1, SUBLANES):
        h_all = pltpu.roll(h_all, n_ext - 1, 0)
        hs[sh - 1] = h_all
    for c in range(ts // CONV_RC):
        r0 = c * CONV_RC
        acc = jnp.zeros((CONV_RC, W_CONF), F32)
        for j in range(CONF_KERNEL):
            sh = (off_a + j) % SUBLANES
            lo = r0 + off_a + j - sh
            rows = ha[lo:lo + CONV_RC, :] if sh == 0 else hs[sh - 1, lo:lo + CONV_RC, :]
            acc = acc + cw_ref[j:j + 1, :] * rows
        hh = _layer_norm(acc + cb, lg, lb)
        ya_ref[0, r0:r0 + CONV_RC, :] = _silu(hh)

    off_d = SC_HIST - (SC_KERNEL - 1)
    accd = jnp.zeros((ts, W_SC), F32)
    for j in range(SC_KERNEL):
        accd = accd + sw_ref[j:j + 1, :] * hd[off_d + j:off_d + j + ts, :]
    yd_ref[0] = sb_ref[0] * accd

    ha[0:CONF_HIST, :] = ha[ts:ts + CONF_HIST, :]
    hd[0:SC_HIST, :] = hd[ts:ts + SC_HIST, :]


def _conv_mixers(u3, conf_w, conf_b, conf_lg, conf_lb, sc_w):
    bsz, seq, _ = u3.shape
    ts = CONV_TS
    wblk = W_CONF

    def col(c):
        return pl.BlockSpec((1, ts, wblk), lambda b, s, c=c: (b, s, c // wblk))

    def full(shape):
        return pl.BlockSpec(shape, lambda b, s: (0,) * len(shape))

    cw = jnp.zeros((32, W_CONF), F32).at[:CONF_KERNEL].set(conf_w)
    sw = jnp.zeros((8, W_SC), F32).at[:SC_KERNEL].set(sc_w)
    out_spec = pl.BlockSpec((1, ts, wblk), lambda b, s: (b, s, 0))
    return pl.pallas_call(
        _conv_kernel,
        grid=(bsz, seq // ts),
        in_specs=[col(COL_A), col(COL_A + W_CONF), col(COL_SB), col(COL_SC), col(COL_SH),
                  full((32, W_CONF)), full((1, W_CONF)), full((1, W_CONF)), full((1, W_CONF)), full((8, W_SC))],
        out_specs=[out_spec, out_spec],
        out_shape=[jax.ShapeDtypeStruct((bsz, seq, W_CONF), F32), jax.ShapeDtypeStruct((bsz, seq, W_SC), F32)],
        scratch_shapes=[pltpu.VMEM((CONF_HIST + ts, W_CONF), F32), pltpu.VMEM((SC_HIST + ts, W_SC), F32),
                        pltpu.VMEM((SUBLANES - 1, CONF_HIST + ts, W_CONF), F32)],
        compiler_params=_cparams(2),
        name="conv_mixers",
    )(u3, u3, u3, u3, u3, cw, conf_b.reshape(1, -1), conf_lg.reshape(1, -1), conf_lb.reshape(1, -1), sw)


GLA_BLK = 128
GLA_SUB = 64
GLA_UNROLL = 4


def _split3(x):
    hi = x.astype(BF16)
    r1 = x - hi.astype(F32)
    mid = r1.astype(BF16)
    lo = (r1 - mid.astype(F32)).astype(BF16)
    return hi, mid, lo


def _gla_kernel(q_ref, k_ref, v_ref, gl_ref, r_ref, wg2h_ref, wg2l_ref, bg2_ref, ng_ref, tril_ref, o_ref, la_ref):
    seq = q_ref.shape[1]
    nblk = seq // GLA_BLK
    nsub = GLA_BLK // GLA_SUB
    assert nsub == 2 and nblk % GLA_UNROLL == 0

    lane = lax.broadcasted_iota(I32, (1, LANES), 1)
    head_mask = [(lane < GLA_DK).astype(F32), (lane >= GLA_DK).astype(F32)]
    row = lax.broadcasted_iota(I32, (GLA_BLK, GLA_BLK), 0)
    col = lax.broadcasted_iota(I32, (GLA_BLK, GLA_BLK), 1)
    sub_shift = GLA_SUB.bit_length() - 1
    diag_mask = ((row >> sub_shift) == (col >> sub_shift)) & (col <= row)
    second_half = lax.broadcasted_iota(I32, (GLA_BLK, 1), 0) >= GLA_SUB
    scale = GLA_DK ** -0.5
    ng = ng_ref[...]
    tril = tril_ref[...]

    gl = gl_ref[0]
    gl_hi = gl.astype(BF16)
    gl_lo = (gl - gl_hi.astype(F32)).astype(BF16)
    wh = wg2h_ref[...]
    z = (jnp.dot(gl_hi, wh, preferred_element_type=F32) + jnp.dot(gl_lo, wh, preferred_element_type=F32)
         + jnp.dot(gl_hi, wg2l_ref[...], preferred_element_type=F32) + bg2_ref[...])
    la_ref[...] = (jnp.minimum(z, 0.0) - jnp.log(1.0 + jnp.exp(-jnp.abs(z)))) * (1.0 / GLA_TAU)

    nt = (((1,), (1,)), ((), ()))

    def body(it, carry):
        grp = range(GLA_UNROLL)
        r0 = [pl.multiple_of((it * GLA_UNROLL + j) * GLA_BLK, GLA_BLK) for j in grp]
        q = [q_ref[0, pl.ds(r, GLA_BLK), :] * scale for r in r0]
        k = [k_ref[0, pl.ds(r, GLA_BLK), :] for r in r0]
        v = [v_ref[0, pl.ds(r, GLA_BLK), :].astype(BF16) for r in r0]
        b = []
        for r in r0:
            acc = None
            for part in _split3(la_ref[pl.ds(r, GLA_BLK), :]):
                t = jnp.dot(tril, part, preferred_element_type=F32)
                acc = t if acc is None else acc + t
            b.append(acc)

        s_in, q_st, kk, b_last = [], [], [], []
        for j in grp:
            b3 = b[j].reshape(nsub, GLA_SUB, LANES)
            mid = jnp.broadcast_to(b3[:, GLA_SUB // 2 - 1:GLA_SUB // 2, :], b3.shape).reshape(GLA_BLK, LANES)
            qd = q[j] * jnp.exp(b[j] - mid)
            kd = (k[j] * jnp.exp(mid - b[j])).astype(BF16)
            e0 = b[j][GLA_SUB - 1:GLA_SUB, :]
            q_off = jnp.where(second_half, q[j] * jnp.exp(jnp.minimum(b[j] - e0, 0.0)), 0.0)
            k_off = jnp.where(second_half, 0.0, k[j] * jnp.exp(jnp.minimum(e0 - b[j], 0.0))).astype(BF16)
            s_in.append((qd, kd, q_off, k_off))
            q_st.append(q[j] * jnp.exp(b[j]))
            b_last.append(b[j][GLA_BLK - 1:GLA_BLK, :])
            kk.append((k[j] * jnp.exp(b_last[j] - b[j])).astype(BF16))

        scores = []
        for j in grp:
            qd, kd, q_off, k_off = s_in[j]
            per_head = []
            for h in range(2):
                hm = head_mask[h]
                s_d = lax.dot_general((qd * hm).astype(BF16), kd, nt, preferred_element_type=F32)
                s_o = lax.dot_general((q_off * hm).astype(BF16), k_off, nt, preferred_element_type=F32)
                per_head.append((jnp.where(diag_mask, s_d, 0.0) + s_o).astype(BF16))
            scores.append(per_head)

        states = []
        st = carry
        for j in grp:
            states.append(st.astype(BF16))
            upd = lax.dot_general(kk[j], v[j], (((0,), (0,)), ((), ())), preferred_element_type=F32)
            decay = jnp.transpose(jnp.broadcast_to(jnp.exp(b_last[j]), (GLA_BLK, LANES)))
            st = st * jnp.concatenate([decay, decay], axis=1) + upd

        for j in grp:
            outs = []
            for h in range(2):
                lhs = jnp.concatenate([scores[j][h], (q_st[j] * head_mask[h]).astype(BF16)], axis=1)
                rhs = jnp.concatenate([v[j][:, h * GLA_DV:(h + 1) * GLA_DV],
                                       states[j][:, h * GLA_DV:(h + 1) * GLA_DV]], axis=0)
                o_h = jnp.dot(lhs, rhs, preferred_element_type=F32)
                outs.append(o_h * lax.rsqrt(jnp.mean(o_h * o_h, axis=-1, keepdims=True) + RMS_EPS))
            o = jnp.concatenate(outs, axis=1) * ng
            o_ref[0, pl.ds(r0[j], GLA_BLK), :] = o * _silu(r_ref[0, pl.ds(r0[j], GLA_BLK), :])
        return st

    lax.fori_loop(0, nblk // GLA_UNROLL, body, jnp.zeros((2 * GLA_DK, 2 * GLA_DV), F32))


def _gla(u3, w_g2, b_g2, norm_g):
    bsz, seq, _ = u3.shape
    npair = GLA_HEADS // 2
    kw = 2 * GLA_DK
    vw = 2 * GLA_DV

    def col(c, w):
        return pl.BlockSpec((1, seq, w), lambda b, p, c=c, w=w: (b, 0, c // w + p))

    wg2 = jnp.zeros((LANES, GLA_HEADS * GLA_DK), F32).at[:GLA_RANK].set(w_g2)
    wg2_hi = wg2.astype(BF16)
    wg2_lo = (wg2 - wg2_hi.astype(F32)).astype(BF16)
    ng = jnp.concatenate([norm_g, norm_g]).reshape(1, vw)
    tril = jnp.tril(jnp.ones((GLA_BLK, GLA_BLK), BF16))
    return pl.pallas_call(
        _gla_kernel,
        grid=(bsz, npair),
        in_specs=[col(COL_GQ, kw), col(COL_GK, kw), col(COL_GV, vw),
                  pl.BlockSpec((1, seq, LANES), lambda b, p: (b, 0, COL_GL // LANES)),
                  col(COL_GR, vw),
                  pl.BlockSpec((LANES, kw), lambda b, p: (0, p)),
                  pl.BlockSpec((LANES, kw), lambda b, p: (0, p)),
                  pl.BlockSpec((1, kw), lambda b, p: (0, p)),
                  pl.BlockSpec((1, vw), lambda b, p: (0, 0)),
                  pl.BlockSpec((GLA_BLK, GLA_BLK), lambda b, p: (0, 0))],
        out_specs=pl.BlockSpec((1, seq, vw), lambda b, p: (b, 0, p)),
        out_shape=jax.ShapeDtypeStruct((bsz, seq, W_GLA), F32),
        scratch_shapes=[pltpu.VMEM((seq, LANES), F32)],
        compiler_params=_cparams(2),
        name="gla",
    )(u3, u3, u3, u3, u3, wg2_hi, wg2_lo, b_g2.reshape(1, -1), ng, tril)


def _t5_bucket(dist):
    max_exact = REL_BUCKETS // 2
    large = max_exact + (jnp.log(jnp.maximum(dist, 1).astype(F32) / max_exact)
                         / math.log(REL_MAX_DIST / max_exact) * (REL_BUCKETS - max_exact)).astype(I32)
    large = jnp.minimum(large, REL_BUCKETS - 1)
    return jnp.where(dist < max_exact, dist, large)


LOG2E = math.log2(math.e)
DIL_STAGE = 4
DIL_GROUP = 8


def _bias_kernel(rel_ref, bucket_ref, o_ref):
    h = pl.program_id(1)
    blk = DIL_BLOCK
    bucket = bucket_ref[0]
    acc = jnp.zeros(bucket.shape, F32)
    for b in range(REL_BUCKETS):
        acc = jnp.where(bucket == b, rel_ref[b, h], acc)
    acc = acc * LOG2E
    qi = lax.broadcasted_iota(I32, (blk, 2 * blk), 0)
    kj = lax.broadcasted_iota(I32, (blk, 2 * blk), 1)
    steps = qi + blk - kj
    in_window = (steps >= 0) & (steps <= blk)
    neg = jnp.float32(-jnp.inf)
    o_ref[0, 0, 0] = jnp.where(in_window, acc, neg)
    o_ref[0, 0, 1] = jnp.where(in_window & (kj >= blk), acc, neg)


def _bias_tables(rel_bias):
    assert all(window // dil == DIL_BLOCK for window, dil in DIL_BRANCHES)
    qi = jnp.arange(DIL_BLOCK)[:, None]
    kj = jnp.arange(2 * DIL_BLOCK)[None, :]
    steps = jnp.maximum(qi + DIL_BLOCK - kj, 0)
    buckets = jnp.stack([_t5_bucket(steps * dil) for _, dil in DIL_BRANCHES]).astype(I32)
    nbr = len(DIL_BRANCHES)
    return pl.pallas_call(
        _bias_kernel,
        grid=(nbr, DIL_HEADS),
        in_specs=[pl.BlockSpec(memory_space=pltpu.SMEM),
                  pl.BlockSpec((1, DIL_BLOCK, 2 * DIL_BLOCK), lambda i, h: (i, 0, 0))],
        out_specs=pl.BlockSpec((1, 1, 2, DIL_BLOCK, 2 * DIL_BLOCK), lambda i, h: (i, h, 0, 0, 0)),
        out_shape=jax.ShapeDtypeStruct((nbr, DIL_HEADS, 2, DIL_BLOCK, 2 * DIL_BLOCK), F32),
        compiler_params=_cparams(2),
        name="rel_bias_tables",
    )(rel_bias, buckets)


def _dil_kernel(q_ref, k_ref, v_ref, bias_ref, o_ref, qs, ks, vs, ob, lb, stage):
    seq = q_ref.shape[1]
    blk = DIL_BLOCK
    scale = DIL_HD ** -0.5 * LOG2E
    lane = lax.broadcasted_iota(I32, (1, LANES), 1)
    first = lane < DIL_HD
    head_mask = [first.astype(F32), 1.0 - first.astype(F32)]

    ks[0:blk, :] = jnp.zeros((blk, LANES), BF16)
    vs[0:blk, :] = jnp.zeros((blk, LANES), BF16)

    for bi, (window, dil) in enumerate(DIL_BRANCHES):
        sub_len = seq // dil
        nb = sub_len // blk
        nb_shift = nb.bit_length() - 1
        for r in range(dil):
            dst = slice(r * sub_len, (r + 1) * sub_len)
            if dil == 1:
                q, k, v = q_ref[0], k_ref[0], v_ref[0]
            elif dil == DIL_STAGE:
                rows = pl.ds(r, sub_len, stride=dil)
                q, k, v = q_ref[0, rows, :], k_ref[0, rows, :], v_ref[0, rows, :]
                stage[0, dst, :] = q
                stage[1, dst, :] = k
                stage[2, dst, :] = v
            else:
                assert dil == DIL_STAGE * DIL_STAGE
                rows = pl.ds((r % DIL_STAGE) * (seq // DIL_STAGE) + r // DIL_STAGE, sub_len, stride=DIL_STAGE)
                q, k, v = stage[0, rows, :], stage[1, rows, :], stage[2, rows, :]
            q = q * scale
            for h in range(2):
                qs[h, dst, :] = (q * head_mask[h]).astype(BF16)
            ks[blk + r * sub_len:blk + (r + 1) * sub_len, :] = k.astype(BF16)
            vs[blk + r * sub_len:blk + (r + 1) * sub_len, :] = v.astype(BF16)

        def body(it, carry, bi=bi, dil=dil, nb=nb, nb_shift=nb_shift):
            grp = range(DIL_GROUP)
            g = [it * DIL_GROUP + gg for gg in grp]
            n = [gi & (nb - 1) for gi in g]
            g0 = [pl.multiple_of(gi * blk, blk) for gi in g]
            kb = [ks[pl.ds(s0, 2 * blk), :] for s0 in g0]
            sc = [[lax.dot_general(qs[h, pl.ds(g0[gg], blk), :], kb[gg], (((1,), (1,)), ((), ())),
                                   preferred_element_type=F32) + bias_ref[bi, h, jnp.where(n[gg] == 0, 1, 0)]
                   for h in range(2)] for gg in grp]
            stats = []
            for gg in grp:
                per_head = []
                for h in range(2):
                    m = jnp.max(sc[gg][h], axis=-1, keepdims=True)
                    p = jnp.exp2(sc[gg][h] - m)
                    l = jnp.sum(p, axis=-1, keepdims=True)
                    per_head.append((p.astype(BF16), m, l))
                stats.append(per_head)
            for gg in grp:
                vb = vs[pl.ds(g0[gg], 2 * blk), :]
                o_acc = None
                l_acc = None
                for h in range(2):
                    p, m, l = stats[gg][h]
                    o_h = jnp.dot(p, vb, preferred_element_type=F32) * (1.0 / l)
                    lse = jnp.broadcast_to(m + jnp.log2(l), (blk, LANES))
                    if h == 0:
                        o_acc, l_acc = o_h, lse
                    else:
                        o_acc = jnp.where(first, o_acc, o_h)
                        l_acc = jnp.where(first, l_acc, lse)
                if dil > 1:
                    dst = pl.ds(n[gg] * (blk * dil) + (g[gg] >> nb_shift), blk, stride=dil)
                else:
                    dst = pl.ds(g0[gg], blk)
                ob[bi, dst, :] = o_acc
                lb[bi, dst, :] = l_acc
            return carry

        lax.fori_loop(0, seq // blk // DIL_GROUP, body, 0)

    nbr = len(DIL_BRANCHES)
    mc = 256
    for c in range(seq // mc):
        rs = slice(c * mc, (c + 1) * mc)
        ls = [lb[i, rs, :] for i in range(nbr)]
        m = functools.reduce(jnp.maximum, ls)
        ws = [jnp.exp2(l - m) for l in ls]
        num = functools.reduce(lambda a, b: a + b, [w * ob[i, rs, :] for i, w in enumerate(ws)])
        den = functools.reduce(lambda a, b: a + b, ws)
        o_ref[0, rs, :] = num / den


def _dilated(u3, bias_tab):
    bsz, seq, _ = u3.shape
    npair = DIL_HEADS // 2
    nbr = len(DIL_BRANCHES)

    def col(c):
        return pl.BlockSpec((1, seq, LANES), lambda b, p, c=c: (b, 0, c // LANES + p))

    return pl.pallas_call(
        _dil_kernel,
        grid=(bsz, npair),
        in_specs=[col(COL_CQ), col(COL_CK), col(COL_CV),
                  pl.BlockSpec((nbr, 2, 2, DIL_BLOCK, 2 * DIL_BLOCK), lambda b, p: (0, p, 0, 0, 0))],
        out_specs=pl.BlockSpec((1, seq, LANES), lambda b, p: (b, 0, p)),
        out_shape=jax.ShapeDtypeStruct((bsz, seq, W_DIL), F32),
        scratch_shapes=[pltpu.VMEM((2, seq, LANES), BF16),
                        pltpu.VMEM((DIL_BLOCK + seq, LANES), BF16),
                        pltpu.VMEM((DIL_BLOCK + seq, LANES), BF16),
                        pltpu.VMEM((nbr, seq, LANES), F32),
                        pltpu.VMEM((nbr, seq, LANES), F32),
                        pltpu.VMEM((3, seq, LANES), F32)],
        compiler_params=_cparams(2),
        name="dilated_attn",
    )(u3, u3, u3, bias_tab)


TOK_TILE = 256
OUT_TILE = 512
OUT_SPLIT = 2
FINAL_SPLIT = 2
GATHER_SLOTS = 3
PLAN_TILE = 1024


def _outproj_kernel(x_ref, ya_ref, yb_ref, yc_ref, yd_ref, ms_ref, wo_ref, g_ref, b_ref, rw_ref, rb_ref,
                    x1_ref, x1b_ref, x1p_ref, ids_ref, gates_ref):
    tm = x_ref.shape[0]
    rows = tm // OUT_SPLIT
    parts = [slice(pi * rows, (pi + 1) * rows) for pi in range(OUT_SPLIT)]
    accs = []
    for rs in parts:
        acc = None
        for gi, y_ref in enumerate((ya_ref, yb_ref, yc_ref, yd_ref)):
            lo = gi * W_CONF
            y = (y_ref[rs, :] * ms_ref[0, :, lo:lo + W_CONF]).astype(BF16)
            part = jnp.dot(y, wo_ref[0, lo:lo + W_CONF, :], preferred_element_type=F32)
            acc = part if acc is None else acc + part
        accs.append(acc)

    logits = []
    for rs, acc in zip(parts, accs):
        x1 = _layer_norm(DN_ALPHA * x_ref[rs, :] + acc, g_ref[0], b_ref[0])
        x1_ref[rs, :] = x1
        x1_hi = x1.astype(BF16)
        x1b_ref[rs, :] = x1_hi
        _store_row_tiles(x1p_ref.at[pl.ds(rs.start * SUBLANES, rows * SUBLANES)], _pack_bf16_pairs(x1))
        x1_lo = (x1 - x1_hi.astype(F32)).astype(BF16)
        hh_hl = jnp.dot(x1_hi, rw_ref[...], preferred_element_type=F32)
        logits.append(hh_hl[:, :LANES] + hh_hl[:, LANES:]
                      + jnp.dot(x1_lo, rw_ref[:, :LANES], preferred_element_type=F32) + rb_ref[...])

    lane = lax.broadcasted_iota(I32, (rows, LANES), 1)
    neg = jnp.float32(-jnp.inf)
    for rs, lg in zip(parts, logits):
        gl = jnp.where(lane < N_GROUPS, lg, neg)
        gmax = jnp.max(gl, axis=-1, keepdims=True)
        g_top = 1.0 / jnp.sum(jnp.exp(gl - gmax), axis=-1, keepdims=True)
        g_idx = jnp.min(jnp.where(gl == gmax, lane, LANES), axis=-1, keepdims=True)
        e_lo = N_GROUPS + g_idx * EXPERTS_PER_GROUP
        el = jnp.where((lane >= e_lo) & (lane < e_lo + EXPERTS_PER_GROUP), lg, neg)
        m1 = jnp.max(el, axis=-1, keepdims=True)
        i1 = jnp.min(jnp.where(el == m1, lane, LANES), axis=-1, keepdims=True)
        el2 = jnp.where(lane == i1, neg, el)
        m2 = jnp.max(el2, axis=-1, keepdims=True)
        i2 = jnp.min(jnp.where(el2 == m2, lane, LANES), axis=-1, keepdims=True)
        e21 = jnp.exp(m2 - m1)
        w1 = g_top / (1.0 + e21)
        w2 = g_top * e21 / (1.0 + e21)
        ids_ref[rs, :] = jnp.where(lane == 0, i1 - N_GROUPS, jnp.where(lane == 1, i2 - N_GROUPS, N_EXPERTS))
        gates_ref[rs, :] = jnp.where(lane == 0, w1, jnp.where(lane == 1, w2, 0.0))


def _outproj(layer, x, ya, yb, yc, yd, mix_scale, w_out_bf, ln_g, ln_b, rw_cat, rb):
    t, d = x.shape
    tm = OUT_TILE

    def row(w):
        return pl.BlockSpec((tm, w), lambda i: (i, 0))

    def full(shape):
        return pl.BlockSpec(shape, lambda i: (0,) * len(shape))

    def per_layer(shape):
        return pl.BlockSpec((1,) + shape, lambda i: (layer,) + (0,) * len(shape))

    return pl.pallas_call(
        _outproj_kernel,
        grid=(t // tm,),
        in_specs=[row(d), row(W_CONF), row(W_GLA), row(W_DIL), row(W_SC), per_layer((1, d)), per_layer((d, d)),
                  per_layer((1, d)), per_layer((1, d)), full((d, 2 * LANES)), full((1, LANES))],
        out_specs=[row(d), row(d), pl.BlockSpec((tm * SUBLANES, LANES), lambda i: (i, 0)), row(LANES), row(LANES)],
        out_shape=[jax.ShapeDtypeStruct((t, d), F32), jax.ShapeDtypeStruct((t, d), BF16),
                   jax.ShapeDtypeStruct((t * SUBLANES, LANES), U32),
                   jax.ShapeDtypeStruct((t, LANES), I32), jax.ShapeDtypeStruct((t, LANES), F32)],
        compiler_params=_cparams(1),
        name="out_proj_ln_router",
    )(x, ya, yb, yc, yd, mix_scale, w_out_bf, ln_g, ln_b, rw_cat, rb)


def _plan_kernel(ids_ref, tril_ref, upper_ref, pos_ref, tile_ref, cnt, carry, start):
    phase = pl.program_id(0)
    i = pl.program_id(1)
    tm = ids_ref.shape[0]
    lane = lax.broadcasted_iota(I32, (tm, LANES), 1)
    ids = ids_ref[...]
    e1 = ids[:, 0:1]
    e2 = ids[:, 1:2]
    oh1 = (lane == e1).astype(F32)
    oh2 = (lane == e2).astype(F32)
    oh = oh1 + oh2

    @pl.when((phase == 0) & (i == 0))
    def _():
        cnt[...] = jnp.zeros(cnt.shape, F32)

    @pl.when(phase == 0)
    def _():
        cnt[...] = cnt[...] + jnp.sum(oh, axis=0, keepdims=True)

    @pl.when((phase == 1) & (i == 0))
    def _():
        padded = jnp.floor((cnt[...] + (MOE_TILE - 1)) * (1.0 / MOE_TILE)) * MOE_TILE
        p8 = jnp.broadcast_to(padded, (SUBLANES, LANES))
        st = jnp.dot(p8, upper_ref[...], precision=HIGHEST, preferred_element_type=F32)
        start[...] = st[0:1, :]
        carry[...] = jnp.zeros(carry.shape, F32)
        end = st[0:1, :] + padded
        trow = lax.broadcasted_iota(I32, (LANES, LANES), 0).astype(F32) * MOE_TILE
        tlane = lax.broadcasted_iota(I32, (LANES, LANES), 1)
        done = jnp.where((end <= trow) & (tlane < N_EXPERTS), 1.0, 0.0)
        tile_ref[...] = jnp.broadcast_to(jnp.sum(done, axis=-1, keepdims=True), (LANES, LANES)).astype(I32)

    @pl.when(phase == 1)
    def _():
        incl = jnp.dot(tril_ref[...], oh.astype(BF16), preferred_element_type=F32) + carry[...]
        base = start[...] + incl - oh
        p1 = jnp.sum(oh1 * base, axis=-1, keepdims=True)
        p2 = jnp.sum(oh2 * base, axis=-1, keepdims=True)
        pos_ref[...] = jnp.where(lane == 0, p1, jnp.where(lane == 1, p2, 0.0)).astype(I32)
        carry[...] = carry[...] + jnp.sum(oh, axis=0, keepdims=True)


def _plan(ids):
    t = ids.shape[0]
    tm = PLAN_TILE
    tril = jnp.tril(jnp.ones((tm, tm), BF16))
    upper = jnp.triu(jnp.ones((LANES, LANES), F32), k=1)
    return pl.pallas_call(
        _plan_kernel,
        grid=(2, t // tm),
        in_specs=[pl.BlockSpec((tm, LANES), lambda ph, i: (i, 0)),
                  pl.BlockSpec((tm, tm), lambda ph, i: (0, 0)),
                  pl.BlockSpec((LANES, LANES), lambda ph, i: (0, 0))],
        out_specs=[pl.BlockSpec((tm, LANES), lambda ph, i: (i * ph, 0)),
                   pl.BlockSpec((LANES, LANES), lambda ph, i: (0, 0))],
        out_shape=[jax.ShapeDtypeStruct((t, LANES), I32), jax.ShapeDtypeStruct((LANES, LANES), I32)],
        scratch_shapes=[pltpu.VMEM((1, LANES), F32), pltpu.VMEM((1, LANES), F32), pltpu.VMEM((1, LANES), F32)],
        compiler_params=_cparams(2),
        name="route_plan",
    )(ids, tril, upper)


def _dispatch_kernel(pos_ref, x_ref, xs_ref, sem):
    i = pl.program_id(0)
    tm = x_ref.shape[0] // SUBLANES
    t = pl.num_programs(0) * tm

    for m in range(tm):
        for k in range(TOP_K):
            dst = pl.multiple_of(pos_ref[k * t + i * tm + m] * SUBLANES, SUBLANES)
            pltpu.make_async_copy(x_ref.at[pl.ds(m * SUBLANES, SUBLANES)], xs_ref.at[pl.ds(dst, SUBLANES)],
                                  sem).start(priority=k)
    for _ in range(TOP_K):
        pltpu.make_async_copy(x_ref, xs_ref.at[pl.ds(0, tm * SUBLANES)], sem).wait()


def _dispatch(pos_flat, x1p, n_rows):
    t = x1p.shape[0] // SUBLANES
    tm = TOK_TILE
    return pl.pallas_call(
        _dispatch_kernel,
        grid_spec=pltpu.PrefetchScalarGridSpec(
            num_scalar_prefetch=1,
            grid=(t // tm,),
            in_specs=[pl.BlockSpec((tm * SUBLANES, LANES), lambda i, pos: (i, 0))],
            out_specs=pl.BlockSpec(memory_space=pl.ANY),
            scratch_shapes=[pltpu.SemaphoreType.DMA],
        ),
        out_shape=jax.ShapeDtypeStruct((n_rows * SUBLANES, LANES), x1p.dtype),
        compiler_params=_cparams(1),
        name="moe_dispatch",
    )(pos_flat, x1p)


def _expert_kernel(te_ref, na_ref, xs_ref, wg_ref, wu_ref, wd_ref, ys_ref):
    i = pl.program_id(0)

    @pl.when(i < na_ref[0])
    def _():
        xb = _unpack_bf16_pairs(_load_row_tiles(xs_ref, MOE_TILE)).astype(BF16)
        g = jnp.dot(xb, wg_ref[0, 0].astype(BF16), preferred_element_type=F32)
        u = jnp.dot(xb, wu_ref[0, 0].astype(BF16), preferred_element_type=F32)
        h = (_silu(g) * u).astype(BF16)
        y = jnp.dot(h, wd_ref[0, 0].astype(BF16), preferred_element_type=F32)
        _store_row_tiles(ys_ref, _pack_bf16_pairs(y))


def _experts(layer, tile_expert, n_active, xs, w_gate, w_up, w_down):
    n_rows = xs.shape[0] // SUBLANES
    d = LANES
    n_tiles = n_rows // MOE_TILE
    d_model = w_gate.shape[2]

    def rows(i, te, na):
        return (jnp.minimum(i, na[0] - 1), 0)

    def expert(i, te, na):
        return (layer, te[i], 0, 0)

    return pl.pallas_call(
        _expert_kernel,
        grid_spec=pltpu.PrefetchScalarGridSpec(
            num_scalar_prefetch=2,
            grid=(n_tiles,),
            in_specs=[pl.BlockSpec((MOE_TILE * SUBLANES, d), rows),
                      pl.BlockSpec((1, 1, d_model, D_EXPERT), expert),
                      pl.BlockSpec((1, 1, d_model, D_EXPERT), expert),
                      pl.BlockSpec((1, 1, D_EXPERT, d_model), expert)],
            out_specs=pl.BlockSpec((MOE_TILE * SUBLANES, d), rows),
        ),
        out_shape=jax.ShapeDtypeStruct((n_rows * SUBLANES, d), U32),
        compiler_params=_cparams(1),
        name="moe_experts",
    )(tile_expert, n_active, xs, w_gate, w_up, w_down)


def _final_kernel(pos_ref, x1_ref, x1b_ref, gates_ref, p_ref, wg_ref, bg_ref, wp_ref, g_ref, b_ref, ys_ref,
                  x2_ref, x2b_ref, ybuf, sem):
    i = pl.program_id(0)
    n_steps = pl.num_programs(0)
    tm = x1_ref.shape[0]
    t = n_steps * tm

    def gather(step, slot):
        for m in range(tm):
            for k in range(TOP_K):
                src = pl.multiple_of(pos_ref[k * t + step * tm + m] * SUBLANES, SUBLANES)
                pltpu.make_async_copy(ys_ref.at[pl.ds(src, SUBLANES)],
                                      ybuf.at[slot, k, pl.ds(m * SUBLANES, SUBLANES)],
                                      sem.at[slot]).start(priority=k)

    def wait_slot(slot):
        for k in range(TOP_K):
            pltpu.make_async_copy(ys_ref.at[pl.ds(0, tm * SUBLANES)], ybuf.at[slot, k], sem.at[slot]).wait()

    @pl.when(i == 0)
    def _():
        gather(0, 0)
        gather(1, 1)

    slot = lax.rem(i, GATHER_SLOTS)
    wait_slot(slot)
    gather(jnp.minimum(i + 2, n_steps - 1), lax.rem(i + 2, GATHER_SLOTS))

    rows = tm // FINAL_SPLIT
    parts = [slice(pi * rows, (pi + 1) * rows) for pi in range(FINAL_SPLIT)]
    ples = []
    for rs in parts:
        gate = _sigmoid(jnp.dot(x1b_ref[rs, :], wg_ref[0], preferred_element_type=F32) + bg_ref[0])
        ples.append(gate * jnp.dot(p_ref[0, rs, :].astype(BF16), wp_ref[0], preferred_element_type=F32))
    for rs, ple in zip(parts, ples):
        gates = gates_ref[rs, :]
        tiles = pl.ds(rs.start * SUBLANES, rows * SUBLANES)
        ffn = (gates[:, 0:1] * _unpack_bf16_pairs(_load_row_tiles(ybuf.at[slot, 0, tiles], rows))
               + gates[:, 1:2] * _unpack_bf16_pairs(_load_row_tiles(ybuf.at[slot, 1, tiles], rows)))
        x2 = _layer_norm(DN_ALPHA * x1_ref[rs, :] + ffn + ple, g_ref[0], b_ref[0])
        x2_ref[rs, :] = x2
        x2b_ref[rs, :] = x2.astype(BF16)

    @pl.when(i == n_steps - 1)
    def _():
        wait_slot(lax.rem(i + 1, GATHER_SLOTS))
        wait_slot(lax.rem(i + 2, GATHER_SLOTS))


def _final(layer, pos_flat, x1, x1b, gates, p, wg_bf, bg, wp_bf, ln_g, ln_b, ys):
    t, d = x1.shape
    tm = TOK_TILE

    def row(w):
        return pl.BlockSpec((tm, w), lambda i, pos: (i, 0))

    def per_layer(shape):
        return pl.BlockSpec((1,) + shape, lambda i, pos: (layer,) + (0,) * len(shape))

    return pl.pallas_call(
        _final_kernel,
        grid_spec=pltpu.PrefetchScalarGridSpec(
            num_scalar_prefetch=1,
            grid=(t // tm,),
            in_specs=[row(d), row(d), row(LANES), pl.BlockSpec((1, tm, PLE_DIM), lambda i, pos: (layer, i, 0)),
                      per_layer((d, d)), per_layer((1, d)), per_layer((PLE_DIM, d)),
                      per_layer((1, d)), per_layer((1, d)), pl.BlockSpec(memory_space=pl.ANY)],
            out_specs=[row(d), row(d)],
            scratch_shapes=[pltpu.VMEM((GATHER_SLOTS, TOP_K, tm * SUBLANES, LANES), U32),
                            pltpu.SemaphoreType.DMA((GATHER_SLOTS,))],
        ),
        out_shape=[jax.ShapeDtypeStruct((t, d), F32), jax.ShapeDtypeStruct((t, d), BF16)],
        compiler_params=_cparams(1),
        name="combine_ple_ln",
    )(pos_flat, x1, x1b, gates, p, wg_bf, bg, wp_bf, ln_g, ln_b, ys)


W_PREP_COLS = 512


def _arrange_kernel(wt_ref, o_ref):
    j = pl.program_id(1)
    wt = wt_ref[0]
    row = lax.broadcasted_iota(I32, (wt.shape[0], 1), 0)
    n_valid = jnp.where(j == COL_GL // W_PREP_COLS, GLA_RANK, W_PREP_COLS)
    o_ref[0] = jnp.where(row < n_valid, wt, 0.0).T.astype(BF16)


def _arrange_w_in(w):
    n_layer, k, n_in = w.shape
    assert sum(IN_SIZES[:5]) == COL_GL + GLA_RANK and n_in + GL_PAD - GLA_RANK == U_COLS
    assert COL_GL % W_PREP_COLS == 0 and GL_PAD == W_PREP_COLS
    gl_blk = COL_GL // W_PREP_COLS

    def src_row(l, j):
        row = jnp.where(j <= gl_blk, j * W_PREP_COLS, j * W_PREP_COLS - (GL_PAD - GLA_RANK))
        return (l, pl.multiple_of(row, GLA_RANK), 0)

    return pl.pallas_call(
        _arrange_kernel,
        grid=(n_layer, U_COLS // W_PREP_COLS),
        in_specs=[pl.BlockSpec((pl.Element(1), pl.Element(W_PREP_COLS), pl.Element(k)), src_row)],
        out_specs=pl.BlockSpec((1, k, W_PREP_COLS), lambda l, j: (l, 0, j)),
        out_shape=jax.ShapeDtypeStruct((n_layer, k, U_COLS), BF16),
        compiler_params=_cparams(2),
        name="arrange_w_in",
    )(jnp.swapaxes(w, 1, 2))


def kernel(x, p, w_in, conf_dw_w, conf_dw_b, conf_ln_g, conf_ln_b, gla_w_g2, gla_b_g2, gla_norm_g, sc_conv_w, mix_scale, w_out, rel_bias, ln1_g, ln1_b, router_g_w, router_g_b, router_e_w, router_e_b, exp_w_gate, exp_w_up, exp_w_down, ple_w_gate, ple_b_gate, ple_w_proj, ln2_g, ln2_b):
    bsz, seq, d = x.shape
    t = bsz * seq
    n_rows = TOP_K * t + N_EXPERTS * MOE_TILE
    n_tiles = n_rows // MOE_TILE
    bias_tab = _bias_tables(rel_bias)
    w_in_bf = _arrange_w_in(w_in)
    w_out_bf = w_out.astype(BF16)
    ple_wg_bf = ple_w_gate.astype(BF16)
    ple_wp_bf = ple_w_proj.astype(BF16)
    p3 = p.reshape(DEPTH, t, PLE_DIM)
    n_route = N_GROUPS + N_EXPERTS
    rw = jnp.concatenate([router_g_w, router_e_w, jnp.zeros((DEPTH, d, LANES - n_route), F32)], axis=2)
    rb = jnp.concatenate([router_g_b, router_e_b, jnp.zeros((DEPTH, LANES - n_route), F32)], axis=1)[:, None, :]
    rw_hi = rw.astype(BF16)
    rw_cat = jnp.concatenate([rw_hi, (rw - rw_hi.astype(F32)).astype(BF16)], axis=2)

    def vec(a):
        return a.reshape(DEPTH, 1, -1)

    xf = x.reshape(t, d)
    xb = xf
    for i in range(DEPTH):
        u = _in_proj(xb, w_in_bf, i, 512, 2048, F32)
        u3 = u.reshape(bsz, seq, U_COLS)
        ya, yd = _conv_mixers(u3, conf_dw_w[i], conf_dw_b[i], conf_ln_g[i], conf_ln_b[i], sc_conv_w[i])
        yb = _gla(u3, gla_w_g2[i], gla_b_g2[i], gla_norm_g[i])
        yc = _dilated(u3, bias_tab)
        x1, x1b, x1p, ids, gates = _outproj(i, xf, ya.reshape(t, -1), yb.reshape(t, -1), yc.reshape(t, -1),
                                            yd.reshape(t, -1), vec(mix_scale), w_out_bf, vec(ln1_g), vec(ln1_b),
                                            rw_cat[i], rb[i])
        pos, tile_tab = _plan(ids)
        pos_flat = pos[:, :TOP_K].T.reshape(-1)
        tile_end = tile_tab[:n_tiles, 0]
        n_active = jnp.sum((tile_end < N_EXPERTS).astype(I32)).reshape(1)
        tile_expert = jnp.minimum(tile_end, N_EXPERTS - 1)
        xs = _dispatch(pos_flat, x1p, n_rows)
        ys = _experts(i, tile_expert, n_active, xs, exp_w_gate, exp_w_up, exp_w_down)
        xf, xb = _final(i, pos_flat, x1, x1b, gates, p3, ple_wg_bf, vec(ple_b_gate), ple_wp_bf, vec(ln2_g), vec(ln2_b), ys)
    return xf.reshape(bsz, seq, d)
```

```python
import functools
import math

import jax
import jax.numpy as jnp
from jax import lax
from jax.experimental import pallas as pl
from jax.experimental.pallas import tpu as pltpu

F32 = jnp.float32
BF16 = jnp.bfloat16
I32 = jnp.int32
HIGHEST = lax.Precision.HIGHEST

D_MODEL = 2048
DEPTH = 2
PLE_DIM = 256
W_CONF = 512
W_GLA = 512
W_DIL = 512
W_SC = 512
CONF_KERNEL = 31
GLA_HEADS = 4
GLA_DV = 128
GLA_DK = 64
GLA_RANK = 16
GLA_TAU = 16.0
DIL_HEADS = 8
DIL_HD = 64
DIL_BRANCHES = ((128, 1), (512, 4), (2048, 16))
DIL_BLOCK = 128
REL_BUCKETS = 32
REL_MAX_DIST = 2048
SC_KERNEL = 3
N_GROUPS = 4
EXPERTS_PER_GROUP = 8
N_EXPERTS = 32
TOP_K = 2
D_EXPERT = 256
DN_ALPHA = (2 * DEPTH) ** 0.25
LN_EPS = 1e-5
RMS_EPS = 1e-6
IN_SIZES = (2 * W_CONF, GLA_HEADS * GLA_DK, GLA_HEADS * GLA_DK, W_GLA, GLA_RANK, W_GLA,
            W_DIL, W_DIL, W_DIL, W_SC, W_SC, W_SC)

LANES = 128
SUBLANES = 8
VMEM_LIMIT = 56 * 1024 * 1024

GL_PAD = 512
COL_A = 0
COL_GQ = 1024
COL_GK = 1280
COL_GV = 1536
COL_GL = 2048
COL_GR = 2560
COL_CQ = 3072
COL_CK = 3584
COL_CV = 4096
COL_SB = 4608
COL_SC = 5120
COL_SH = 5632
U_COLS = 6144

MOE_TILE = 512


def _cparams(n_axes):
    return pltpu.CompilerParams(dimension_semantics=("arbitrary",) * n_axes, vmem_limit_bytes=VMEM_LIMIT)


def _layer_norm(x, g, b):
    mu = jnp.mean(x, axis=-1, keepdims=True)
    xc = x - mu
    var = jnp.mean(xc * xc, axis=-1, keepdims=True)
    return xc * lax.rsqrt(var + LN_EPS) * g + b


U32 = jnp.uint32


def _pack_bf16_pairs(x):
    n = x.shape[1] // 2
    bits = lax.bitcast_convert_type(x.astype(BF16).astype(F32), U32)
    return (bits[:, :n] >> 16) | (bits[:, n:] & jnp.uint32(0xFFFF0000))


def _unpack_bf16_pairs(w):
    lo = lax.bitcast_convert_type(w << 16, F32)
    hi = lax.bitcast_convert_type(w & jnp.uint32(0xFFFF0000), F32)
    return jnp.concatenate([lo, hi], axis=1)


def _store_row_tiles(ref, packed):
    rows, width = packed.shape
    assert width == SUBLANES * LANES
    for s in range(SUBLANES):
        ref[pl.ds(s, rows, stride=SUBLANES), :] = packed[:, s * LANES:(s + 1) * LANES]


def _load_row_tiles(ref, rows):
    return jnp.concatenate([ref[pl.ds(s, rows, stride=SUBLANES), :] for s in range(SUBLANES)], axis=1)


def _sigmoid(x):
    return 1.0 / (1.0 + jnp.exp(-x))


def _silu(x):
    return x * _sigmoid(x)


def _matmul_kernel(x_ref, w_ref, o_ref):
    o_ref[...] = jnp.dot(x_ref[...].astype(BF16), w_ref[0],
                         preferred_element_type=F32).astype(o_ref.dtype)


def _in_proj(x, w_all, layer, tm, tn, out_dtype):
    m, k = x.shape
    n = w_all.shape[2]
    return pl.pallas_call(
        _matmul_kernel,
        grid=(n // tn, m // tm),
        in_specs=[pl.BlockSpec((tm, k), lambda j, i: (i, 0)),
                  pl.BlockSpec((1, k, tn), lambda j, i: (layer, 0, j))],
        out_specs=pl.BlockSpec((tm, tn), lambda j, i: (i, j)),
        out_shape=jax.ShapeDtypeStruct((m, n), out_dtype),
        compiler_params=_cparams(2),
        name="in_proj",
    )(x, w_all)


CONV_TS = 512
CONV_RC = 64
CONF_HIST = 32
SC_HIST = 8


def _conv_kernel(a_ref, gate_ref, sb_ref, sc_ref, sh_ref, cw_ref, cb_ref, lg_ref, lb_ref, sw_ref,
                 ya_ref, yd_ref, ha, hd, hs):
    s = pl.program_id(1)
    ts = a_ref.shape[1]
    n_ext = CONF_HIST + ts

    @pl.when(s == 0)
    def _():
        ha[0:CONF_HIST, :] = jnp.zeros((CONF_HIST, W_CONF), F32)
        hd[0:SC_HIST, :] = jnp.zeros((SC_HIST, W_SC), F32)

    ha[CONF_HIST:CONF_HIST + ts, :] = a_ref[0] * _sigmoid(gate_ref[0])
    hd[SC_HIST:SC_HIST + ts, :] = sc_ref[0] * sh_ref[0]

    cb = cb_ref[...]
    lg = lg_ref[...]
    lb = lb_ref[...]
    off_a = CONF_HIST - (CONF_KERNEL - 1)
    h_all = ha[...]
    for sh in range(1, SUBLANES):
        h_all = pltpu.roll(h_all, n_ext - 1, 0)
        hs[sh - 1] = h_all
    for c in range(ts // CONV_RC):
        r0 = c * CONV_RC
        acc = jnp.zeros((CONV_RC, W_CONF), F32)
        for j in range(CONF_KERNEL):
            sh = (off_a + j) % SUBLANES
            lo = r0 + off_a + j - sh
            rows = ha[lo:lo + CONV_RC, :] if sh == 0 else hs[sh - 1, lo:lo + CONV_RC, :]
            acc = acc + cw_ref[j:j + 1, :] * rows
        hh = _layer_norm(acc + cb, lg, lb)
        ya_ref[0, r0:r0 + CONV_RC, :] = _silu(hh)

    off_d = SC_HIST - (SC_KERNEL - 1)
    accd = jnp.zeros((ts, W_SC), F32)
    for j in range(SC_KERNEL):
        accd = accd + sw_ref[j:j + 1, :] * hd[off_d + j:off_d + j + ts, :]
    yd_ref[0] = sb_ref[0] * accd

    ha[0:CONF_HIST, :] = ha[ts:ts + CONF_HIST, :]
    hd[0:SC_HIST, :] = hd[ts:ts + SC_HIST, :]


def _conv_mixers(u3, conf_w, conf_b, conf_lg, conf_lb, sc_w):
    bsz, seq, _ = u3.shape
    ts = CONV_TS
    wblk = W_CONF

    def col(c):
        return pl.BlockSpec((1, ts, wblk), lambda b, s, c=c: (b, s, c // wblk))

    def full(shape):
        return pl.BlockSpec(shape, lambda b, s: (0,) * len(shape))

    cw = jnp.zeros((32, W_CONF), F32).at[:CONF_KERNEL].set(conf_w)
    sw = jnp.zeros((8, W_SC), F32).at[:SC_KERNEL].set(sc_w)
    out_spec = pl.BlockSpec((1, ts, wblk), lambda b, s: (b, s, 0))
    return pl.pallas_call(
        _conv_kernel,
        grid=(bsz, seq // ts),
        in_specs=[col(COL_A), col(COL_A + W_CONF), col(COL_SB), col(COL_SC), col(COL_SH),
                  full((32, W_CONF)), full((1, W_CONF)), full((1, W_CONF)), full((1, W_CONF)), full((8, W_SC))],
        out_specs=[out_spec, out_spec],
        out_shape=[jax.ShapeDtypeStruct((bsz, seq, W_CONF), F32), jax.ShapeDtypeStruct((bsz, seq, W_SC), F32)],
        scratch_shapes=[pltpu.VMEM((CONF_HIST + ts, W_CONF), F32), pltpu.VMEM((SC_HIST + ts, W_SC), F32),
                        pltpu.VMEM((SUBLANES - 1, CONF_HIST + ts, W_CONF), F32)],
        compiler_params=_cparams(2),
        name="conv_mixers",
    )(u3, u3, u3, u3, u3, cw, conf_b.reshape(1, -1), conf_lg.reshape(1, -1), conf_lb.reshape(1, -1), sw)


GLA_BLK = 128
GLA_SUB = 64
GLA_UNROLL = 4


def _split3(x):
    hi = x.astype(BF16)
    r1 = x - hi.astype(F32)
    mid = r1.astype(BF16)
    lo = (r1 - mid.astype(F32)).astype(BF16)
    return hi, mid, lo


def _gla_kernel(q_ref, k_ref, v_ref, gl_ref, r_ref, wg2h_ref, wg2l_ref, bg2_ref, ng_ref, tril_ref, o_ref, la_ref):
    seq = q_ref.shape[1]
    nblk = seq // GLA_BLK
    nsub = GLA_BLK // GLA_SUB
    assert nsub == 2 and nblk % GLA_UNROLL == 0

    lane = lax.broadcasted_iota(I32, (1, LANES), 1)
    head_mask = [(lane < GLA_DK).astype(F32), (lane >= GLA_DK).astype(F32)]
    row = lax.broadcasted_iota(I32, (GLA_BLK, GLA_BLK), 0)
    col = lax.broadcasted_iota(I32, (GLA_BLK, GLA_BLK), 1)
    sub_shift = GLA_SUB.bit_length() - 1
    diag_mask = ((row >> sub_shift) == (col >> sub_shift)) & (col <= row)
    second_half = lax.broadcasted_iota(I32, (GLA_BLK, 1), 0) >= GLA_SUB
    scale = GLA_DK ** -0.5
    ng = ng_ref[...]
    tril = tril_ref[...]

    gl = gl_ref[0]
    gl_hi = gl.astype(BF16)
    gl_lo = (gl - gl_hi.astype(F32)).astype(BF16)
    wh = wg2h_ref[...]
    z = (jnp.dot(gl_hi, wh, preferred_element_type=F32) + jnp.dot(gl_lo, wh, preferred_element_type=F32)
         + jnp.dot(gl_hi, wg2l_ref[...], preferred_element_type=F32) + bg2_ref[...])
    la_ref[...] = (jnp.minimum(z, 0.0) - jnp.log(1.0 + jnp.exp(-jnp.abs(z)))) * (1.0 / GLA_TAU)

    nt = (((1,), (1,)), ((), ()))

    def body(it, carry):
        grp = range(GLA_UNROLL)
        r0 = [pl.multiple_of((it * GLA_UNROLL + j) * GLA_BLK, GLA_BLK) for j in grp]
        q = [q_ref[0, pl.ds(r, GLA_BLK), :] * scale for r in r0]
        k = [k_ref[0, pl.ds(r, GLA_BLK), :] for r in r0]
        v = [v_ref[0, pl.ds(r, GLA_BLK), :].astype(BF16) for r in r0]
        b = []
        for r in r0:
            acc = None
            for part in _split3(la_ref[pl.ds(r, GLA_BLK), :]):
                t = jnp.dot(tril, part, preferred_element_type=F32)
                acc = t if acc is None else acc + t
            b.append(acc)

        s_in, q_st, kk, b_last = [], [], [], []
        for j in grp:
            b3 = b[j].reshape(nsub, GLA_SUB, LANES)
            mid = jnp.broadcast_to(b3[:, GLA_SUB // 2 - 1:GLA_SUB // 2, :], b3.shape).reshape(GLA_BLK, LANES)
            qd = q[j] * jnp.exp(b[j] - mid)
            kd = (k[j] * jnp.exp(mid - b[j])).astype(BF16)
            e0 = b[j][GLA_SUB - 1:GLA_SUB, :]
            q_off = jnp.where(second_half, q[j] * jnp.exp(jnp.minimum(b[j] - e0, 0.0)), 0.0)
            k_off = jnp.where(second_half, 0.0, k[j] * jnp.exp(jnp.minimum(e0 - b[j], 0.0))).astype(BF16)
            s_in.append((qd, kd, q_off, k_off))
            q_st.append(q[j] * jnp.exp(b[j]))
            b_last.append(b[j][GLA_BLK - 1:GLA_BLK, :])
            kk.append((k[j] * jnp.exp(b_last[j] - b[j])).astype(BF16))

        scores = []
        for j in grp:
            qd, kd, q_off, k_off = s_in[j]
            per_head = []
            for h in range(2):
                hm = head_mask[h]
                s_d = lax.dot_general((qd * hm).astype(BF16), kd, nt, preferred_element_type=F32)
                s_o = lax.dot_general((q_off * hm).astype(BF16), k_off, nt, preferred_element_type=F32)
                per_head.append((jnp.where(diag_mask, s_d, 0.0) + s_o).astype(BF16))
            scores.append(per_head)

        states = []
        st = carry
        for j in grp:
            states.append(st.astype(BF16))
            upd = lax.dot_general(kk[j], v[j], (((0,), (0,)), ((), ())), preferred_element_type=F32)
            decay = jnp.transpose(jnp.broadcast_to(jnp.exp(b_last[j]), (GLA_BLK, LANES)))
            st = st * jnp.concatenate([decay, decay], axis=1) + upd

        for j in grp:
            outs = []
            for h in range(2):
                lhs = jnp.concatenate([scores[j][h], (q_st[j] * head_mask[h]).astype(BF16)], axis=1)
                rhs = jnp.concatenate([v[j][:, h * GLA_DV:(h + 1) * GLA_DV],
                                       states[j][:, h * GLA_DV:(h + 1) * GLA_DV]], axis=0)
                o_h = jnp.dot(lhs, rhs, preferred_element_type=F32)
                outs.append(o_h * lax.rsqrt(jnp.mean(o_h * o_h, axis=-1, keepdims=True) + RMS_EPS))
            o = jnp.concatenate(outs, axis=1) * ng
            o_ref[0, pl.ds(r0[j], GLA_BLK), :] = o * _silu(r_ref[0, pl.ds(r0[j], GLA_BLK), :])
        return st

    lax.fori_loop(0, nblk // GLA_UNROLL, body, jnp.zeros((2 * GLA_DK, 2 * GLA_DV), F32))


def _gla(u3, w_g2, b_g2, norm_g):
    bsz, seq, _ = u3.shape
    npair = GLA_HEADS // 2
    kw = 2 * GLA_DK
    vw = 2 * GLA_DV

    def col(c, w):
        return pl.BlockSpec((1, seq, w), lambda b, p, c=c, w=w: (b, 0, c // w + p))

    wg2 = jnp.zeros((LANES, GLA_HEADS * GLA_DK), F32).at[:GLA_RANK].set(w_g2)
    wg2_hi = wg2.astype(BF16)
    wg2_lo = (wg2 - wg2_hi.astype(F32)).astype(BF16)
    ng = jnp.concatenate([norm_g, norm_g]).reshape(1, vw)
    tril = jnp.tril(jnp.ones((GLA_BLK, GLA_BLK), BF16))
    return pl.pallas_call(
        _gla_kernel,
        grid=(bsz, npair),
        in_specs=[col(COL_GQ, kw), col(COL_GK, kw), col(COL_GV, vw),
                  pl.BlockSpec((1, seq, LANES), lambda b, p: (b, 0, COL_GL // LANES)),
                  col(COL_GR, vw),
                  pl.BlockSpec((LANES, kw), lambda b, p: (0, p)),
                  pl.BlockSpec((LANES, kw), lambda b, p: (0, p)),
                  pl.BlockSpec((1, kw), lambda b, p: (0, p)),
                  pl.BlockSpec((1, vw), lambda b, p: (0, 0)),
                  pl.BlockSpec((GLA_BLK, GLA_BLK), lambda b, p: (0, 0))],
        out_specs=pl.BlockSpec((1, seq, vw), lambda b, p: (b, 0, p)),
        out_shape=jax.ShapeDtypeStruct((bsz, seq, W_GLA), F32),
        scratch_shapes=[pltpu.VMEM((seq, LANES), F32)],
        compiler_params=_cparams(2),
        name="gla",
    )(u3, u3, u3, u3, u3, wg2_hi, wg2_lo, b_g2.reshape(1, -1), ng, tril)


def _t5_bucket(dist):
    max_exact = REL_BUCKETS // 2
    large = max_exact + (jnp.log(jnp.maximum(dist, 1).astype(F32) / max_exact)
                         / math.log(REL_MAX_DIST / max_exact) * (REL_BUCKETS - max_exact)).astype(I32)
    large = jnp.minimum(large, REL_BUCKETS - 1)
    return jnp.where(dist < max_exact, dist, large)


LOG2E = math.log2(math.e)
DIL_STAGE = 4
DIL_GROUP = 8


def _bias_kernel(rel_ref, bucket_ref, o_ref):
    h = pl.program_id(1)
    blk = DIL_BLOCK
    bucket = bucket_ref[0]
    acc = jnp.zeros(bucket.shape, F32)
    for b in range(REL_BUCKETS):
        acc = jnp.where(bucket == b, rel_ref[b, h], acc)
    acc = acc * LOG2E
    qi = lax.broadcasted_iota(I32, (blk, 2 * blk), 0)
    kj = lax.broadcasted_iota(I32, (blk, 2 * blk), 1)
    steps = qi + blk - kj
    in_window = (steps >= 0) & (steps <= blk)
    neg = jnp.float32(-jnp.inf)
    o_ref[0, 0, 0] = jnp.where(in_window, acc, neg)
    o_ref[0, 0, 1] = jnp.where(in_window & (kj >= blk), acc, neg)


def _bias_tables(rel_bias):
    assert all(window // dil == DIL_BLOCK for window, dil in DIL_BRANCHES)
    qi = jnp.arange(DIL_BLOCK)[:, None]
    kj = jnp.arange(2 * DIL_BLOCK)[None, :]
    steps = jnp.maximum(qi + DIL_BLOCK - kj, 0)
    buckets = jnp.stack([_t5_bucket(steps * dil) for _, dil in DIL_BRANCHES]).astype(I32)
    nbr = len(DIL_BRANCHES)
    return pl.pallas_call(
        _bias_kernel,
        grid=(nbr, DIL_HEADS),
        in_specs=[pl.BlockSpec(memory_space=pltpu.SMEM),
                  pl.BlockSpec((1, DIL_BLOCK, 2 * DIL_BLOCK), lambda i, h: (i, 0, 0))],
        out_specs=pl.BlockSpec((1, 1, 2, DIL_BLOCK, 2 * DIL_BLOCK), lambda i, h: (i, h, 0, 0, 0)),
        out_shape=jax.ShapeDtypeStruct((nbr, DIL_HEADS, 2, DIL_BLOCK, 2 * DIL_BLOCK), F32),
        compiler_params=_cparams(2),
        name="rel_bias_tables",
    )(rel_bias, buckets)


def _dil_kernel(q_ref, k_ref, v_ref, bias_ref, o_ref, qs, ks, vs, ob, lb, stage):
    seq = q_ref.shape[1]
    blk = DIL_BLOCK
    scale = DIL_HD ** -0.5 * LOG2E
    lane = lax.broadcasted_iota(I32, (1, LANES), 1)
    first = lane < DIL_HD
    head_mask = [first.astype(F32), 1.0 - first.astype(F32)]

    ks[0:blk, :] = jnp.zeros((blk, LANES), BF16)
    vs[0:blk, :] = jnp.zeros((blk, LANES), BF16)

    for bi, (window, dil) in enumerate(DIL_BRANCHES):
        sub_len = seq // dil
        nb = sub_len // blk
        nb_shift = nb.bit_length() - 1
        for r in range(dil):
            dst = slice(r * sub_len, (r + 1) * sub_len)
            if dil == 1:
                q, k, v = q_ref[0], k_ref[0], v_ref[0]
            elif dil == DIL_STAGE:
                rows = pl.ds(r, sub_len, stride=dil)
                q, k, v = q_ref[0, rows, :], k_ref[0, rows, :], v_ref[0, rows, :]
                stage[0, dst, :] = q
                stage[1, dst, :] = k
                stage[2, dst, :] = v
            else:
                assert dil == DIL_STAGE * DIL_STAGE
                rows = pl.ds((r % DIL_STAGE) * (seq // DIL_STAGE) + r // DIL_STAGE, sub_len, stride=DIL_STAGE)
                q, k, v = stage[0, rows, :], stage[1, rows, :], stage[2, rows, :]
            q = q * scale
            for h in range(2):
                qs[h, dst, :] = (q * head_mask[h]).astype(BF16)
            ks[blk + r * sub_len:blk + (r + 1) * sub_len, :] = k.astype(BF16)
            vs[blk + r * sub_len:blk + (r + 1) * sub_len, :] = v.astype(BF16)

        def body(it, carry, bi=bi, dil=dil, nb=nb, nb_shift=nb_shift):
            grp = range(DIL_GROUP)
            g = [it * DIL_GROUP + gg for gg in grp]
            n = [gi & (nb - 1) for gi in g]
            g0 = [pl.multiple_of(gi * blk, blk) for gi in g]
            kb = [ks[pl.ds(s0, 2 * blk), :] for s0 in g0]
            sc = [[lax.dot_general(qs[h, pl.ds(g0[gg], blk), :], kb[gg], (((1,), (1,)), ((), ())),
                                   preferred_element_type=F32) + bias_ref[bi, h, jnp.where(n[gg] == 0, 1, 0)]
                   for h in range(2)] for gg in grp]
            stats = []
            for gg in grp:
                per_head = []
                for h in range(2):
                    m = jnp.max(sc[gg][h], axis=-1, keepdims=True)
                    p = jnp.exp2(sc[gg][h] - m)
                    l = jnp.sum(p, axis=-1, keepdims=True)
                    per_head.append((p.astype(BF16), m, l))
                stats.append(per_head)
            for gg in grp:
                vb = vs[pl.ds(g0[gg], 2 * blk), :]
                o_acc = None
                l_acc = None
                for h in range(2):
                    p, m, l = stats[gg][h]
                    o_h = jnp.dot(p, vb, preferred_element_type=F32) * (1.0 / l)
                    lse = jnp.broadcast_to(m + jnp.log2(l), (blk, LANES))
                    if h == 0:
                        o_acc, l_acc = o_h, lse
                    else:
                        o_acc = jnp.where(first, o_acc, o_h)
                        l_acc = jnp.where(first, l_acc, lse)
                if dil > 1:
                    dst = pl.ds(n[gg] * (blk * dil) + (g[gg] >> nb_shift), blk, stride=dil)
                else:
                    dst = pl.ds(g0[gg], blk)
                ob[bi, dst, :] = o_acc
                lb[bi, dst, :] = l_acc
            return carry

        lax.fori_loop(0, seq // blk // DIL_GROUP, body, 0)

    nbr = len(DIL_BRANCHES)
    mc = 256
    for c in range(seq // mc):
        rs = slice(c * mc, (c + 1) * mc)
        ls = [lb[i, rs, :] for i in range(nbr)]
        m = functools.reduce(jnp.maximum, ls)
        ws = [jnp.exp2(l - m) for l in ls]
        num = functools.reduce(lambda a, b: a + b, [w * ob[i, rs, :] for i, w in enumerate(ws)])
        den = functools.reduce(lambda a, b: a + b, ws)
        o_ref[0, rs, :] = num / den


def _dilated(u3, bias_tab):
    bsz, seq, _ = u3.shape
    npair = DIL_HEADS // 2
    nbr = len(DIL_BRANCHES)

    def col(c):
        return pl.BlockSpec((1, seq, LANES), lambda b, p, c=c: (b, 0, c // LANES + p))

    return pl.pallas_call(
        _dil_kernel,
        grid=(bsz, npair),
        in_specs=[col(COL_CQ), col(COL_CK), col(COL_CV),
                  pl.BlockSpec((nbr, 2, 2, DIL_BLOCK, 2 * DIL_BLOCK), lambda b, p: (0, p, 0, 0, 0))],
        out_specs=pl.BlockSpec((1, seq, LANES), lambda b, p: (b, 0, p)),
        out_shape=jax.ShapeDtypeStruct((bsz, seq, W_DIL), F32),
        scratch_shapes=[pltpu.VMEM((2, seq, LANES), BF16),
                        pltpu.VMEM((DIL_BLOCK + seq, LANES), BF16),
                        pltpu.VMEM((DIL_BLOCK + seq, LANES), BF16),
                        pltpu.VMEM((nbr, seq, LANES), F32),
                        pltpu.VMEM((nbr, seq, LANES), F32),
                        pltpu.VMEM((3, seq, LANES), F32)],
        compiler_params=_cparams(2),
        name="dilated_attn",
    )(u3, u3, u3, bias_tab)


TOK_TILE = 256
OUT_TILE = 512
OUT_SPLIT = 2
FINAL_SPLIT = 2
GATHER_SLOTS = 3
PLAN_TILE = 1024


def _outproj_kernel(x_ref, ya_ref, yb_ref, yc_ref, yd_ref, ms_ref, wo_ref, g_ref, b_ref, rw_ref, rb_ref,
                    x1_ref, x1b_ref, x1p_ref, ids_ref, gates_ref):
    tm = x_ref.shape[0]
    rows = tm // OUT_SPLIT
    parts = [slice(pi * rows, (pi + 1) * rows) for pi in range(OUT_SPLIT)]
    accs = []
    for rs in parts:
        acc = None
        for gi, y_ref in enumerate((ya_ref, yb_ref, yc_ref, yd_ref)):
            lo = gi * W_CONF
            y = (y_ref[rs, :] * ms_ref[0, :, lo:lo + W_CONF]).astype(BF16)
            part = jnp.dot(y, wo_ref[0, lo:lo + W_CONF, :], preferred_element_type=F32)
            acc = part if acc is None else acc + part
        accs.append(acc)

    logits = []
    for rs, acc in zip(parts, accs):
        x1 = _layer_norm(DN_ALPHA * x_ref[rs, :] + acc, g_ref[0], b_ref[0])
        x1_ref[rs, :] = x1
        x1_hi = x1.astype(BF16)
        x1b_ref[rs, :] = x1_hi
        _store_row_tiles(x1p_ref.at[pl.ds(rs.start * SUBLANES, rows * SUBLANES)], _pack_bf16_pairs(x1))
        x1_lo = (x1 - x1_hi.astype(F32)).astype(BF16)
        hh_hl = jnp.dot(x1_hi, rw_ref[...], preferred_element_type=F32)
        logits.append(hh_hl[:, :LANES] + hh_hl[:, LANES:]
                      + jnp.dot(x1_lo, rw_ref[:, :LANES], preferred_element_type=F32) + rb_ref[...])

    lane = lax.broadcasted_iota(I32, (rows, LANES), 1)
    neg = jnp.float32(-jnp.inf)
    for rs, lg in zip(parts, logits):
        gl = jnp.where(lane < N_GROUPS, lg, neg)
        gmax = jnp.max(gl, axis=-1, keepdims=True)
        g_top = 1.0 / jnp.sum(jnp.exp(gl - gmax), axis=-1, keepdims=True)
        g_idx = jnp.min(jnp.where(gl == gmax, lane, LANES), axis=-1, keepdims=True)
        e_lo = N_GROUPS + g_idx * EXPERTS_PER_GROUP
        el = jnp.where((lane >= e_lo) & (lane < e_lo + EXPERTS_PER_GROUP), lg, neg)
        m1 = jnp.max(el, axis=-1, keepdims=True)
        i1 = jnp.min(jnp.where(el == m1, lane, LANES), axis=-1, keepdims=True)
        el2 = jnp.where(lane == i1, neg, el)
        m2 = jnp.max(el2, axis=-1, keepdims=True)
        i2 = jnp.min(jnp.where(el2 == m2, lane, LANES), axis=-1, keepdims=True)
        e21 = jnp.exp(m2 - m1)
        w1 = g_top / (1.0 + e21)
        w2 = g_top * e21 / (1.0 + e21)
        ids_ref[rs, :] = jnp.where(lane == 0, i1 - N_GROUPS, jnp.where(lane == 1, i2 - N_GROUPS, N_EXPERTS))
        gates_ref[rs, :] = jnp.where(lane == 0, w1, jnp.where(lane == 1, w2, 0.0))


def _outproj(layer, x, ya, yb, yc, yd, mix_scale, w_out_bf, ln_g, ln_b, rw_cat, rb):
    t, d = x.shape
    tm = OUT_TILE

    def row(w):
        return pl.BlockSpec((tm, w), lambda i: (i, 0))

    def full(shape):
        return pl.BlockSpec(shape, lambda i: (0,) * len(shape))

    def per_layer(shape):
        return pl.BlockSpec((1,) + shape, lambda i: (layer,) + (0,) * len(shape))

    return pl.pallas_call(
        _outproj_kernel,
        grid=(t // tm,),
        in_specs=[row(d), row(W_CONF), row(W_GLA), row(W_DIL), row(W_SC), per_layer((1, d)), per_layer((d, d)),
                  per_layer((1, d)), per_layer((1, d)), full((d, 2 * LANES)), full((1, LANES))],
        out_specs=[row(d), row(d), pl.BlockSpec((tm * SUBLANES, LANES), lambda i: (i, 0)), row(LANES), row(LANES)],
        out_shape=[jax.ShapeDtypeStruct((t, d), F32), jax.ShapeDtypeStruct((t, d), BF16),
                   jax.ShapeDtypeStruct((t * SUBLANES, LANES), U32),
                   jax.ShapeDtypeStruct((t, LANES), I32), jax.ShapeDtypeStruct((t, LANES), F32)],
        compiler_params=_cparams(1),
        name="out_proj_ln_router",
    )(x, ya, yb, yc, yd, mix_scale, w_out_bf, ln_g, ln_b, rw_cat, rb)


def _plan_kernel(ids_ref, tril_ref, upper_ref, pos_ref, tile_ref, cnt, carry, start):
    phase = pl.program_id(0)
    i = pl.program_id(1)
    tm = ids_ref.shape[0]
    lane = lax.broadcasted_iota(I32, (tm, LANES), 1)
    ids = ids_ref[...]
    e1 = ids[:, 0:1]
    e2 = ids[:, 1:2]
    oh1 = (lane == e1).astype(F32)
    oh2 = (lane == e2).astype(F32)
    oh = oh1 + oh2

    @pl.when((phase == 0) & (i == 0))
    def _():
        cnt[...] = jnp.zeros(cnt.shape, F32)

    @pl.when(phase == 0)
    def _():
        cnt[...] = cnt[...] + jnp.sum(oh, axis=0, keepdims=True)

    @pl.when((phase == 1) & (i == 0))
    def _():
        padded = jnp.floor((cnt[...] + (MOE_TILE - 1)) * (1.0 / MOE_TILE)) * MOE_TILE
        p8 = jnp.broadcast_to(padded, (SUBLANES, LANES))
        st = jnp.dot(p8, upper_ref[...], precision=HIGHEST, preferred_element_type=F32)
        start[...] = st[0:1, :]
        carry[...] = jnp.zeros(carry.shape, F32)
        end = st[0:1, :] + padded
        trow = lax.broadcasted_iota(I32, (LANES, LANES), 0).astype(F32) * MOE_TILE
        tlane = lax.broadcasted_iota(I32, (LANES, LANES), 1)
        done = jnp.where((end <= trow) & (tlane < N_EXPERTS), 1.0, 0.0)
        tile_ref[...] = jnp.broadcast_to(jnp.sum(done, axis=-1, keepdims=True), (LANES, LANES)).astype(I32)

    @pl.when(phase == 1)
    def _():
        incl = jnp.dot(tril_ref[...], oh.astype(BF16), preferred_element_type=F32) + carry[...]
        base = start[...] + incl - oh
        p1 = jnp.sum(oh1 * base, axis=-1, keepdims=True)
        p2 = jnp.sum(oh2 * base, axis=-1, keepdims=True)
        pos_ref[...] = jnp.where(lane == 0, p1, jnp.where(lane == 1, p2, 0.0)).astype(I32)
        carry[...] = carry[...] + jnp.sum(oh, axis=0, keepdims=True)


def _plan(ids):
    t = ids.shape[0]
    tm = PLAN_TILE
    tril = jnp.tril(jnp.ones((tm, tm), BF16))
    upper = jnp.triu(jnp.ones((LANES, LANES), F32), k=1)
    return pl.pallas_call(
        _plan_kernel,
        grid=(2, t // tm),
        in_specs=[pl.BlockSpec((tm, LANES), lambda ph, i: (i, 0)),
                  pl.BlockSpec((tm, tm), lambda ph, i: (0, 0)),
                  pl.BlockSpec((LANES, LANES), lambda ph, i: (0, 0))],
        out_specs=[pl.BlockSpec((tm, LANES), lambda ph, i: (i * ph, 0)),
                   pl.BlockSpec((LANES, LANES), lambda ph, i: (0, 0))],
        out_shape=[jax.ShapeDtypeStruct((t, LANES), I32), jax.ShapeDtypeStruct((LANES, LANES), I32)],
        scratch_shapes=[pltpu.VMEM((1, LANES), F32), pltpu.VMEM((1, LANES), F32), pltpu.VMEM((1, LANES), F32)],
        compiler_params=_cparams(2),
        name="route_plan",
    )(ids, tril, upper)


DISPATCH_SLOTS = 3


def _dispatch_kernel(pos_ref, x_ref, xs_ref, stage, sem):
    i = pl.program_id(0)
    n_steps = pl.num_programs(0)
    tm = x_ref.shape[0] // SUBLANES
    t = n_steps * tm
    slot = lax.rem(i, DISPATCH_SLOTS)

    def wait_slot(s):
        for _ in range(TOP_K):
            pltpu.make_async_copy(stage.at[s], xs_ref.at[pl.ds(0, tm * SUBLANES)], sem.at[s]).wait()

    @pl.when(i >= DISPATCH_SLOTS)
    def _():
        wait_slot(slot)

    stage[slot] = x_ref[...]
    for m in range(tm):
        for k in range(TOP_K):
            dst = pl.multiple_of(pos_ref[k * t + i * tm + m] * SUBLANES, SUBLANES)
            pltpu.make_async_copy(stage.at[slot, pl.ds(m * SUBLANES, SUBLANES)], xs_ref.at[pl.ds(dst, SUBLANES)],
                                  sem.at[slot]).start(priority=k)

    @pl.when(i == n_steps - 1)
    def _():
        for s in range(DISPATCH_SLOTS):
            wait_slot(s)


def _dispatch(pos_flat, x1p, n_rows):
    t = x1p.shape[0] // SUBLANES
    tm = TOK_TILE
    assert t // tm >= DISPATCH_SLOTS
    return pl.pallas_call(
        _dispatch_kernel,
        grid_spec=pltpu.PrefetchScalarGridSpec(
            num_scalar_prefetch=1,
            grid=(t // tm,),
            in_specs=[pl.BlockSpec((tm * SUBLANES, LANES), lambda i, pos: (i, 0))],
            out_specs=pl.BlockSpec(memory_space=pl.ANY),
            scratch_shapes=[pltpu.VMEM((DISPATCH_SLOTS, tm * SUBLANES, LANES), x1p.dtype),
                            pltpu.SemaphoreType.DMA((DISPATCH_SLOTS,))],
        ),
        out_shape=jax.ShapeDtypeStruct((n_rows * SUBLANES, LANES), x1p.dtype),
        compiler_params=_cparams(1),
        name="moe_dispatch",
    )(pos_flat, x1p)


def _expert_kernel(layer, te_ref, na_ref, first_ref, run_ref, next2_ref, runs_ref, nruns_ref,
                   xs_ref, wg_hbm, wu_hbm, wd_hbm, ys_ref, wg_buf, wu_buf, wd_buf, wg_bf, wu_bf, wd_bf, sem):
    i = pl.program_id(0)

    def copies(e, s):
        return (pltpu.make_async_copy(wg_hbm.at[layer, e], wg_buf.at[s], sem.at[s]),
                pltpu.make_async_copy(wu_hbm.at[layer, e], wu_buf.at[s], sem.at[s]),
                pltpu.make_async_copy(wd_hbm.at[layer, e], wd_buf.at[s], sem.at[s]))

    def fetch(e, s):
        for c in copies(e, s):
            c.start()

    @pl.when(i == 0)
    def _():
        fetch(runs_ref[0], 0)

        @pl.when(nruns_ref[0] > 1)
        def _():
            fetch(runs_ref[1], 1)

    active = i < na_ref[0]

    @pl.when(active & (first_ref[i] == 1))
    def _():
        s = run_ref[i] % 2
        for c in copies(te_ref[i], s):
            c.wait()
        wg_bf[...] = wg_buf[s].astype(BF16)
        wu_bf[...] = wu_buf[s].astype(BF16)
        wd_bf[...] = wd_buf[s].astype(BF16)

        @pl.when(next2_ref[i] >= 0)
        def _():
            fetch(next2_ref[i], s)

    @pl.when(active)
    def _():
        xb = _unpack_bf16_pairs(_load_row_tiles(xs_ref, MOE_TILE)).astype(BF16)
        g = jnp.dot(xb, wg_bf[...], preferred_element_type=F32)
        u = jnp.dot(xb, wu_bf[...], preferred_element_type=F32)
        h = (_silu(g) * u).astype(BF16)
        y = jnp.dot(h, wd_bf[...], preferred_element_type=F32)
        _store_row_tiles(ys_ref, _pack_bf16_pairs(y))


def _experts(layer, tile_expert, n_active, xs, w_gate, w_up, w_down):
    n_rows = xs.shape[0] // SUBLANES
    d = LANES
    n_tiles = n_rows // MOE_TILE
    d_model = w_gate.shape[2]

    tile = jnp.arange(n_tiles, dtype=I32)
    active = tile < n_active[0]
    first = active & ((tile == 0) | (tile_expert != jnp.roll(tile_expert, 1)))
    run_of_tile = jnp.cumsum(first.astype(I32)) - 1
    n_runs = jnp.sum(first.astype(I32)).reshape(1)
    is_run_start = first[None, :] & (run_of_tile[None, :] == tile[:, None])
    run_expert = jnp.sum(jnp.where(is_run_start, tile_expert[None, :], 0), axis=1).astype(I32)
    nxt = run_of_tile + 2
    next2 = jnp.where(nxt < n_runs[0], run_expert[jnp.minimum(nxt, n_tiles - 1)], -1).astype(I32)
    first = first.astype(I32)

    def rows(i, te, na, *_):
        return (jnp.minimum(i, na[0] - 1), 0)

    any_spec = pl.BlockSpec(memory_space=pl.ANY)
    return pl.pallas_call(
        functools.partial(_expert_kernel, layer),
        grid_spec=pltpu.PrefetchScalarGridSpec(
            num_scalar_prefetch=7,
            grid=(n_tiles,),
            in_specs=[pl.BlockSpec((MOE_TILE * SUBLANES, d), rows), any_spec, any_spec, any_spec],
            out_specs=pl.BlockSpec((MOE_TILE * SUBLANES, d), rows),
            scratch_shapes=[pltpu.VMEM((2, d_model, D_EXPERT), F32), pltpu.VMEM((2, d_model, D_EXPERT), F32),
                            pltpu.VMEM((2, D_EXPERT, d_model), F32),
                            pltpu.VMEM((d_model, D_EXPERT), BF16), pltpu.VMEM((d_model, D_EXPERT), BF16),
                            pltpu.VMEM((D_EXPERT, d_model), BF16),
                            pltpu.SemaphoreType.DMA((2,))],
        ),
        out_shape=jax.ShapeDtypeStruct((n_rows * SUBLANES, d), U32),
        compiler_params=_cparams(1),
        name="moe_experts",
    )(tile_expert, n_active, first, run_of_tile, next2, run_expert, n_runs, xs, w_gate, w_up, w_down)


def _final_kernel(pos_ref, x1_ref, x1b_ref, gates_ref, p_ref, wg_ref, bg_ref, wp_ref, g_ref, b_ref, ys_ref,
                  x2_ref, x2b_ref, ybuf, sem):
    i = pl.program_id(0)
    n_steps = pl.num_programs(0)
    tm = x1_ref.shape[0]
    t = n_steps * tm

    def gather(step, slot):
        for m in range(tm):
            for k in range(TOP_K):
                src = pl.multiple_of(pos_ref[k * t + step * tm + m] * SUBLANES, SUBLANES)
                pltpu.make_async_copy(ys_ref.at[pl.ds(src, SUBLANES)],
                                      ybuf.at[slot, k, pl.ds(m * SUBLANES, SUBLANES)],
                                      sem.at[slot]).start(priority=k)

    def wait_slot(slot):
        for k in range(TOP_K):
            pltpu.make_async_copy(ys_ref.at[pl.ds(0, tm * SUBLANES)], ybuf.at[slot, k], sem.at[slot]).wait()

    @pl.when(i == 0)
    def _():
        gather(0, 0)
        gather(1, 1)

    slot = lax.rem(i, GATHER_SLOTS)
    wait_slot(slot)
    gather(jnp.minimum(i + 2, n_steps - 1), lax.rem(i + 2, GATHER_SLOTS))

    rows = tm // FINAL_SPLIT
    parts = [slice(pi * rows, (pi + 1) * rows) for pi in range(FINAL_SPLIT)]
    ples = []
    for rs in parts:
        gate = _sigmoid(jnp.dot(x1b_ref[rs, :], wg_ref[0], preferred_element_type=F32) + bg_ref[0])
        ples.append(gate * jnp.dot(p_ref[0, rs, :].astype(BF16), wp_ref[0], preferred_element_type=F32))
    for rs, ple in zip(parts, ples):
        gates = gates_ref[rs, :]
        tiles = pl.ds(rs.start * SUBLANES, rows * SUBLANES)
        ffn = (gates[:, 0:1] * _unpack_bf16_pairs(_load_row_tiles(ybuf.at[slot, 0, tiles], rows))
               + gates[:, 1:2] * _unpack_bf16_pairs(_load_row_tiles(ybuf.at[slot, 1, tiles], rows)))
        x2 = _layer_norm(DN_ALPHA * x1_ref[rs, :] + ffn + ple, g_ref[0], b_ref[0])
        x2_ref[rs, :] = x2
        x2b_ref[rs, :] = x2.astype(BF16)

    @pl.when(i == n_steps - 1)
    def _():
        wait_slot(lax.rem(i + 1, GATHER_SLOTS))
        wait_slot(lax.rem(i + 2, GATHER_SLOTS))


def _final(layer, pos_flat, x1, x1b, gates, p, wg_bf, bg, wp_bf, ln_g, ln_b, ys):
    t, d = x1.shape
    tm = TOK_TILE

    def row(w):
        return pl.BlockSpec((tm, w), lambda i, pos: (i, 0))

    def per_layer(shape):
        return pl.BlockSpec((1,) + shape, lambda i, pos: (layer,) + (0,) * len(shape))

    return pl.pallas_call(
        _final_kernel,
        grid_spec=pltpu.PrefetchScalarGridSpec(
            num_scalar_prefetch=1,
            grid=(t // tm,),
            in_specs=[row(d), row(d), row(LANES), pl.BlockSpec((1, tm, PLE_DIM), lambda i, pos: (layer, i, 0)),
                      per_layer((d, d)), per_layer((1, d)), per_layer((PLE_DIM, d)),
                      per_layer((1, d)), per_layer((1, d)), pl.BlockSpec(memory_space=pl.ANY)],
            out_specs=[row(d), row(d)],
            scratch_shapes=[pltpu.VMEM((GATHER_SLOTS, TOP_K, tm * SUBLANES, LANES), U32),
                            pltpu.SemaphoreType.DMA((GATHER_SLOTS,))],
        ),
        out_shape=[jax.ShapeDtypeStruct((t, d), F32), jax.ShapeDtypeStruct((t, d), BF16)],
        compiler_params=_cparams(1),
        name="combine_ple_ln",
    )(pos_flat, x1, x1b, gates, p, wg_bf, bg, wp_bf, ln_g, ln_b, ys)


W_PREP_COLS = 512


def _arrange_kernel(wt_ref, o_ref):
    j = pl.program_id(1)
    wt = wt_ref[0]
    row = lax.broadcasted_iota(I32, (wt.shape[0], 1), 0)
    n_valid = jnp.where(j == COL_GL // W_PREP_COLS, GLA_RANK, W_PREP_COLS)
    o_ref[0] = jnp.where(row < n_valid, wt, 0.0).T.astype(BF16)


def _arrange_w_in(w):
    n_layer, k, n_in = w.shape
    assert sum(IN_SIZES[:5]) == COL_GL + GLA_RANK and n_in + GL_PAD - GLA_RANK == U_COLS
    assert COL_GL % W_PREP_COLS == 0 and GL_PAD == W_PREP_COLS
    gl_blk = COL_GL // W_PREP_COLS

    def src_row(l, j):
        row = jnp.where(j <= gl_blk, j * W_PREP_COLS, j * W_PREP_COLS - (GL_PAD - GLA_RANK))
        return (l, pl.multiple_of(row, GLA_RANK), 0)

    return pl.pallas_call(
        _arrange_kernel,
        grid=(n_layer, U_COLS // W_PREP_COLS),
        in_specs=[pl.BlockSpec((pl.Element(1), pl.Element(W_PREP_COLS), pl.Element(k)), src_row)],
        out_specs=pl.BlockSpec((1, k, W_PREP_COLS), lambda l, j: (l, 0, j)),
        out_shape=jax.ShapeDtypeStruct((n_layer, k, U_COLS), BF16),
        compiler_params=_cparams(2),
        name="arrange_w_in",
    )(jnp.swapaxes(w, 1, 2))


def kernel(x, p, w_in, conf_dw_w, conf_dw_b, conf_ln_g, conf_ln_b, gla_w_g2, gla_b_g2, gla_norm_g, sc_conv_w, mix_scale, w_out, rel_bias, ln1_g, ln1_b, router_g_w, router_g_b, router_e_w, router_e_b, exp_w_gate, exp_w_up, exp_w_down, ple_w_gate, ple_b_gate, ple_w_proj, ln2_g, ln2_b):
    bsz, seq, d = x.shape
    t = bsz * seq
    n_rows = TOP_K * t + N_EXPERTS * MOE_TILE
    n_tiles = n_rows // MOE_TILE
    bias_tab = _bias_tables(rel_bias)
    w_in_bf = _arrange_w_in(w_in)
    w_out_bf = w_out.astype(BF16)
    ple_wg_bf = ple_w_gate.astype(BF16)
    ple_wp_bf = ple_w_proj.astype(BF16)
    p3 = p.reshape(DEPTH, t, PLE_DIM)
    n_route = N_GROUPS + N_EXPERTS
    rw = jnp.concatenate([router_g_w, router_e_w, jnp.zeros((DEPTH, d, LANES - n_route), F32)], axis=2)
    rb = jnp.concatenate([router_g_b, router_e_b, jnp.zeros((DEPTH, LANES - n_route), F32)], axis=1)[:, None, :]
    rw_hi = rw.astype(BF16)
    rw_cat = jnp.concatenate([rw_hi, (rw - rw_hi.astype(F32)).astype(BF16)], axis=2)

    def vec(a):
        return a.reshape(DEPTH, 1, -1)

    xf = x.reshape(t, d)
    xb = xf
    for i in range(DEPTH):
        u = _in_proj(xb, w_in_bf, i, 512, 2048, F32)
        u3 = u.reshape(bsz, seq, U_COLS)
        ya, yd = _conv_mixers(u3, conf_dw_w[i], conf_dw_b[i], conf_ln_g[i], conf_ln_b[i], sc_conv_w[i])
        yb = _gla(u3, gla_w_g2[i], gla_b_g2[i], gla_norm_g[i])
        yc = _dilated(u3, bias_tab)
        x1, x1b, x1p, ids, gates = _outproj(i, xf, ya.reshape(t, -1), yb.reshape(t, -1), yc.reshape(t, -1),
                                            yd.reshape(t, -1), vec(mix_scale), w_out_bf, vec(ln1_g), vec(ln1_b),
                                            rw_cat[i], rb[i])
        pos, tile_tab = _plan(ids)
        pos_flat = pos[:, :TOP_K].T.reshape(-1)
        tile_end = tile_tab[:n_tiles, 0]
        n_active = jnp.sum((tile_end < N_EXPERTS).astype(I32)).reshape(1)
        tile_expert = jnp.minimum(tile_end, N_EXPERTS - 1)
        xs = _dispatch(pos_flat, x1p, n_rows)
        ys = _experts(i, tile_expert, n_active, xs, exp_w_gate, exp_w_up, exp_w_down)
        xf, xb = _final(i, pos_flat, x1, x1b, gates, p3, ple_wg_bf, vec(ple_b_gate), ple_wp_bf, vec(ln2_g), vec(ln2_b), ys)
    return xf.reshape(bsz, seq, d)
```

```python
import functools
import math

import jax
import jax.numpy as jnp
from jax import lax
from jax.experimental import pallas as pl
from jax.experimental.pallas import tpu as pltpu

F32 = jnp.float32
BF16 = jnp.bfloat16
I32 = jnp.int32
HIGHEST = lax.Precision.HIGHEST

D_MODEL = 2048
DEPTH = 2
PLE_DIM = 256
W_CONF = 512
W_GLA = 512
W_DIL = 512
W_SC = 512
CONF_KERNEL = 31
GLA_HEADS = 4
GLA_DV = 128
GLA_DK = 64
GLA_RANK = 16
GLA_TAU = 16.0
DIL_HEADS = 8
DIL_HD = 64
DIL_BRANCHES = ((128, 1), (512, 4), (2048, 16))
DIL_BLOCK = 128
REL_BUCKETS = 32
REL_MAX_DIST = 2048
SC_KERNEL = 3
N_GROUPS = 4
EXPERTS_PER_GROUP = 8
N_EXPERTS = 32
TOP_K = 2
D_EXPERT = 256
DN_ALPHA = (2 * DEPTH) ** 0.25
LN_EPS = 1e-5
RMS_EPS = 1e-6
IN_SIZES = (2 * W_CONF, GLA_HEADS * GLA_DK, GLA_HEADS * GLA_DK, W_GLA, GLA_RANK, W_GLA,
            W_DIL, W_DIL, W_DIL, W_SC, W_SC, W_SC)

LANES = 128
SUBLANES = 8
VMEM_LIMIT = 56 * 1024 * 1024

GL_PAD = 512
COL_A = 0
COL_GQ = 1024
COL_GK = 1280
COL_GV = 1536
COL_GL = 2048
COL_GR = 2560
COL_CQ = 3072
COL_CK = 3584
COL_CV = 4096
COL_SB = 4608
COL_SC = 5120
COL_SH = 5632
U_COLS = 6144

MOE_TILE = 512
EXPERT_SPLIT = 1


def _cparams(n_axes):
    return pltpu.CompilerParams(dimension_semantics=("arbitrary",) * n_axes, vmem_limit_bytes=VMEM_LIMIT)


def _layer_norm(x, g, b):
    mu = jnp.mean(x, axis=-1, keepdims=True)
    xc = x - mu
    var = jnp.mean(xc * xc, axis=-1, keepdims=True)
    return xc * lax.rsqrt(var + LN_EPS) * g + b


U32 = jnp.uint32


def _pack_bf16_pairs(x):
    n = x.shape[1] // 2
    bits = lax.bitcast_convert_type(x.astype(BF16).astype(F32), U32)
    return (bits[:, :n] >> 16) | (bits[:, n:] & jnp.uint32(0xFFFF0000))


def _unpack_bf16_pairs(w):
    lo = lax.bitcast_convert_type(w << 16, F32)
    hi = lax.bitcast_convert_type(w & jnp.uint32(0xFFFF0000), F32)
    return jnp.concatenate([lo, hi], axis=1)


def _store_row_tiles(ref, packed):
    rows, width = packed.shape
    assert width == SUBLANES * LANES
    for s in range(SUBLANES):
        ref[pl.ds(s, rows, stride=SUBLANES), :] = packed[:, s * LANES:(s + 1) * LANES]


def _load_row_tiles(ref, rows):
    return jnp.concatenate([ref[pl.ds(s, rows, stride=SUBLANES), :] for s in range(SUBLANES)], axis=1)


def _sigmoid(x):
    return 1.0 / (1.0 + jnp.exp(-x))


def _silu(x):
    return x * _sigmoid(x)


def _matmul_kernel(x_ref, w_ref, o_ref):
    o_ref[...] = jnp.dot(x_ref[...].astype(BF16), w_ref[0],
                         preferred_element_type=F32).astype(o_ref.dtype)


def _in_proj(x, w_all, layer, tm, tn, out_dtype):
    m, k = x.shape
    n = w_all.shape[2]
    return pl.pallas_call(
        _matmul_kernel,
        grid=(n // tn, m // tm),
        in_specs=[pl.BlockSpec((tm, k), lambda j, i: (i, 0)),
                  pl.BlockSpec((1, k, tn), lambda j, i: (layer, 0, j))],
        out_specs=pl.BlockSpec((tm, tn), lambda j, i: (i, j)),
        out_shape=jax.ShapeDtypeStruct((m, n), out_dtype),
        compiler_params=_cparams(2),
        name="in_proj",
    )(x, w_all)


CONV_TS = 512
CONV_RC = 64
CONF_HIST = 32
SC_HIST = 8


def _conv_kernel(a_ref, gate_ref, sb_ref, sc_ref, sh_ref, cw_ref, cb_ref, lg_ref, lb_ref, sw_ref,
                 ya_ref, yd_ref, ha, hd, hs):
    s = pl.program_id(1)
    ts = a_ref.shape[1]
    n_ext = CONF_HIST + ts

    @pl.when(s == 0)
    def _():
        ha[0:CONF_HIST, :] = jnp.zeros((CONF_HIST, W_CONF), F32)
        hd[0:SC_HIST, :] = jnp.zeros((SC_HIST, W_SC), F32)

    ha[CONF_HIST:CONF_HIST + ts, :] = a_ref[0] * _sigmoid(gate_ref[0])
    hd[SC_HIST:SC_HIST + ts, :] = sc_ref[0] * sh_ref[0]

    cb = cb_ref[...]
    lg = lg_ref[...]
    lb = lb_ref[...]
    off_a = CONF_HIST - (CONF_KERNEL - 1)
    h_all = ha[...]
    for sh in range(1, SUBLANES):
        h_all = pltpu.roll(h_all, n_ext - 1, 0)
        hs[sh - 1] = h_all
    for c in range(ts // CONV_RC):
        r0 = c * CONV_RC
        acc = jnp.zeros((CONV_RC, W_CONF), F32)
        for j in range(CONF_KERNEL):
            sh = (off_a + j) % SUBLANES
            lo = r0 + off_a + j - sh
            rows = ha[lo:lo + CONV_RC, :] if sh == 0 else hs[sh - 1, lo:lo + CONV_RC, :]
            acc = acc + cw_ref[j:j + 1, :] * rows
        hh = _layer_norm(acc + cb, lg, lb)
        ya_ref[0, r0:r0 + CONV_RC, :] = _silu(hh)

    off_d = SC_HIST - (SC_KERNEL - 1)
    accd = jnp.zeros((ts, W_SC), F32)
    for j in range(SC_KERNEL):
        accd = accd + sw_ref[j:j + 1, :] * hd[off_d + j:off_d + j + ts, :]
    yd_ref[0] = sb_ref[0] * accd

    ha[0:CONF_HIST, :] = ha[ts:ts + CONF_HIST, :]
    hd[0:SC_HIST, :] = hd[ts:ts + SC_HIST, :]


def _conv_mixers(u3, conf_w, conf_b, conf_lg, conf_lb, sc_w):
    bsz, seq, _ = u3.shape
    ts = CONV_TS
    wblk = W_CONF

    def col(c):
        return pl.BlockSpec((1, ts, wblk), lambda b, s, c=c: (b, s, c // wblk))

    def full(shape):
        return pl.BlockSpec(shape, lambda b, s: (0,) * len(shape))

    cw = jnp.zeros((32, W_CONF), F32).at[:CONF_KERNEL].set(conf_w)
    sw = jnp.zeros((8, W_SC), F32).at[:SC_KERNEL].set(sc_w)
    out_spec = pl.BlockSpec((1, ts, wblk), lambda b, s: (b, s, 0))
    return pl.pallas_call(
        _conv_kernel,
        grid=(bsz, seq // ts),
        in_specs=[col(COL_A), col(COL_A + W_CONF), col(COL_SB), col(COL_SC), col(COL_SH),
                  full((32, W_CONF)), full((1, W_CONF)), full((1, W_CONF)), full((1, W_CONF)), full((8, W_SC))],
        out_specs=[out_spec, out_spec],
        out_shape=[jax.ShapeDtypeStruct((bsz, seq, W_CONF), F32), jax.ShapeDtypeStruct((bsz, seq, W_SC), F32)],
        scratch_shapes=[pltpu.VMEM((CONF_HIST + ts, W_CONF), F32), pltpu.VMEM((SC_HIST + ts, W_SC), F32),
                        pltpu.VMEM((SUBLANES - 1, CONF_HIST + ts, W_CONF), F32)],
        compiler_params=_cparams(2),
        name="conv_mixers",
    )(u3, u3, u3, u3, u3, cw, conf_b.reshape(1, -1), conf_lg.reshape(1, -1), conf_lb.reshape(1, -1), sw)


GLA_BLK = 128
GLA_SUB = 64
GLA_UNROLL = 8


def _split3(x):
    hi = x.astype(BF16)
    r1 = x - hi.astype(F32)
    mid = r1.astype(BF16)
    lo = (r1 - mid.astype(F32)).astype(BF16)
    return hi, mid, lo


def _gla_kernel(q_ref, k_ref, v_ref, gl_ref, r_ref, wg2h_ref, wg2l_ref, bg2_ref, ng_ref, tril_ref, o_ref, la_ref):
    seq = q_ref.shape[1]
    nblk = seq // GLA_BLK
    nsub = GLA_BLK // GLA_SUB
    assert nsub == 2 and nblk % GLA_UNROLL == 0

    lane = lax.broadcasted_iota(I32, (1, LANES), 1)
    head_mask = [(lane < GLA_DK).astype(F32), (lane >= GLA_DK).astype(F32)]
    row = lax.broadcasted_iota(I32, (GLA_BLK, GLA_BLK), 0)
    col = lax.broadcasted_iota(I32, (GLA_BLK, GLA_BLK), 1)
    sub_shift = GLA_SUB.bit_length() - 1
    diag_mask = ((row >> sub_shift) == (col >> sub_shift)) & (col <= row)
    second_half = lax.broadcasted_iota(I32, (GLA_BLK, 1), 0) >= GLA_SUB
    scale = GLA_DK ** -0.5
    ng = ng_ref[...]
    tril = tril_ref[...]

    gl = gl_ref[0]
    gl_hi = gl.astype(BF16)
    gl_lo = (gl - gl_hi.astype(F32)).astype(BF16)
    wh = wg2h_ref[...]
    z = (jnp.dot(gl_hi, wh, preferred_element_type=F32) + jnp.dot(gl_lo, wh, preferred_element_type=F32)
         + jnp.dot(gl_hi, wg2l_ref[...], preferred_element_type=F32) + bg2_ref[...])
    la_ref[...] = (jnp.minimum(z, 0.0) - jnp.log(1.0 + jnp.exp(-jnp.abs(z)))) * (1.0 / GLA_TAU)

    nt = (((1,), (1,)), ((), ()))

    def body(it, carry):
        grp = range(GLA_UNROLL)
        r0 = [pl.multiple_of((it * GLA_UNROLL + j) * GLA_BLK, GLA_BLK) for j in grp]
        q = [q_ref[0, pl.ds(r, GLA_BLK), :] * scale for r in r0]
        k = [k_ref[0, pl.ds(r, GLA_BLK), :] for r in r0]
        v = [v_ref[0, pl.ds(r, GLA_BLK), :].astype(BF16) for r in r0]
        b = []
        for r in r0:
            acc = None
            for part in _split3(la_ref[pl.ds(r, GLA_BLK), :]):
                t = jnp.dot(tril, part, preferred_element_type=F32)
                acc = t if acc is None else acc + t
            b.append(acc)

        s_in, q_st, kk, b_last = [], [], [], []
        for j in grp:
            b3 = b[j].reshape(nsub, GLA_SUB, LANES)
            mid = jnp.broadcast_to(b3[:, GLA_SUB // 2 - 1:GLA_SUB // 2, :], b3.shape).reshape(GLA_BLK, LANES)
            qd = q[j] * jnp.exp(b[j] - mid)
            kd = (k[j] * jnp.exp(mid - b[j])).astype(BF16)
            e0 = b[j][GLA_SUB - 1:GLA_SUB, :]
            q_off = jnp.where(second_half, q[j] * jnp.exp(jnp.minimum(b[j] - e0, 0.0)), 0.0)
            k_off = jnp.where(second_half, 0.0, k[j] * jnp.exp(jnp.minimum(e0 - b[j], 0.0))).astype(BF16)
            s_in.append((qd, kd, q_off, k_off))
            q_st.append(q[j] * jnp.exp(b[j]))
            b_last.append(b[j][GLA_BLK - 1:GLA_BLK, :])
            kk.append((k[j] * jnp.exp(b_last[j] - b[j])).astype(BF16))

        scores = []
        for j in grp:
            qd, kd, q_off, k_off = s_in[j]
            per_head = []
            for h in range(2):
                hm = head_mask[h]
                s_d = lax.dot_general((qd * hm).astype(BF16), kd, nt, preferred_element_type=F32)
                s_o = lax.dot_general((q_off * hm).astype(BF16), k_off, nt, preferred_element_type=F32)
                per_head.append((jnp.where(diag_mask, s_d, 0.0) + s_o).astype(BF16))
            scores.append(per_head)

        states = []
        st = carry
        for j in grp:
            states.append(st.astype(BF16))
            upd = lax.dot_general(kk[j], v[j], (((0,), (0,)), ((), ())), preferred_element_type=F32)
            decay = jnp.transpose(jnp.broadcast_to(jnp.exp(b_last[j]), (GLA_BLK, LANES)))
            st = st * jnp.concatenate([decay, decay], axis=1) + upd

        for j in grp:
            outs = []
            for h in range(2):
                lhs = jnp.concatenate([scores[j][h], (q_st[j] * head_mask[h]).astype(BF16)], axis=1)
                rhs = jnp.concatenate([v[j][:, h * GLA_DV:(h + 1) * GLA_DV],
                                       states[j][:, h * GLA_DV:(h + 1) * GLA_DV]], axis=0)
                o_h = jnp.dot(lhs, rhs, preferred_element_type=F32)
                outs.append(o_h * lax.rsqrt(jnp.mean(o_h * o_h, axis=-1, keepdims=True) + RMS_EPS))
            o = jnp.concatenate(outs, axis=1) * ng
            o_ref[0, pl.ds(r0[j], GLA_BLK), :] = o * _silu(r_ref[0, pl.ds(r0[j], GLA_BLK), :])
        return st

    lax.fori_loop(0, nblk // GLA_UNROLL, body, jnp.zeros((2 * GLA_DK, 2 * GLA_DV), F32))


def _gla(u3, w_g2, b_g2, norm_g):
    bsz, seq, _ = u3.shape
    npair = GLA_HEADS // 2
    kw = 2 * GLA_DK
    vw = 2 * GLA_DV

    def col(c, w):
        return pl.BlockSpec((1, seq, w), lambda b, p, c=c, w=w: (b, 0, c // w + p))

    wg2 = jnp.zeros((LANES, GLA_HEADS * GLA_DK), F32).at[:GLA_RANK].set(w_g2)
    wg2_hi = wg2.astype(BF16)
    wg2_lo = (wg2 - wg2_hi.astype(F32)).astype(BF16)
    ng = jnp.concatenate([norm_g, norm_g]).reshape(1, vw)
    tril = jnp.tril(jnp.ones((GLA_BLK, GLA_BLK), BF16))
    return pl.pallas_call(
        _gla_kernel,
        grid=(bsz, npair),
        in_specs=[col(COL_GQ, kw), col(COL_GK, kw), col(COL_GV, vw),
                  pl.BlockSpec((1, seq, LANES), lambda b, p: (b, 0, COL_GL // LANES)),
                  col(COL_GR, vw),
                  pl.BlockSpec((LANES, kw), lambda b, p: (0, p)),
                  pl.BlockSpec((LANES, kw), lambda b, p: (0, p)),
                  pl.BlockSpec((1, kw), lambda b, p: (0, p)),
                  pl.BlockSpec((1, vw), lambda b, p: (0, 0)),
                  pl.BlockSpec((GLA_BLK, GLA_BLK), lambda b, p: (0, 0))],
        out_specs=pl.BlockSpec((1, seq, vw), lambda b, p: (b, 0, p)),
        out_shape=jax.ShapeDtypeStruct((bsz, seq, W_GLA), F32),
        scratch_shapes=[pltpu.VMEM((seq, LANES), F32)],
        compiler_params=_cparams(2),
        name="gla",
    )(u3, u3, u3, u3, u3, wg2_hi, wg2_lo, b_g2.reshape(1, -1), ng, tril)


def _t5_bucket(dist):
    max_exact = REL_BUCKETS // 2
    large = max_exact + (jnp.log(jnp.maximum(dist, 1).astype(F32) / max_exact)
                         / math.log(REL_MAX_DIST / max_exact) * (REL_BUCKETS - max_exact)).astype(I32)
    large = jnp.minimum(large, REL_BUCKETS - 1)
    return jnp.where(dist < max_exact, dist, large)


LOG2E = math.log2(math.e)
DIL_STAGE = 4
DIL_GROUP = 8


def _bias_kernel(rel_ref, bucket_ref, o_ref):
    h = pl.program_id(1)
    blk = DIL_BLOCK
    bucket = bucket_ref[0]
    acc = jnp.zeros(bucket.shape, F32)
    for b in range(REL_BUCKETS):
        acc = jnp.where(bucket == b, rel_ref[b, h], acc)
    acc = acc * LOG2E
    qi = lax.broadcasted_iota(I32, (blk, 2 * blk), 0)
    kj = lax.broadcasted_iota(I32, (blk, 2 * blk), 1)
    steps = qi + blk - kj
    in_window = (steps >= 0) & (steps <= blk)
    neg = jnp.float32(-jnp.inf)
    o_ref[0, 0, 0] = jnp.where(in_window, acc, neg)
    o_ref[0, 0, 1] = jnp.where(in_window & (kj >= blk), acc, neg)


def _bias_tables(rel_bias):
    assert all(window // dil == DIL_BLOCK for window, dil in DIL_BRANCHES)
    qi = jnp.arange(DIL_BLOCK)[:, None]
    kj = jnp.arange(2 * DIL_BLOCK)[None, :]
    steps = jnp.maximum(qi + DIL_BLOCK - kj, 0)
    buckets = jnp.stack([_t5_bucket(steps * dil) for _, dil in DIL_BRANCHES]).astype(I32)
    nbr = len(DIL_BRANCHES)
    return pl.pallas_call(
        _bias_kernel,
        grid=(nbr, DIL_HEADS),
        in_specs=[pl.BlockSpec(memory_space=pltpu.SMEM),
                  pl.BlockSpec((1, DIL_BLOCK, 2 * DIL_BLOCK), lambda i, h: (i, 0, 0))],
        out_specs=pl.BlockSpec((1, 1, 2, DIL_BLOCK, 2 * DIL_BLOCK), lambda i, h: (i, h, 0, 0, 0)),
        out_shape=jax.ShapeDtypeStruct((nbr, DIL_HEADS, 2, DIL_BLOCK, 2 * DIL_BLOCK), F32),
        compiler_params=_cparams(2),
        name="rel_bias_tables",
    )(rel_bias, buckets)


def _dil_kernel(q_ref, k_ref, v_ref, bias_ref, o_ref, qs, ks, vs, ob, lb, stage):
    seq = q_ref.shape[1]
    blk = DIL_BLOCK
    scale = DIL_HD ** -0.5 * LOG2E
    lane = lax.broadcasted_iota(I32, (1, LANES), 1)
    first = lane < DIL_HD
    head_mask = [first.astype(F32), 1.0 - first.astype(F32)]

    ks[0:blk, :] = jnp.zeros((blk, LANES), BF16)
    vs[0:blk, :] = jnp.zeros((blk, LANES), BF16)

    for bi, (window, dil) in enumerate(DIL_BRANCHES):
        sub_len = seq // dil
        nb = sub_len // blk
        nb_shift = nb.bit_length() - 1
        for r in range(dil):
            dst = slice(r * sub_len, (r + 1) * sub_len)
            if dil == 1:
                q, k, v = q_ref[0], k_ref[0], v_ref[0]
            elif dil == DIL_STAGE:
                rows = pl.ds(r, sub_len, stride=dil)
                q, k, v = q_ref[0, rows, :], k_ref[0, rows, :], v_ref[0, rows, :]
                stage[0, dst, :] = q
                stage[1, dst, :] = k
                stage[2, dst, :] = v
            else:
                assert dil == DIL_STAGE * DIL_STAGE
                rows = pl.ds((r % DIL_STAGE) * (seq // DIL_STAGE) + r // DIL_STAGE, sub_len, stride=DIL_STAGE)
                q, k, v = stage[0, rows, :], stage[1, rows, :], stage[2, rows, :]
            q = q * scale
            for h in range(2):
                qs[h, dst, :] = (q * head_mask[h]).astype(BF16)
            ks[blk + r * sub_len:blk + (r + 1) * sub_len, :] = k.astype(BF16)
            vs[blk + r * sub_len:blk + (r + 1) * sub_len, :] = v.astype(BF16)

        def body(it, carry, bi=bi, dil=dil, nb=nb, nb_shift=nb_shift):
            grp = range(DIL_GROUP)
            g = [it * DIL_GROUP + gg for gg in grp]
            n = [gi & (nb - 1) for gi in g]
            g0 = [pl.multiple_of(gi * blk, blk) for gi in g]
            kb = [ks[pl.ds(s0, 2 * blk), :] for s0 in g0]
            sc = [[lax.dot_general(qs[h, pl.ds(g0[gg], blk), :], kb[gg], (((1,), (1,)), ((), ())),
                                   preferred_element_type=F32) + bias_ref[bi, h, jnp.where(n[gg] == 0, 1, 0)]
                   for h in range(2)] for gg in grp]
            stats = []
            for gg in grp:
                per_head = []
                for h in range(2):
                    m = jnp.max(sc[gg][h], axis=-1, keepdims=True)
                    p = jnp.exp2(sc[gg][h] - m)
                    l = jnp.sum(p, axis=-1, keepdims=True)
                    per_head.append((p.astype(BF16), m, l))
                stats.append(per_head)
            for gg in grp:
                vb = vs[pl.ds(g0[gg], 2 * blk), :]
                o_acc = None
                l_acc = None
                for h in range(2):
                    p, m, l = stats[gg][h]
                    o_h = jnp.dot(p, vb, preferred_element_type=F32) * (1.0 / l)
                    lse = jnp.broadcast_to(m + jnp.log2(l), (blk, LANES))
                    if h == 0:
                        o_acc, l_acc = o_h, lse
                    else:
                        o_acc = jnp.where(first, o_acc, o_h)
                        l_acc = jnp.where(first, l_acc, lse)
                if dil > 1:
                    dst = pl.ds(n[gg] * (blk * dil) + (g[gg] >> nb_shift), blk, stride=dil)
                else:
                    dst = pl.ds(g0[gg], blk)
                ob[bi, dst, :] = o_acc
                lb[bi, dst, :] = l_acc
            return carry

        lax.fori_loop(0, seq // blk // DIL_GROUP, body, 0)

    nbr = len(DIL_BRANCHES)
    mc = 256
    for c in range(seq // mc):
        rs = slice(c * mc, (c + 1) * mc)
        ls = [lb[i, rs, :] for i in range(nbr)]
        m = functools.reduce(jnp.maximum, ls)
        ws = [jnp.exp2(l - m) for l in ls]
        num = functools.reduce(lambda a, b: a + b, [w * ob[i, rs, :] for i, w in enumerate(ws)])
        den = functools.reduce(lambda a, b: a + b, ws)
        o_ref[0, rs, :] = num / den


def _dilated(u3, bias_tab):
    bsz, seq, _ = u3.shape
    npair = DIL_HEADS // 2
    nbr = len(DIL_BRANCHES)

    def col(c):
        return pl.BlockSpec((1, seq, LANES), lambda b, p, c=c: (b, 0, c // LANES + p))

    return pl.pallas_call(
        _dil_kernel,
        grid=(bsz, npair),
        in_specs=[col(COL_CQ), col(COL_CK), col(COL_CV),
                  pl.BlockSpec((nbr, 2, 2, DIL_BLOCK, 2 * DIL_BLOCK), lambda b, p: (0, p, 0, 0, 0))],
        out_specs=pl.BlockSpec((1, seq, LANES), lambda b, p: (b, 0, p)),
        out_shape=jax.ShapeDtypeStruct((bsz, seq, W_DIL), F32),
        scratch_shapes=[pltpu.VMEM((2, seq, LANES), BF16),
                        pltpu.VMEM((DIL_BLOCK + seq, LANES), BF16),
                        pltpu.VMEM((DIL_BLOCK + seq, LANES), BF16),
                        pltpu.VMEM((nbr, seq, LANES), F32),
                        pltpu.VMEM((nbr, seq, LANES), F32),
                        pltpu.VMEM((3, seq, LANES), F32)],
        compiler_params=_cparams(2),
        name="dilated_attn",
    )(u3, u3, u3, bias_tab)


TOK_TILE = 256
OUT_TILE = 512
OUT_SPLIT = 2
FINAL_SPLIT = 2
GATHER_SLOTS = 3
PLAN_TILE = 1024


def _outproj_kernel(x_ref, ya_ref, yb_ref, yc_ref, yd_ref, ms_ref, wo_ref, g_ref, b_ref, rw_ref, rb_ref,
                    x1_ref, x1b_ref, x1p_ref, ids_ref, gates_ref):
    tm = x_ref.shape[0]
    rows = tm // OUT_SPLIT
    parts = [slice(pi * rows, (pi + 1) * rows) for pi in range(OUT_SPLIT)]
    accs = []
    for rs in parts:
        acc = None
        for gi, y_ref in enumerate((ya_ref, yb_ref, yc_ref, yd_ref)):
            lo = gi * W_CONF
            y = (y_ref[rs, :] * ms_ref[0, :, lo:lo + W_CONF]).astype(BF16)
            part = jnp.dot(y, wo_ref[0, lo:lo + W_CONF, :], preferred_element_type=F32)
            acc = part if acc is None else acc + part
        accs.append(acc)

    logits = []
    for rs, acc in zip(parts, accs):
        x1 = _layer_norm(DN_ALPHA * x_ref[rs, :] + acc, g_ref[0], b_ref[0])
        x1_ref[rs, :] = x1
        x1_hi = x1.astype(BF16)
        x1b_ref[rs, :] = x1_hi
        _store_row_tiles(x1p_ref.at[pl.ds(rs.start * SUBLANES, rows * SUBLANES)], _pack_bf16_pairs(x1))
        x1_lo = (x1 - x1_hi.astype(F32)).astype(BF16)
        hh_hl = jnp.dot(x1_hi, rw_ref[...], preferred_element_type=F32)
        logits.append(hh_hl[:, :LANES] + hh_hl[:, LANES:]
                      + jnp.dot(x1_lo, rw_ref[:, :LANES], preferred_element_type=F32) + rb_ref[...])

    lane = lax.broadcasted_iota(I32, (rows, LANES), 1)
    neg = jnp.float32(-jnp.inf)
    for rs, lg in zip(parts, logits):
        gl = jnp.where(lane < N_GROUPS, lg, neg)
        gmax = jnp.max(gl, axis=-1, keepdims=True)
        g_top = 1.0 / jnp.sum(jnp.exp(gl - gmax), axis=-1, keepdims=True)
        g_idx = jnp.min(jnp.where(gl == gmax, lane, LANES), axis=-1, keepdims=True)
        e_lo = N_GROUPS + g_idx * EXPERTS_PER_GROUP
        el = jnp.where((lane >= e_lo) & (lane < e_lo + EXPERTS_PER_GROUP), lg, neg)
        m1 = jnp.max(el, axis=-1, keepdims=True)
        i1 = jnp.min(jnp.where(el == m1, lane, LANES), axis=-1, keepdims=True)
        el2 = jnp.where(lane == i1, neg, el)
        m2 = jnp.max(el2, axis=-1, keepdims=True)
        i2 = jnp.min(jnp.where(el2 == m2, lane, LANES), axis=-1, keepdims=True)
        e21 = jnp.exp(m2 - m1)
        w1 = g_top / (1.0 + e21)
        w2 = g_top * e21 / (1.0 + e21)
        ids_ref[rs, :] = jnp.where(lane == 0, i1 - N_GROUPS, jnp.where(lane == 1, i2 - N_GROUPS, N_EXPERTS))
        gates_ref[rs, :] = jnp.where(lane == 0, w1, jnp.where(lane == 1, w2, 0.0))


def _outproj(layer, x, ya, yb, yc, yd, mix_scale, w_out_bf, ln_g, ln_b, rw_cat, rb):
    t, d = x.shape
    tm = OUT_TILE

    def row(w):
        return pl.BlockSpec((tm, w), lambda i: (i, 0))

    def full(shape):
        return pl.BlockSpec(shape, lambda i: (0,) * len(shape))

    def per_layer(shape):
        return pl.BlockSpec((1,) + shape, lambda i: (layer,) + (0,) * len(shape))

    return pl.pallas_call(
        _outproj_kernel,
        grid=(t // tm,),
        in_specs=[row(d), row(W_CONF), row(W_GLA), row(W_DIL), row(W_SC), per_layer((1, d)), per_layer((d, d)),
                  per_layer((1, d)), per_layer((1, d)), full((d, 2 * LANES)), full((1, LANES))],
        out_specs=[row(d), row(d), pl.BlockSpec((tm * SUBLANES, LANES), lambda i: (i, 0)), row(LANES), row(LANES)],
        out_shape=[jax.ShapeDtypeStruct((t, d), F32), jax.ShapeDtypeStruct((t, d), BF16),
                   jax.ShapeDtypeStruct((t * SUBLANES, LANES), U32),
                   jax.ShapeDtypeStruct((t, LANES), I32), jax.ShapeDtypeStruct((t, LANES), F32)],
        compiler_params=_cparams(1),
        name="out_proj_ln_router",
    )(x, ya, yb, yc, yd, mix_scale, w_out_bf, ln_g, ln_b, rw_cat, rb)


def _plan_kernel(ids_ref, tril_ref, upper_ref, pos_ref, tile_ref, cnt, carry, start):
    phase = pl.program_id(0)
    i = pl.program_id(1)
    tm = ids_ref.shape[0]
    lane = lax.broadcasted_iota(I32, (tm, LANES), 1)
    ids = ids_ref[...]
    e1 = ids[:, 0:1]
    e2 = ids[:, 1:2]
    oh1 = (lane == e1).astype(F32)
    oh2 = (lane == e2).astype(F32)
    oh = oh1 + oh2

    @pl.when((phase == 0) & (i == 0))
    def _():
        cnt[...] = jnp.zeros(cnt.shape, F32)

    @pl.when(phase == 0)
    def _():
        cnt[...] = cnt[...] + jnp.sum(oh, axis=0, keepdims=True)

    @pl.when((phase == 1) & (i == 0))
    def _():
        padded = jnp.floor((cnt[...] + (MOE_TILE - 1)) * (1.0 / MOE_TILE)) * MOE_TILE
        p8 = jnp.broadcast_to(padded, (SUBLANES, LANES))
        st = jnp.dot(p8, upper_ref[...], precision=HIGHEST, preferred_element_type=F32)
        start[...] = st[0:1, :]
        carry[...] = jnp.zeros(carry.shape, F32)
        end = st[0:1, :] + padded
        trow = lax.broadcasted_iota(I32, (LANES, LANES), 0).astype(F32) * MOE_TILE
        tlane = lax.broadcasted_iota(I32, (LANES, LANES), 1)
        done = jnp.where((end <= trow) & (tlane < N_EXPERTS), 1.0, 0.0)
        tile_ref[...] = jnp.broadcast_to(jnp.sum(done, axis=-1, keepdims=True), (LANES, LANES)).astype(I32)

    @pl.when(phase == 1)
    def _():
        incl = jnp.dot(tril_ref[...], oh.astype(BF16), preferred_element_type=F32) + carry[...]
        base = start[...] + incl - oh
        p1 = jnp.sum(oh1 * base, axis=-1, keepdims=True)
        p2 = jnp.sum(oh2 * base, axis=-1, keepdims=True)
        pos_ref[...] = jnp.where(lane == 0, p1, jnp.where(lane == 1, p2, 0.0)).astype(I32)
        carry[...] = carry[...] + jnp.sum(oh, axis=0, keepdims=True)


def _plan(ids):
    t = ids.shape[0]
    tm = PLAN_TILE
    tril = jnp.tril(jnp.ones((tm, tm), BF16))
    upper = jnp.triu(jnp.ones((LANES, LANES), F32), k=1)
    return pl.pallas_call(
        _plan_kernel,
        grid=(2, t // tm),
        in_specs=[pl.BlockSpec((tm, LANES), lambda ph, i: (i, 0)),
                  pl.BlockSpec((tm, tm), lambda ph, i: (0, 0)),
                  pl.BlockSpec((LANES, LANES), lambda ph, i: (0, 0))],
        out_specs=[pl.BlockSpec((tm, LANES), lambda ph, i: (i * ph, 0)),
                   pl.BlockSpec((LANES, LANES), lambda ph, i: (0, 0))],
        out_shape=[jax.ShapeDtypeStruct((t, LANES), I32), jax.ShapeDtypeStruct((LANES, LANES), I32)],
        scratch_shapes=[pltpu.VMEM((1, LANES), F32), pltpu.VMEM((1, LANES), F32), pltpu.VMEM((1, LANES), F32)],
        compiler_params=_cparams(2),
        name="route_plan",
    )(ids, tril, upper)


DISPATCH_TILE = 512
DISPATCH_SLOTS = 3


def _dispatch_kernel(pos_ref, x_ref, xs_ref, stage, sem):
    i = pl.program_id(0)
    n_steps = pl.num_programs(0)
    tm = x_ref.shape[0] // SUBLANES
    t = n_steps * tm
    slot = lax.rem(i, DISPATCH_SLOTS)

    def wait_slot(s):
        for _ in range(TOP_K):
            pltpu.make_async_copy(stage.at[s], xs_ref.at[pl.ds(0, tm * SUBLANES)], sem.at[s]).wait()

    @pl.when(i >= DISPATCH_SLOTS)
    def _():
        wait_slot(slot)

    stage[slot] = x_ref[...]
    for m in range(tm):
        for k in range(TOP_K):
            dst = pl.multiple_of(pos_ref[k * t + i * tm + m] * SUBLANES, SUBLANES)
            pltpu.make_async_copy(stage.at[slot, pl.ds(m * SUBLANES, SUBLANES)], xs_ref.at[pl.ds(dst, SUBLANES)],
                                  sem.at[slot]).start(priority=k)

    @pl.when(i == n_steps - 1)
    def _():
        for s in range(DISPATCH_SLOTS):
            wait_slot(s)


def _dispatch(pos_flat, x1p, n_rows):
    t = x1p.shape[0] // SUBLANES
    tm = DISPATCH_TILE
    assert t // tm >= DISPATCH_SLOTS
    return pl.pallas_call(
        _dispatch_kernel,
        grid_spec=pltpu.PrefetchScalarGridSpec(
            num_scalar_prefetch=1,
            grid=(t // tm,),
            in_specs=[pl.BlockSpec((tm * SUBLANES, LANES), lambda i, pos: (i, 0))],
            out_specs=pl.BlockSpec(memory_space=pl.ANY),
            scratch_shapes=[pltpu.VMEM((DISPATCH_SLOTS, tm * SUBLANES, LANES), x1p.dtype),
                            pltpu.SemaphoreType.DMA((DISPATCH_SLOTS,))],
        ),
        out_shape=jax.ShapeDtypeStruct((n_rows * SUBLANES, LANES), x1p.dtype),
        compiler_params=_cparams(1),
        name="moe_dispatch",
    )(pos_flat, x1p)


def _expert_kernel(layer, te_ref, na_ref, first_ref, run_ref, next2_ref, runs_ref, nruns_ref,
                   xs_ref, wg_hbm, wu_hbm, wd_hbm, ys_ref, wg_buf, wu_buf, wd_buf, wg_bf, wu_bf, wd_bf, sem):
    i = pl.program_id(0)

    def copies(e, s):
        return (pltpu.make_async_copy(wg_hbm.at[layer, e], wg_buf.at[s], sem.at[s]),
                pltpu.make_async_copy(wu_hbm.at[layer, e], wu_buf.at[s], sem.at[s]),
                pltpu.make_async_copy(wd_hbm.at[layer, e], wd_buf.at[s], sem.at[s]))

    def fetch(e, s):
        for c in copies(e, s):
            c.start()

    @pl.when(i == 0)
    def _():
        fetch(runs_ref[0], 0)

        @pl.when(nruns_ref[0] > 1)
        def _():
            fetch(runs_ref[1], 1)

    active = i < na_ref[0]

    @pl.when(active & (first_ref[i] == 1))
    def _():
        s = run_ref[i] % 2
        for c in copies(te_ref[i], s):
            c.wait()
        wg_bf[...] = wg_buf[s].astype(BF16)
        wu_bf[...] = wu_buf[s].astype(BF16)
        wd_bf[...] = wd_buf[s].astype(BF16)

        @pl.when(next2_ref[i] >= 0)
        def _():
            fetch(next2_ref[i], s)

    @pl.when(active)
    def _():
        rows = MOE_TILE // EXPERT_SPLIT
        views = [pl.ds(pi * rows * SUBLANES, rows * SUBLANES) for pi in range(EXPERT_SPLIT)]
        xb = [_unpack_bf16_pairs(_load_row_tiles(xs_ref.at[v], rows)).astype(BF16) for v in views]
        gu = [(jnp.dot(x, wg_bf[...], preferred_element_type=F32), jnp.dot(x, wu_bf[...], preferred_element_type=F32))
              for x in xb]
        h = [(_silu(g) * u).astype(BF16) for g, u in gu]
        y = [jnp.dot(hh, wd_bf[...], preferred_element_type=F32) for hh in h]
        for v, yy in zip(views, y):
            _store_row_tiles(ys_ref.at[v], _pack_bf16_pairs(yy))


def _experts(layer, tile_expert, n_active, xs, w_gate, w_up, w_down):
    n_rows = xs.shape[0] // SUBLANES
    d = LANES
    n_tiles = n_rows // MOE_TILE
    d_model = w_gate.shape[2]

    tile = jnp.arange(n_tiles, dtype=I32)
    active = tile < n_active[0]
    first = active & ((tile == 0) | (tile_expert != jnp.roll(tile_expert, 1)))
    run_of_tile = jnp.cumsum(first.astype(I32)) - 1
    n_runs = jnp.sum(first.astype(I32)).reshape(1)
    is_run_start = first[None, :] & (run_of_tile[None, :] == tile[:, None])
    run_expert = jnp.sum(jnp.where(is_run_start, tile_expert[None, :], 0), axis=1).astype(I32)
    nxt = run_of_tile + 2
    next2 = jnp.where(nxt < n_runs[0], run_expert[jnp.minimum(nxt, n_tiles - 1)], -1).astype(I32)
    first = first.astype(I32)

    def rows(i, te, na, *_):
        return (jnp.minimum(i, na[0] - 1), 0)

    any_spec = pl.BlockSpec(memory_space=pl.ANY)
    return pl.pallas_call(
        functools.partial(_expert_kernel, layer),
        grid_spec=pltpu.PrefetchScalarGridSpec(
            num_scalar_prefetch=7,
            grid=(n_tiles,),
            in_specs=[pl.BlockSpec((MOE_TILE * SUBLANES, d), rows), any_spec, any_spec, any_spec],
            out_specs=pl.BlockSpec((MOE_TILE * SUBLANES, d), rows),
            scratch_shapes=[pltpu.VMEM((2, d_model, D_EXPERT), F32), pltpu.VMEM((2, d_model, D_EXPERT), F32),
                            pltpu.VMEM((2, D_EXPERT, d_model), F32),
                            pltpu.VMEM((d_model, D_EXPERT), BF16), pltpu.VMEM((d_model, D_EXPERT), BF16),
                            pltpu.VMEM((D_EXPERT, d_model), BF16),
                            pltpu.SemaphoreType.DMA((2,))],
        ),
        out_shape=jax.ShapeDtypeStruct((n_rows * SUBLANES, d), U32),
        compiler_params=_cparams(1),
        name="moe_experts",
    )(tile_expert, n_active, first, run_of_tile, next2, run_expert, n_runs, xs, w_gate, w_up, w_down)


def _final_kernel(pos_ref, x1_ref, x1b_ref, gates_ref, p_ref, wg_ref, bg_ref, wp_ref, g_ref, b_ref, ys_ref,
                  x2_ref, x2b_ref, ybuf, sem):
    i = pl.program_id(0)
    n_steps = pl.num_programs(0)
    tm = x1_ref.shape[0]
    t = n_steps * tm

    def gather(step, slot):
        for m in range(tm):
            for k in range(TOP_K):
                src = pl.multiple_of(pos_ref[k * t + step * tm + m] * SUBLANES, SUBLANES)
                pltpu.make_async_copy(ys_ref.at[pl.ds(src, SUBLANES)],
                                      ybuf.at[slot, k, pl.ds(m * SUBLANES, SUBLANES)],
                                      sem.at[slot]).start(priority=k)

    def wait_slot(slot):
        for k in range(TOP_K):
            pltpu.make_async_copy(ys_ref.at[pl.ds(0, tm * SUBLANES)], ybuf.at[slot, k], sem.at[slot]).wait()

    @pl.when(i == 0)
    def _():
        gather(0, 0)
        gather(1, 1)

    slot = lax.rem(i, GATHER_SLOTS)
    wait_slot(slot)
    gather(jnp.minimum(i + 2, n_steps - 1), lax.rem(i + 2, GATHER_SLOTS))

    rows = tm // FINAL_SPLIT
    parts = [slice(pi * rows, (pi + 1) * rows) for pi in range(FINAL_SPLIT)]
    ples = []
    for rs in parts:
        gate = _sigmoid(jnp.dot(x1b_ref[rs, :], wg_ref[0], preferred_element_type=F32) + bg_ref[0])
        ples.append(gate * jnp.dot(p_ref[0, rs, :].astype(BF16), wp_ref[0], preferred_element_type=F32))
    for rs, ple in zip(parts, ples):
        gates = gates_ref[rs, :]
        tiles = pl.ds(rs.start * SUBLANES, rows * SUBLANES)
        ffn = (gates[:, 0:1] * _unpack_bf16_pairs(_load_row_tiles(ybuf.at[slot, 0, tiles], rows))
               + gates[:, 1:2] * _unpack_bf16_pairs(_load_row_tiles(ybuf.at[slot, 1, tiles], rows)))
        x2 = _layer_norm(DN_ALPHA * x1_ref[rs, :] + ffn + ple, g_ref[0], b_ref[0])
        x2_ref[rs, :] = x2
        x2b_ref[rs, :] = x2.astype(BF16)

    @pl.when(i == n_steps - 1)
    def _():
        wait_slot(lax.rem(i + 1, GATHER_SLOTS))
        wait_slot(lax.rem(i + 2, GATHER_SLOTS))


def _final(layer, pos_flat, x1, x1b, gates, p, wg_bf, bg, wp_bf, ln_g, ln_b, ys):
    t, d = x1.shape
    tm = TOK_TILE

    def row(w):
        return pl.BlockSpec((tm, w), lambda i, pos: (i, 0))

    def per_layer(shape):
        return pl.BlockSpec((1,) + shape, lambda i, pos: (layer,) + (0,) * len(shape))

    return pl.pallas_call(
        _final_kernel,
        grid_spec=pltpu.PrefetchScalarGridSpec(
            num_scalar_prefetch=1,
            grid=(t // tm,),
            in_specs=[row(d), row(d), row(LANES), pl.BlockSpec((1, tm, PLE_DIM), lambda i, pos: (layer, i, 0)),
                      per_layer((d, d)), per_layer((1, d)), per_layer((PLE_DIM, d)),
                      per_layer((1, d)), per_layer((1, d)), pl.BlockSpec(memory_space=pl.ANY)],
            out_specs=[row(d), row(d)],
            scratch_shapes=[pltpu.VMEM((GATHER_SLOTS, TOP_K, tm * SUBLANES, LANES), U32),
                            pltpu.SemaphoreType.DMA((GATHER_SLOTS,))],
        ),
        out_shape=[jax.ShapeDtypeStruct((t, d), F32), jax.ShapeDtypeStruct((t, d), BF16)],
        compiler_params=_cparams(1),
        name="combine_ple_ln",
    )(pos_flat, x1, x1b, gates, p, wg_bf, bg, wp_bf, ln_g, ln_b, ys)


W_PREP_COLS = 512


def _arrange_kernel(wt_ref, o_ref):
    j = pl.program_id(1)
    wt = wt_ref[0]
    row = lax.broadcasted_iota(I32, (wt.shape[0], 1), 0)
    n_valid = jnp.where(j == COL_GL // W_PREP_COLS, GLA_RANK, W_PREP_COLS)
    o_ref[0] = jnp.where(row < n_valid, wt, 0.0).T.astype(BF16)


def _arrange_w_in(w):
    n_layer, k, n_in = w.shape
    assert sum(IN_SIZES[:5]) == COL_GL + GLA_RANK and n_in + GL_PAD - GLA_RANK == U_COLS
    assert COL_GL % W_PREP_COLS == 0 and GL_PAD == W_PREP_COLS
    gl_blk = COL_GL // W_PREP_COLS

    def src_row(l, j):
        row = jnp.where(j <= gl_blk, j * W_PREP_COLS, j * W_PREP_COLS - (GL_PAD - GLA_RANK))
        return (l, pl.multiple_of(row, GLA_RANK), 0)

    return pl.pallas_call(
        _arrange_kernel,
        grid=(n_layer, U_COLS // W_PREP_COLS),
        in_specs=[pl.BlockSpec((pl.Element(1), pl.Element(W_PREP_COLS), pl.Element(k)), src_row)],
        out_specs=pl.BlockSpec((1, k, W_PREP_COLS), lambda l, j: (l, 0, j)),
        out_shape=jax.ShapeDtypeStruct((n_layer, k, U_COLS), BF16),
        compiler_params=_cparams(2),
        name="arrange_w_in",
    )(jnp.swapaxes(w, 1, 2))


def kernel(x, p, w_in, conf_dw_w, conf_dw_b, conf_ln_g, conf_ln_b, gla_w_g2, gla_b_g2, gla_norm_g, sc_conv_w, mix_scale, w_out, rel_bias, ln1_g, ln1_b, router_g_w, router_g_b, router_e_w, router_e_b, exp_w_gate, exp_w_up, exp_w_down, ple_w_gate, ple_b_gate, ple_w_proj, ln2_g, ln2_b):
    bsz, seq, d = x.shape
    t = bsz * seq
    n_rows = TOP_K * t + N_EXPERTS * MOE_TILE
    n_tiles = n_rows // MOE_TILE
    bias_tab = _bias_tables(rel_bias)
    w_in_bf = _arrange_w_in(w_in)
    w_out_bf = w_out.astype(BF16)
    ple_wg_bf = ple_w_gate.astype(BF16)
    ple_wp_bf = ple_w_proj.astype(BF16)
    p3 = p.reshape(DEPTH, t, PLE_DIM)
    n_route = N_GROUPS + N_EXPERTS
    rw = jnp.concatenate([router_g_w, router_e_w, jnp.zeros((DEPTH, d, LANES - n_route), F32)], axis=2)
    rb = jnp.concatenate([router_g_b, router_e_b, jnp.zeros((DEPTH, LANES - n_route), F32)], axis=1)[:, None, :]
    rw_hi = rw.astype(BF16)
    rw_cat = jnp.concatenate([rw_hi, (rw - rw_hi.astype(F32)).astype(BF16)], axis=2)

    def vec(a):
        return a.reshape(DEPTH, 1, -1)

    xf = x.reshape(t, d)
    xb = xf
    for i in range(DEPTH):
        u = _in_proj(xb, w_in_bf, i, 512, 2048, F32)
        u3 = u.reshape(bsz, seq, U_COLS)
        ya, yd = _conv_mixers(u3, conf_dw_w[i], conf_dw_b[i], conf_ln_g[i], conf_ln_b[i], sc_conv_w[i])
        yb = _gla(u3, gla_w_g2[i], gla_b_g2[i], gla_norm_g[i])
        yc = _dilated(u3, bias_tab)
        x1, x1b, x1p, ids, gates = _outproj(i, xf, ya.reshape(t, -1), yb.reshape(t, -1), yc.reshape(t, -1),
                                            yd.reshape(t, -1), vec(mix_scale), w_out_bf, vec(ln1_g), vec(ln1_b),
                                            rw_cat[i], rb[i])
        pos, tile_tab = _plan(ids)
        pos_flat = pos[:, :TOP_K].T.reshape(-1)
        tile_end = tile_tab[:n_tiles, 0]
        n_active = jnp.sum((tile_end < N_EXPERTS).astype(I32)).reshape(1)
        tile_expert = jnp.minimum(tile_end, N_EXPERTS - 1)
        xs = _dispatch(pos_flat, x1p, n_rows)
        ys = _experts(i, tile_expert, n_active, xs, exp_w_gate, exp_w_up, exp_w_down)
        xf, xb = _final(i, pos_flat, x1, x1b, gates, p3, ple_wg_bf, vec(ple_b_gate), ple_wp_bf, vec(ln2_g), vec(ln2_b), ys)
    return xf.reshape(bsz, seq, d)
```

```python
import functools
import math

import jax
import jax.numpy as jnp
from jax import lax
from jax.experimental import pallas as pl
from jax.experimental.pallas import tpu as pltpu

F32 = jnp.float32
BF16 = jnp.bfloat16
I32 = jnp.int32
HIGHEST = lax.Precision.HIGHEST

D_MODEL = 2048
DEPTH = 2
PLE_DIM = 256
W_CONF = 512
W_GLA = 512
W_DIL = 512
W_SC = 512
CONF_KERNEL = 31
GLA_HEADS = 4
GLA_DV = 128
GLA_DK = 64
GLA_RANK = 16
GLA_TAU = 16.0
DIL_HEADS = 8
DIL_HD = 64
DIL_BRANCHES = ((128, 1), (512, 4), (2048, 16))
DIL_BLOCK = 128
REL_BUCKETS = 32
REL_MAX_DIST = 2048
SC_KERNEL = 3
N_GROUPS = 4
EXPERTS_PER_GROUP = 8
N_EXPERTS = 32
TOP_K = 2
D_EXPERT = 256
DN_ALPHA = (2 * DEPTH) ** 0.25
LN_EPS = 1e-5
RMS_EPS = 1e-6
IN_SIZES = (2 * W_CONF, GLA_HEADS * GLA_DK, GLA_HEADS * GLA_DK, W_GLA, GLA_RANK, W_GLA,
            W_DIL, W_DIL, W_DIL, W_SC, W_SC, W_SC)

LANES = 128
SUBLANES = 8
VMEM_LIMIT = 56 * 1024 * 1024

GL_PAD = 512
COL_A = 0
COL_GQ = 1024
COL_GK = 1280
COL_GV = 1536
COL_GL = 2048
COL_GR = 2560
COL_CQ = 3072
COL_CK = 3584
COL_CV = 4096
COL_SB = 4608
COL_SC = 5120
COL_SH = 5632
U_COLS = 6144

MOE_TILE = 512
EXPERT_SPLIT = 1


def _cparams(n_axes):
    return pltpu.CompilerParams(dimension_semantics=("arbitrary",) * n_axes, vmem_limit_bytes=VMEM_LIMIT)


def _layer_norm(x, g, b):
    mu = jnp.mean(x, axis=-1, keepdims=True)
    xc = x - mu
    var = jnp.mean(xc * xc, axis=-1, keepdims=True)
    return xc * lax.rsqrt(var + LN_EPS) * g + b


U32 = jnp.uint32


def _pack_bf16_pairs(x):
    n = x.shape[1] // 2
    bits = lax.bitcast_convert_type(x.astype(BF16).astype(F32), U32)
    return (bits[:, :n] >> 16) | (bits[:, n:] & jnp.uint32(0xFFFF0000))


def _unpack_bf16_pairs(w):
    lo = lax.bitcast_convert_type(w << 16, F32)
    hi = lax.bitcast_convert_type(w & jnp.uint32(0xFFFF0000), F32)
    return jnp.concatenate([lo, hi], axis=1)


def _store_row_tiles(ref, packed):
    rows, width = packed.shape
    assert width == SUBLANES * LANES
    for s in range(SUBLANES):
        ref[pl.ds(s, rows, stride=SUBLANES), :] = packed[:, s * LANES:(s + 1) * LANES]


def _load_row_tiles(ref, rows):
    return jnp.concatenate([ref[pl.ds(s, rows, stride=SUBLANES), :] for s in range(SUBLANES)], axis=1)


def _sigmoid(x):
    return 1.0 / (1.0 + jnp.exp(-x))


def _silu(x):
    return x * _sigmoid(x)


def _matmul_kernel(x_ref, w_ref, o_ref):
    o_ref[...] = jnp.dot(x_ref[...].astype(BF16), w_ref[0],
                         preferred_element_type=F32).astype(o_ref.dtype)


def _in_proj(x, w_all, layer, tm, tn, out_dtype):
    m, k = x.shape
    n = w_all.shape[2]
    return pl.pallas_call(
        _matmul_kernel,
        grid=(n // tn, m // tm),
        in_specs=[pl.BlockSpec((tm, k), lambda j, i: (i, 0)),
                  pl.BlockSpec((1, k, tn), lambda j, i: (layer, 0, j))],
        out_specs=pl.BlockSpec((tm, tn), lambda j, i: (i, j)),
        out_shape=jax.ShapeDtypeStruct((m, n), out_dtype),
        compiler_params=_cparams(2),
        name="in_proj",
    )(x, w_all)


CONV_TS = 1024
CONV_RC = 64
CONF_HIST = 32
SC_HIST = 8


def _conv_kernel(a_ref, gate_ref, sb_ref, sc_ref, sh_ref, cw_ref, cb_ref, lg_ref, lb_ref, sw_ref,
                 ya_ref, yd_ref, ha, hd, hs):
    s = pl.program_id(1)
    ts = a_ref.shape[1]
    n_ext = CONF_HIST + ts

    @pl.when(s == 0)
    def _():
        ha[0:CONF_HIST, :] = jnp.zeros((CONF_HIST, W_CONF), F32)
        hd[0:SC_HIST, :] = jnp.zeros((SC_HIST, W_SC), F32)

    ha[CONF_HIST:CONF_HIST + ts, :] = a_ref[0] * _sigmoid(gate_ref[0])
    hd[SC_HIST:SC_HIST + ts, :] = sc_ref[0] * sh_ref[0]

    cb = cb_ref[...]
    lg = lg_ref[...]
    lb = lb_ref[...]
    off_a = CONF_HIST - (CONF_KERNEL - 1)
    h_all = ha[...]
    for sh in range(1, SUBLANES):
        h_all = pltpu.roll(h_all, n_ext - 1, 0)
        hs[sh - 1] = h_all
    for c in range(ts // CONV_RC):
        r0 = c * CONV_RC
        acc = jnp.zeros((CONV_RC, W_CONF), F32)
        for j in range(CONF_KERNEL):
            sh = (off_a + j) % SUBLANES
            lo = r0 + off_a + j - sh
            rows = ha[lo:lo + CONV_RC, :] if sh == 0 else hs[sh - 1, lo:lo + CONV_RC, :]
            acc = acc + cw_ref[j:j + 1, :] * rows
        hh = _layer_norm(acc + cb, lg, lb)
        ya_ref[0, r0:r0 + CONV_RC, :] = _silu(hh)

    off_d = SC_HIST - (SC_KERNEL - 1)
    accd = jnp.zeros((ts, W_SC), F32)
    for j in range(SC_KERNEL):
        accd = accd + sw_ref[j:j + 1, :] * hd[off_d + j:off_d + j + ts, :]
    yd_ref[0] = sb_ref[0] * accd

    ha[0:CONF_HIST, :] = ha[ts:ts + CONF_HIST, :]
    hd[0:SC_HIST, :] = hd[ts:ts + SC_HIST, :]


def _conv_mixers(u3, conf_w, conf_b, conf_lg, conf_lb, sc_w):
    bsz, seq, _ = u3.shape
    ts = CONV_TS
    wblk = W_CONF

    def col(c):
        return pl.BlockSpec((1, ts, wblk), lambda b, s, c=c: (b, s, c // wblk))

    def full(shape):
        return pl.BlockSpec(shape, lambda b, s: (0,) * len(shape))

    cw = jnp.zeros((32, W_CONF), F32).at[:CONF_KERNEL].set(conf_w)
    sw = jnp.zeros((8, W_SC), F32).at[:SC_KERNEL].set(sc_w)
    out_spec = pl.BlockSpec((1, ts, wblk), lambda b, s: (b, s, 0))
    return pl.pallas_call(
        _conv_kernel,
        grid=(bsz, seq // ts),
        in_specs=[col(COL_A), col(COL_A + W_CONF), col(COL_SB), col(COL_SC), col(COL_SH),
                  full((32, W_CONF)), full((1, W_CONF)), full((1, W_CONF)), full((1, W_CONF)), full((8, W_SC))],
        out_specs=[out_spec, out_spec],
        out_shape=[jax.ShapeDtypeStruct((bsz, seq, W_CONF), F32), jax.ShapeDtypeStruct((bsz, seq, W_SC), F32)],
        scratch_shapes=[pltpu.VMEM((CONF_HIST + ts, W_CONF), F32), pltpu.VMEM((SC_HIST + ts, W_SC), F32),
                        pltpu.VMEM((SUBLANES - 1, CONF_HIST + ts, W_CONF), F32)],
        compiler_params=_cparams(2),
        name="conv_mixers",
    )(u3, u3, u3, u3, u3, cw, conf_b.reshape(1, -1), conf_lg.reshape(1, -1), conf_lb.reshape(1, -1), sw)


GLA_BLK = 128
GLA_SUB = 64
GLA_UNROLL = 8


def _split3(x):
    hi = x.astype(BF16)
    r1 = x - hi.astype(F32)
    mid = r1.astype(BF16)
    lo = (r1 - mid.astype(F32)).astype(BF16)
    return hi, mid, lo


def _gla_kernel(q_ref, k_ref, v_ref, gl_ref, r_ref, wg2h_ref, wg2l_ref, bg2_ref, ng_ref, tril_ref, o_ref, la_ref):
    seq = q_ref.shape[1]
    nblk = seq // GLA_BLK
    nsub = GLA_BLK // GLA_SUB
    assert nsub == 2 and nblk % GLA_UNROLL == 0

    lane = lax.broadcasted_iota(I32, (1, LANES), 1)
    head_mask = [(lane < GLA_DK).astype(F32), (lane >= GLA_DK).astype(F32)]
    row = lax.broadcasted_iota(I32, (GLA_BLK, GLA_BLK), 0)
    col = lax.broadcasted_iota(I32, (GLA_BLK, GLA_BLK), 1)
    sub_shift = GLA_SUB.bit_length() - 1
    diag_mask = ((row >> sub_shift) == (col >> sub_shift)) & (col <= row)
    second_half = lax.broadcasted_iota(I32, (GLA_BLK, 1), 0) >= GLA_SUB
    scale = GLA_DK ** -0.5
    ng = ng_ref[...]
    tril = tril_ref[...]

    gl = gl_ref[0]
    gl_hi = gl.astype(BF16)
    gl_lo = (gl - gl_hi.astype(F32)).astype(BF16)
    wh = wg2h_ref[...]
    z = (jnp.dot(gl_hi, wh, preferred_element_type=F32) + jnp.dot(gl_lo, wh, preferred_element_type=F32)
         + jnp.dot(gl_hi, wg2l_ref[...], preferred_element_type=F32) + bg2_ref[...])
    la_ref[...] = (jnp.minimum(z, 0.0) - jnp.log(1.0 + jnp.exp(-jnp.abs(z)))) * (1.0 / GLA_TAU)

    nt = (((1,), (1,)), ((), ()))

    def body(it, carry):
        grp = range(GLA_UNROLL)
        r0 = [pl.multiple_of((it * GLA_UNROLL + j) * GLA_BLK, GLA_BLK) for j in grp]
        q = [q_ref[0, pl.ds(r, GLA_BLK), :] * scale for r in r0]
        k = [k_ref[0, pl.ds(r, GLA_BLK), :] for r in r0]
        v = [v_ref[0, pl.ds(r, GLA_BLK), :].astype(BF16) for r in r0]
        b = []
        for r in r0:
            acc = None
            for part in _split3(la_ref[pl.ds(r, GLA_BLK), :]):
                t = jnp.dot(tril, part, preferred_element_type=F32)
                acc = t if acc is None else acc + t
            b.append(acc)

        s_in, q_st, kk, b_last = [], [], [], []
        for j in grp:
            b3 = b[j].reshape(nsub, GLA_SUB, LANES)
            mid = jnp.broadcast_to(b3[:, GLA_SUB // 2 - 1:GLA_SUB // 2, :], b3.shape).reshape(GLA_BLK, LANES)
            qd = q[j] * jnp.exp(b[j] - mid)
            kd = (k[j] * jnp.exp(mid - b[j])).astype(BF16)
            e0 = b[j][GLA_SUB - 1:GLA_SUB, :]
            q_off = jnp.where(second_half, q[j] * jnp.exp(jnp.minimum(b[j] - e0, 0.0)), 0.0)
            k_off = jnp.where(second_half, 0.0, k[j] * jnp.exp(jnp.minimum(e0 - b[j], 0.0))).astype(BF16)
            s_in.append((qd, kd, q_off, k_off))
            q_st.append(q[j] * jnp.exp(b[j]))
            b_last.append(b[j][GLA_BLK - 1:GLA_BLK, :])
            kk.append((k[j] * jnp.exp(b_last[j] - b[j])).astype(BF16))

        scores = []
        for j in grp:
            qd, kd, q_off, k_off = s_in[j]
            per_head = []
            for h in range(2):
                hm = head_mask[h]
                s_d = lax.dot_general((qd * hm).astype(BF16), kd, nt, preferred_element_type=F32)
                s_o = lax.dot_general((q_off * hm).astype(BF16), k_off, nt, preferred_element_type=F32)
                per_head.append((jnp.where(diag_mask, s_d, 0.0) + s_o).astype(BF16))
            scores.append(per_head)

        states = []
        st = carry
        for j in grp:
            states.append(st.astype(BF16))
            upd = lax.dot_general(kk[j], v[j], (((0,), (0,)), ((), ())), preferred_element_type=F32)
            decay = jnp.transpose(jnp.broadcast_to(jnp.exp(b_last[j]), (GLA_BLK, LANES)))
            st = st * jnp.concatenate([decay, decay], axis=1) + upd

        for j in grp:
            outs = []
            for h in range(2):
                lhs = jnp.concatenate([scores[j][h], (q_st[j] * head_mask[h]).astype(BF16)], axis=1)
                rhs = jnp.concatenate([v[j][:, h * GLA_DV:(h + 1) * GLA_DV],
                                       states[j][:, h * GLA_DV:(h + 1) * GLA_DV]], axis=0)
                o_h = jnp.dot(lhs, rhs, preferred_element_type=F32)
                outs.append(o_h * lax.rsqrt(jnp.mean(o_h * o_h, axis=-1, keepdims=True) + RMS_EPS))
            o = jnp.concatenate(outs, axis=1) * ng
            o_ref[0, pl.ds(r0[j], GLA_BLK), :] = o * _silu(r_ref[0, pl.ds(r0[j], GLA_BLK), :])
        return st

    lax.fori_loop(0, nblk // GLA_UNROLL, body, jnp.zeros((2 * GLA_DK, 2 * GLA_DV), F32))


def _gla(u3, w_g2, b_g2, norm_g):
    bsz, seq, _ = u3.shape
    npair = GLA_HEADS // 2
    kw = 2 * GLA_DK
    vw = 2 * GLA_DV

    def col(c, w):
        return pl.BlockSpec((1, seq, w), lambda b, p, c=c, w=w: (b, 0, c // w + p))

    wg2 = jnp.zeros((LANES, GLA_HEADS * GLA_DK), F32).at[:GLA_RANK].set(w_g2)
    wg2_hi = wg2.astype(BF16)
    wg2_lo = (wg2 - wg2_hi.astype(F32)).astype(BF16)
    ng = jnp.concatenate([norm_g, norm_g]).reshape(1, vw)
    tril = jnp.tril(jnp.ones((GLA_BLK, GLA_BLK), BF16))
    return pl.pallas_call(
        _gla_kernel,
        grid=(bsz, npair),
        in_specs=[col(COL_GQ, kw), col(COL_GK, kw), col(COL_GV, vw),
                  pl.BlockSpec((1, seq, LANES), lambda b, p: (b, 0, COL_GL // LANES)),
                  col(COL_GR, vw),
                  pl.BlockSpec((LANES, kw), lambda b, p: (0, p)),
                  pl.BlockSpec((LANES, kw), lambda b, p: (0, p)),
                  pl.BlockSpec((1, kw), lambda b, p: (0, p)),
                  pl.BlockSpec((1, vw), lambda b, p: (0, 0)),
                  pl.BlockSpec((GLA_BLK, GLA_BLK), lambda b, p: (0, 0))],
        out_specs=pl.BlockSpec((1, seq, vw), lambda b, p: (b, 0, p)),
        out_shape=jax.ShapeDtypeStruct((bsz, seq, W_GLA), F32),
        scratch_shapes=[pltpu.VMEM((seq, LANES), F32)],
        compiler_params=_cparams(2),
        name="gla",
    )(u3, u3, u3, u3, u3, wg2_hi, wg2_lo, b_g2.reshape(1, -1), ng, tril)


def _t5_bucket(dist):
    max_exact = REL_BUCKETS // 2
    large = max_exact + (jnp.log(jnp.maximum(dist, 1).astype(F32) / max_exact)
                         / math.log(REL_MAX_DIST / max_exact) * (REL_BUCKETS - max_exact)).astype(I32)
    large = jnp.minimum(large, REL_BUCKETS - 1)
    return jnp.where(dist < max_exact, dist, large)


LOG2E = math.log2(math.e)
DIL_STAGE = 4
DIL_GROUP = 8


def _bias_kernel(rel_ref, bucket_ref, o_ref):
    h = pl.program_id(1)
    blk = DIL_BLOCK
    bucket = bucket_ref[0]
    acc = jnp.zeros(bucket.shape, F32)
    for b in range(REL_BUCKETS):
        acc = jnp.where(bucket == b, rel_ref[b, h], acc)
    acc = acc * LOG2E
    qi = lax.broadcasted_iota(I32, (blk, 2 * blk), 0)
    kj = lax.broadcasted_iota(I32, (blk, 2 * blk), 1)
    steps = qi + blk - kj
    in_window = (steps >= 0) & (steps <= blk)
    neg = jnp.float32(-jnp.inf)
    o_ref[0, 0, 0] = jnp.where(in_window, acc, neg)
    o_ref[0, 0, 1] = jnp.where(in_window & (kj >= blk), acc, neg)


def _bias_tables(rel_bias):
    assert all(window // dil == DIL_BLOCK for window, dil in DIL_BRANCHES)
    qi = jnp.arange(DIL_BLOCK)[:, None]
    kj = jnp.arange(2 * DIL_BLOCK)[None, :]
    steps = jnp.maximum(qi + DIL_BLOCK - kj, 0)
    buckets = jnp.stack([_t5_bucket(steps * dil) for _, dil in DIL_BRANCHES]).astype(I32)
    nbr = len(DIL_BRANCHES)
    return pl.pallas_call(
        _bias_kernel,
        grid=(nbr, DIL_HEADS),
        in_specs=[pl.BlockSpec(memory_space=pltpu.SMEM),
                  pl.BlockSpec((1, DIL_BLOCK, 2 * DIL_BLOCK), lambda i, h: (i, 0, 0))],
        out_specs=pl.BlockSpec((1, 1, 2, DIL_BLOCK, 2 * DIL_BLOCK), lambda i, h: (i, h, 0, 0, 0)),
        out_shape=jax.ShapeDtypeStruct((nbr, DIL_HEADS, 2, DIL_BLOCK, 2 * DIL_BLOCK), F32),
        compiler_params=_cparams(2),
        name="rel_bias_tables",
    )(rel_bias, buckets)


def _dil_kernel(q_ref, k_ref, v_ref, bias_ref, o_ref, qs, ks, vs, ob, lb, stage):
    seq = q_ref.shape[1]
    blk = DIL_BLOCK
    scale = DIL_HD ** -0.5 * LOG2E
    lane = lax.broadcasted_iota(I32, (1, LANES), 1)
    first = lane < DIL_HD
    head_mask = [first.astype(F32), 1.0 - first.astype(F32)]

    ks[0:blk, :] = jnp.zeros((blk, LANES), BF16)
    vs[0:blk, :] = jnp.zeros((blk, LANES), BF16)

    for bi, (window, dil) in enumerate(DIL_BRANCHES):
        sub_len = seq // dil
        nb = sub_len // blk
        nb_shift = nb.bit_length() - 1
        for r in range(dil):
            dst = slice(r * sub_len, (r + 1) * sub_len)
            if dil == 1:
                q, k, v = q_ref[0], k_ref[0], v_ref[0]
            elif dil == DIL_STAGE:
                rows = pl.ds(r, sub_len, stride=dil)
                q, k, v = q_ref[0, rows, :], k_ref[0, rows, :], v_ref[0, rows, :]
                stage[0, dst, :] = q
                stage[1, dst, :] = k
                stage[2, dst, :] = v
            else:
                assert dil == DIL_STAGE * DIL_STAGE
                rows = pl.ds((r % DIL_STAGE) * (seq // DIL_STAGE) + r // DIL_STAGE, sub_len, stride=DIL_STAGE)
                q, k, v = stage[0, rows, :], stage[1, rows, :], stage[2, rows, :]
            q = q * scale
            for h in range(2):
                qs[h, dst, :] = (q * head_mask[h]).astype(BF16)
            ks[blk + r * sub_len:blk + (r + 1) * sub_len, :] = k.astype(BF16)
            vs[blk + r * sub_len:blk + (r + 1) * sub_len, :] = v.astype(BF16)

        def body(it, carry, bi=bi, dil=dil, nb=nb, nb_shift=nb_shift):
            grp = range(DIL_GROUP)
            g = [it * DIL_GROUP + gg for gg in grp]
            n = [gi & (nb - 1) for gi in g]
            g0 = [pl.multiple_of(gi * blk, blk) for gi in g]
            kb = [ks[pl.ds(s0, 2 * blk), :] for s0 in g0]
            sc = [[lax.dot_general(qs[h, pl.ds(g0[gg], blk), :], kb[gg], (((1,), (1,)), ((), ())),
                                   preferred_element_type=F32) + bias_ref[bi, h, jnp.where(n[gg] == 0, 1, 0)]
                   for h in range(2)] for gg in grp]
            stats = []
            for gg in grp:
                per_head = []
                for h in range(2):
                    m = jnp.max(sc[gg][h], axis=-1, keepdims=True)
                    p = jnp.exp2(sc[gg][h] - m)
                    l = jnp.sum(p, axis=-1, keepdims=True)
                    per_head.append((p.astype(BF16), m, l))
                stats.append(per_head)
            for gg in grp:
                vb = vs[pl.ds(g0[gg], 2 * blk), :]
                o_acc = None
                l_acc = None
                for h in range(2):
                    p, m, l = stats[gg][h]
                    o_h = jnp.dot(p, vb, preferred_element_type=F32) * (1.0 / l)
                    lse = jnp.broadcast_to(m + jnp.log2(l), (blk, LANES))
                    if h == 0:
                        o_acc, l_acc = o_h, lse
                    else:
                        o_acc = jnp.where(first, o_acc, o_h)
                        l_acc = jnp.where(first, l_acc, lse)
                if dil > 1:
                    dst = pl.ds(n[gg] * (blk * dil) + (g[gg] >> nb_shift), blk, stride=dil)
                else:
                    dst = pl.ds(g0[gg], blk)
                ob[bi, dst, :] = o_acc
                lb[bi, dst, :] = l_acc
            return carry

        lax.fori_loop(0, seq // blk // DIL_GROUP, body, 0)

    nbr = len(DIL_BRANCHES)
    mc = 256
    for c in range(seq // mc):
        rs = slice(c * mc, (c + 1) * mc)
        ls = [lb[i, rs, :] for i in range(nbr)]
        m = functools.reduce(jnp.maximum, ls)
        ws = [jnp.exp2(l - m) for l in ls]
        num = functools.reduce(lambda a, b: a + b, [w * ob[i, rs, :] for i, w in enumerate(ws)])
        den = functools.reduce(lambda a, b: a + b, ws)
        o_ref[0, rs, :] = num / den


def _dilated(u3, bias_tab):
    bsz, seq, _ = u3.shape
    npair = DIL_HEADS // 2
    nbr = len(DIL_BRANCHES)

    def col(c):
        return pl.BlockSpec((1, seq, LANES), lambda b, p, c=c: (b, 0, c // LANES + p))

    return pl.pallas_call(
        _dil_kernel,
        grid=(bsz, npair),
        in_specs=[col(COL_CQ), col(COL_CK), col(COL_CV),
                  pl.BlockSpec((nbr, 2, 2, DIL_BLOCK, 2 * DIL_BLOCK), lambda b, p: (0, p, 0, 0, 0))],
        out_specs=pl.BlockSpec((1, seq, LANES), lambda b, p: (b, 0, p)),
        out_shape=jax.ShapeDtypeStruct((bsz, seq, W_DIL), F32),
        scratch_shapes=[pltpu.VMEM((2, seq, LANES), BF16),
                        pltpu.VMEM((DIL_BLOCK + seq, LANES), BF16),
                        pltpu.VMEM((DIL_BLOCK + seq, LANES), BF16),
                        pltpu.VMEM((nbr, seq, LANES), F32),
                        pltpu.VMEM((nbr, seq, LANES), F32),
                        pltpu.VMEM((3, seq, LANES), F32)],
        compiler_params=_cparams(2),
        name="dilated_attn",
    )(u3, u3, u3, bias_tab)


TOK_TILE = 256
OUT_TILE = 512
OUT_SPLIT = 2
COMBINE_TILE = 256
FINAL_SPLIT = 2
GATHER_SLOTS = 3
PLAN_TILE = 2048


def _outproj_kernel(x_ref, ya_ref, yb_ref, yc_ref, yd_ref, ms_ref, wo_ref, g_ref, b_ref, rw_ref, rb_ref,
                    x1_ref, x1b_ref, x1p_ref, ids_ref, gates_ref):
    tm = x_ref.shape[0]
    rows = tm // OUT_SPLIT
    parts = [slice(pi * rows, (pi + 1) * rows) for pi in range(OUT_SPLIT)]
    accs = []
    for rs in parts:
        acc = None
        for gi, y_ref in enumerate((ya_ref, yb_ref, yc_ref, yd_ref)):
            lo = gi * W_CONF
            y = (y_ref[rs, :] * ms_ref[0, :, lo:lo + W_CONF]).astype(BF16)
            part = jnp.dot(y, wo_ref[0, lo:lo + W_CONF, :], preferred_element_type=F32)
            acc = part if acc is None else acc + part
        accs.append(acc)

    logits = []
    for rs, acc in zip(parts, accs):
        x1 = _layer_norm(DN_ALPHA * x_ref[rs, :] + acc, g_ref[0], b_ref[0])
        x1_ref[rs, :] = x1
        x1_hi = x1.astype(BF16)
        x1b_ref[rs, :] = x1_hi
        _store_row_tiles(x1p_ref.at[pl.ds(rs.start * SUBLANES, rows * SUBLANES)], _pack_bf16_pairs(x1))
        x1_lo = (x1 - x1_hi.astype(F32)).astype(BF16)
        hh_hl = jnp.dot(x1_hi, rw_ref[...], preferred_element_type=F32)
        logits.append(hh_hl[:, :LANES] + hh_hl[:, LANES:]
                      + jnp.dot(x1_lo, rw_ref[:, :LANES], preferred_element_type=F32) + rb_ref[...])

    lane = lax.broadcasted_iota(I32, (rows, LANES), 1)
    neg = jnp.float32(-jnp.inf)
    for rs, lg in zip(parts, logits):
        gl = jnp.where(lane < N_GROUPS, lg, neg)
        gmax = jnp.max(gl, axis=-1, keepdims=True)
        g_top = 1.0 / jnp.sum(jnp.exp(gl - gmax), axis=-1, keepdims=True)
        g_idx = jnp.min(jnp.where(gl == gmax, lane, LANES), axis=-1, keepdims=True)
        e_lo = N_GROUPS + g_idx * EXPERTS_PER_GROUP
        el = jnp.where((lane >= e_lo) & (lane < e_lo + EXPERTS_PER_GROUP), lg, neg)
        m1 = jnp.max(el, axis=-1, keepdims=True)
        i1 = jnp.min(jnp.where(el == m1, lane, LANES), axis=-1, keepdims=True)
        el2 = jnp.where(lane == i1, neg, el)
        m2 = jnp.max(el2, axis=-1, keepdims=True)
        i2 = jnp.min(jnp.where(el2 == m2, lane, LANES), axis=-1, keepdims=True)
        e21 = jnp.exp(m2 - m1)
        w1 = g_top / (1.0 + e21)
        w2 = g_top * e21 / (1.0 + e21)
        ids_ref[rs, :] = jnp.where(lane == 0, i1 - N_GROUPS, jnp.where(lane == 1, i2 - N_GROUPS, N_EXPERTS))
        gates_ref[rs, :] = jnp.where(lane == 0, w1, jnp.where(lane == 1, w2, 0.0))


def _outproj(layer, x, ya, yb, yc, yd, mix_scale, w_out_bf, ln_g, ln_b, rw_cat, rb):
    t, d = x.shape
    tm = OUT_TILE

    def row(w):
        return pl.BlockSpec((tm, w), lambda i: (i, 0))

    def full(shape):
        return pl.BlockSpec(shape, lambda i: (0,) * len(shape))

    def per_layer(shape):
        return pl.BlockSpec((1,) + shape, lambda i: (layer,) + (0,) * len(shape))

    return pl.pallas_call(
        _outproj_kernel,
        grid=(t // tm,),
        in_specs=[row(d), row(W_CONF), row(W_GLA), row(W_DIL), row(W_SC), per_layer((1, d)), per_layer((d, d)),
                  per_layer((1, d)), per_layer((1, d)), full((d, 2 * LANES)), full((1, LANES))],
        out_specs=[row(d), row(d), pl.BlockSpec((tm * SUBLANES, LANES), lambda i: (i, 0)), row(LANES), row(LANES)],
        out_shape=[jax.ShapeDtypeStruct((t, d), F32), jax.ShapeDtypeStruct((t, d), BF16),
                   jax.ShapeDtypeStruct((t * SUBLANES, LANES), U32),
                   jax.ShapeDtypeStruct((t, LANES), I32), jax.ShapeDtypeStruct((t, LANES), F32)],
        compiler_params=_cparams(1),
        name="out_proj_ln_router",
    )(x, ya, yb, yc, yd, mix_scale, w_out_bf, ln_g, ln_b, rw_cat, rb)


def _plan_kernel(ids_ref, tril_ref, upper_ref, pos_ref, tile_ref, cnt, carry, start):
    phase = pl.program_id(0)
    i = pl.program_id(1)
    tm = ids_ref.shape[0]
    lane = lax.broadcasted_iota(I32, (tm, LANES), 1)
    ids = ids_ref[...]
    e1 = ids[:, 0:1]
    e2 = ids[:, 1:2]
    oh1 = (lane == e1).astype(F32)
    oh2 = (lane == e2).astype(F32)
    oh = oh1 + oh2

    @pl.when((phase == 0) & (i == 0))
    def _():
        cnt[...] = jnp.zeros(cnt.shape, F32)

    @pl.when(phase == 0)
    def _():
        cnt[...] = cnt[...] + jnp.sum(oh, axis=0, keepdims=True)

    @pl.when((phase == 1) & (i == 0))
    def _():
        padded = jnp.floor((cnt[...] + (MOE_TILE - 1)) * (1.0 / MOE_TILE)) * MOE_TILE
        p8 = jnp.broadcast_to(padded, (SUBLANES, LANES))
        st = jnp.dot(p8, upper_ref[...], precision=HIGHEST, preferred_element_type=F32)
        start[...] = st[0:1, :]
        carry[...] = jnp.zeros(carry.shape, F32)
        end = st[0:1, :] + padded
        trow = lax.broadcasted_iota(I32, (LANES, LANES), 0).astype(F32) * MOE_TILE
        tlane = lax.broadcasted_iota(I32, (LANES, LANES), 1)
        done = jnp.where((end <= trow) & (tlane < N_EXPERTS), 1.0, 0.0)
        tile_ref[...] = jnp.broadcast_to(jnp.sum(done, axis=-1, keepdims=True), (LANES, LANES)).astype(I32)

    @pl.when(phase == 1)
    def _():
        incl = jnp.dot(tril_ref[...], oh.astype(BF16), preferred_element_type=F32) + carry[...]
        base = start[...] + incl - oh
        p1 = jnp.sum(oh1 * base, axis=-1, keepdims=True)
        p2 = jnp.sum(oh2 * base, axis=-1, keepdims=True)
        pos_ref[...] = jnp.where(lane == 0, p1, jnp.where(lane == 1, p2, 0.0)).astype(I32)
        carry[...] = carry[...] + jnp.sum(oh, axis=0, keepdims=True)


def _plan(ids):
    t = ids.shape[0]
    tm = PLAN_TILE
    tril = jnp.tril(jnp.ones((tm, tm), BF16))
    upper = jnp.triu(jnp.ones((LANES, LANES), F32), k=1)
    return pl.pallas_call(
        _plan_kernel,
        grid=(2, t // tm),
        in_specs=[pl.BlockSpec((tm, LANES), lambda ph, i: (i, 0)),
                  pl.BlockSpec((tm, tm), lambda ph, i: (0, 0)),
                  pl.BlockSpec((LANES, LANES), lambda ph, i: (0, 0))],
        out_specs=[pl.BlockSpec((tm, LANES), lambda ph, i: (i * ph, 0)),
                   pl.BlockSpec((LANES, LANES), lambda ph, i: (0, 0))],
        out_shape=[jax.ShapeDtypeStruct((t, LANES), I32), jax.ShapeDtypeStruct((LANES, LANES), I32)],
        scratch_shapes=[pltpu.VMEM((1, LANES), F32), pltpu.VMEM((1, LANES), F32), pltpu.VMEM((1, LANES), F32)],
        compiler_params=_cparams(2),
        name="route_plan",
    )(ids, tril, upper)


DISPATCH_TILE = 512
DISPATCH_SLOTS = 3


def _dispatch_kernel(pos_ref, x_ref, xs_ref, stage, sem):
    i = pl.program_id(0)
    n_steps = pl.num_programs(0)
    tm = x_ref.shape[0] // SUBLANES
    t = n_steps * tm
    slot = lax.rem(i, DISPATCH_SLOTS)

    def wait_slot(s):
        for _ in range(TOP_K):
            pltpu.make_async_copy(stage.at[s], xs_ref.at[pl.ds(0, tm * SUBLANES)], sem.at[s]).wait()

    @pl.when(i >= DISPATCH_SLOTS)
    def _():
        wait_slot(slot)

    stage[slot] = x_ref[...]
    for m in range(tm):
        for k in range(TOP_K):
            dst = pl.multiple_of(pos_ref[k * t + i * tm + m] * SUBLANES, SUBLANES)
            pltpu.make_async_copy(stage.at[slot, pl.ds(m * SUBLANES, SUBLANES)], xs_ref.at[pl.ds(dst, SUBLANES)],
                                  sem.at[slot]).start(priority=k)

    @pl.when(i == n_steps - 1)
    def _():
        for s in range(DISPATCH_SLOTS):
            wait_slot(s)


def _dispatch(pos_flat, x1p, n_rows):
    t = x1p.shape[0] // SUBLANES
    tm = DISPATCH_TILE
    assert t // tm >= DISPATCH_SLOTS
    return pl.pallas_call(
        _dispatch_kernel,
        grid_spec=pltpu.PrefetchScalarGridSpec(
            num_scalar_prefetch=1,
            grid=(t // tm,),
            in_specs=[pl.BlockSpec((tm * SUBLANES, LANES), lambda i, pos: (i, 0))],
            out_specs=pl.BlockSpec(memory_space=pl.ANY),
            scratch_shapes=[pltpu.VMEM((DISPATCH_SLOTS, tm * SUBLANES, LANES), x1p.dtype),
                            pltpu.SemaphoreType.DMA((DISPATCH_SLOTS,))],
        ),
        out_shape=jax.ShapeDtypeStruct((n_rows * SUBLANES, LANES), x1p.dtype),
        compiler_params=_cparams(1),
        name="moe_dispatch",
    )(pos_flat, x1p)


def _expert_kernel(layer, te_ref, na_ref, first_ref, run_ref, next2_ref, runs_ref, nruns_ref,
                   xs_ref, wg_hbm, wu_hbm, wd_hbm, ys_ref, wg_buf, wu_buf, wd_buf, wg_bf, wu_bf, wd_bf, sem):
    i = pl.program_id(0)

    def copies(e, s):
        return (pltpu.make_async_copy(wg_hbm.at[layer, e], wg_buf.at[s], sem.at[s]),
                pltpu.make_async_copy(wu_hbm.at[layer, e], wu_buf.at[s], sem.at[s]),
                pltpu.make_async_copy(wd_hbm.at[layer, e], wd_buf.at[s], sem.at[s]))

    def fetch(e, s):
        for c in copies(e, s):
            c.start()

    @pl.when(i == 0)
    def _():
        fetch(runs_ref[0], 0)

        @pl.when(nruns_ref[0] > 1)
        def _():
            fetch(runs_ref[1], 1)

    active = i < na_ref[0]

    @pl.when(active & (first_ref[i] == 1))
    def _():
        s = run_ref[i] % 2
        for c in copies(te_ref[i], s):
            c.wait()
        wg_bf[...] = wg_buf[s].astype(BF16)
        wu_bf[...] = wu_buf[s].astype(BF16)
        wd_bf[...] = wd_buf[s].astype(BF16)

        @pl.when(next2_ref[i] >= 0)
        def _():
            fetch(next2_ref[i], s)

    @pl.when(active)
    def _():
        rows = MOE_TILE // EXPERT_SPLIT
        views = [pl.ds(pi * rows * SUBLANES, rows * SUBLANES) for pi in range(EXPERT_SPLIT)]
        xb = [_unpack_bf16_pairs(_load_row_tiles(xs_ref.at[v], rows)).astype(BF16) for v in views]
        gu = [(jnp.dot(x, wg_bf[...], preferred_element_type=F32), jnp.dot(x, wu_bf[...], preferred_element_type=F32))
              for x in xb]
        h = [(_silu(g) * u).astype(BF16) for g, u in gu]
        y = [jnp.dot(hh, wd_bf[...], preferred_element_type=F32) for hh in h]
        for v, yy in zip(views, y):
            _store_row_tiles(ys_ref.at[v], _pack_bf16_pairs(yy))


def _experts(layer, tile_expert, n_active, xs, w_gate, w_up, w_down):
    n_rows = xs.shape[0] // SUBLANES
    d = LANES
    n_tiles = n_rows // MOE_TILE
    d_model = w_gate.shape[2]

    tile = jnp.arange(n_tiles, dtype=I32)
    active = tile < n_active[0]
    first = active & ((tile == 0) | (tile_expert != jnp.roll(tile_expert, 1)))
    run_of_tile = jnp.cumsum(first.astype(I32)) - 1
    n_runs = jnp.sum(first.astype(I32)).reshape(1)
    is_run_start = first[None, :] & (run_of_tile[None, :] == tile[:, None])
    run_expert = jnp.sum(jnp.where(is_run_start, tile_expert[None, :], 0), axis=1).astype(I32)
    nxt = run_of_tile + 2
    next2 = jnp.where(nxt < n_runs[0], run_expert[jnp.minimum(nxt, n_tiles - 1)], -1).astype(I32)
    first = first.astype(I32)

    def rows(i, te, na, *_):
        return (jnp.minimum(i, na[0] - 1), 0)

    any_spec = pl.BlockSpec(memory_space=pl.ANY)
    return pl.pallas_call(
        functools.partial(_expert_kernel, layer),
        grid_spec=pltpu.PrefetchScalarGridSpec(
            num_scalar_prefetch=7,
            grid=(n_tiles,),
            in_specs=[pl.BlockSpec((MOE_TILE * SUBLANES, d), rows), any_spec, any_spec, any_spec],
            out_specs=pl.BlockSpec((MOE_TILE * SUBLANES, d), rows),
            scratch_shapes=[pltpu.VMEM((2, d_model, D_EXPERT), F32), pltpu.VMEM((2, d_model, D_EXPERT), F32),
                            pltpu.VMEM((2, D_EXPERT, d_model), F32),
                            pltpu.VMEM((d_model, D_EXPERT), BF16), pltpu.VMEM((d_model, D_EXPERT), BF16),
                            pltpu.VMEM((D_EXPERT, d_model), BF16),
                            pltpu.SemaphoreType.DMA((2,))],
        ),
        out_shape=jax.ShapeDtypeStruct((n_rows * SUBLANES, d), U32),
        compiler_params=_cparams(1),
        name="moe_experts",
    )(tile_expert, n_active, first, run_of_tile, next2, run_expert, n_runs, xs, w_gate, w_up, w_down)


def _final_kernel(pos_ref, x1_ref, x1b_ref, gates_ref, p_ref, wg_ref, bg_ref, wp_ref, g_ref, b_ref, ys_ref,
                  x2_ref, x2b_ref, ybuf, sem):
    i = pl.program_id(0)
    n_steps = pl.num_programs(0)
    tm = x1_ref.shape[0]
    t = n_steps * tm

    def gather(step, slot):
        for m in range(tm):
            for k in range(TOP_K):
                src = pl.multiple_of(pos_ref[k * t + step * tm + m] * SUBLANES, SUBLANES)
                pltpu.make_async_copy(ys_ref.at[pl.ds(src, SUBLANES)],
                                      ybuf.at[slot, k, pl.ds(m * SUBLANES, SUBLANES)],
                                      sem.at[slot]).start(priority=k)

    def wait_slot(slot):
        for k in range(TOP_K):
            pltpu.make_async_copy(ys_ref.at[pl.ds(0, tm * SUBLANES)], ybuf.at[slot, k], sem.at[slot]).wait()

    @pl.when(i == 0)
    def _():
        gather(0, 0)
        gather(1, 1)

    slot = lax.rem(i, GATHER_SLOTS)
    wait_slot(slot)
    gather(jnp.minimum(i + 2, n_steps - 1), lax.rem(i + 2, GATHER_SLOTS))

    rows = tm // FINAL_SPLIT
    parts = [slice(pi * rows, (pi + 1) * rows) for pi in range(FINAL_SPLIT)]
    ples = []
    for rs in parts:
        gate = _sigmoid(jnp.dot(x1b_ref[rs, :], wg_ref[0], preferred_element_type=F32) + bg_ref[0])
        ples.append(gate * jnp.dot(p_ref[0, rs, :].astype(BF16), wp_ref[0], preferred_element_type=F32))
    for rs, ple in zip(parts, ples):
        gates = gates_ref[rs, :]
        tiles = pl.ds(rs.start * SUBLANES, rows * SUBLANES)
        ffn = (gates[:, 0:1] * _unpack_bf16_pairs(_load_row_tiles(ybuf.at[slot, 0, tiles], rows))
               + gates[:, 1:2] * _unpack_bf16_pairs(_load_row_tiles(ybuf.at[slot, 1, tiles], rows)))
        x2 = _layer_norm(DN_ALPHA * x1_ref[rs, :] + ffn + ple, g_ref[0], b_ref[0])
        x2_ref[rs, :] = x2
        x2b_ref[rs, :] = x2.astype(BF16)

    @pl.when(i == n_steps - 1)
    def _():
        wait_slot(lax.rem(i + 1, GATHER_SLOTS))
        wait_slot(lax.rem(i + 2, GATHER_SLOTS))


def _final(layer, pos_flat, x1, x1b, gates, p, wg_bf, bg, wp_bf, ln_g, ln_b, ys):
    t, d = x1.shape
    tm = COMBINE_TILE

    def row(w):
        return pl.BlockSpec((tm, w), lambda i, pos: (i, 0))

    def per_layer(shape, **kw):
        return pl.BlockSpec((1,) + shape, lambda i, pos: (layer,) + (0,) * len(shape), **kw)

    once = dict(pipeline_mode=pl.Buffered(1))
    return pl.pallas_call(
        _final_kernel,
        grid_spec=pltpu.PrefetchScalarGridSpec(
            num_scalar_prefetch=1,
            grid=(t // tm,),
            in_specs=[row(d), row(d), row(LANES), pl.BlockSpec((1, tm, PLE_DIM), lambda i, pos: (layer, i, 0)),
                      per_layer((d, d), **once), per_layer((1, d)), per_layer((PLE_DIM, d), **once),
                      per_layer((1, d)), per_layer((1, d)), pl.BlockSpec(memory_space=pl.ANY)],
            out_specs=[row(d), row(d)],
            scratch_shapes=[pltpu.VMEM((GATHER_SLOTS, TOP_K, tm * SUBLANES, LANES), U32),
                            pltpu.SemaphoreType.DMA((GATHER_SLOTS,))],
        ),
        out_shape=[jax.ShapeDtypeStruct((t, d), F32), jax.ShapeDtypeStruct((t, d), BF16)],
        compiler_params=_cparams(1),
        name="combine_ple_ln",
    )(pos_flat, x1, x1b, gates, p, wg_bf, bg, wp_bf, ln_g, ln_b, ys)


W_PREP_COLS = 512


def _arrange_kernel(wt_ref, o_ref):
    j = pl.program_id(1)
    wt = wt_ref[0]
    row = lax.broadcasted_iota(I32, (wt.shape[0], 1), 0)
    n_valid = jnp.where(j == COL_GL // W_PREP_COLS, GLA_RANK, W_PREP_COLS)
    o_ref[0] = jnp.where(row < n_valid, wt, 0.0).T.astype(BF16)


def _arrange_w_in(w):
    n_layer, k, n_in = w.shape
    assert sum(IN_SIZES[:5]) == COL_GL + GLA_RANK and n_in + GL_PAD - GLA_RANK == U_COLS
    assert COL_GL % W_PREP_COLS == 0 and GL_PAD == W_PREP_COLS
    gl_blk = COL_GL // W_PREP_COLS

    def src_row(l, j):
        row = jnp.where(j <= gl_blk, j * W_PREP_COLS, j * W_PREP_COLS - (GL_PAD - GLA_RANK))
        return (l, pl.multiple_of(row, GLA_RANK), 0)

    return pl.pallas_call(
        _arrange_kernel,
        grid=(n_layer, U_COLS // W_PREP_COLS),
        in_specs=[pl.BlockSpec((pl.Element(1), pl.Element(W_PREP_COLS), pl.Element(k)), src_row)],
        out_specs=pl.BlockSpec((1, k, W_PREP_COLS), lambda l, j: (l, 0, j)),
        out_shape=jax.ShapeDtypeStruct((n_layer, k, U_COLS), BF16),
        compiler_params=_cparams(2),
        name="arrange_w_in",
    )(jnp.swapaxes(w, 1, 2))


def kernel(x, p, w_in, conf_dw_w, conf_dw_b, conf_ln_g, conf_ln_b, gla_w_g2, gla_b_g2, gla_norm_g, sc_conv_w, mix_scale, w_out, rel_bias, ln1_g, ln1_b, router_g_w, router_g_b, router_e_w, router_e_b, exp_w_gate, exp_w_up, exp_w_down, ple_w_gate, ple_b_gate, ple_w_proj, ln2_g, ln2_b):
    bsz, seq, d = x.shape
    t = bsz * seq
    n_rows = TOP_K * t + N_EXPERTS * MOE_TILE
    n_tiles = n_rows // MOE_TILE
    bias_tab = _bias_tables(rel_bias)
    w_in_bf = _arrange_w_in(w_in)
    w_out_bf = w_out.astype(BF16)
    ple_wg_bf = ple_w_gate.astype(BF16)
    ple_wp_bf = ple_w_proj.astype(BF16)
    p3 = p.reshape(DEPTH, t, PLE_DIM)
    n_route = N_GROUPS + N_EXPERTS
    rw = jnp.concatenate([router_g_w, router_e_w, jnp.zeros((DEPTH, d, LANES - n_route), F32)], axis=2)
    rb = jnp.concatenate([router_g_b, router_e_b, jnp.zeros((DEPTH, LANES - n_route), F32)], axis=1)[:, None, :]
    rw_hi = rw.astype(BF16)
    rw_cat = jnp.concatenate([rw_hi, (rw - rw_hi.astype(F32)).astype(BF16)], axis=2)

    def vec(a):
        return a.reshape(DEPTH, 1, -1)

    xf = x.reshape(t, d)
    xb = xf
    for i in range(DEPTH):
        u = _in_proj(xb, w_in_bf, i, 512, 2048, F32)
        u3 = u.reshape(bsz, seq, U_COLS)
        ya, yd = _conv_mixers(u3, conf_dw_w[i], conf_dw_b[i], conf_ln_g[i], conf_ln_b[i], sc_conv_w[i])
        yb = _gla(u3, gla_w_g2[i], gla_b_g2[i], gla_norm_g[i])
        yc = _dilated(u3, bias_tab)
        x1, x1b, x1p, ids, gates = _outproj(i, xf, ya.reshape(t, -1), yb.reshape(t, -1), yc.reshape(t, -1),
                                            yd.reshape(t, -1), vec(mix_scale), w_out_bf, vec(ln1_g), vec(ln1_b),
                                            rw_cat[i], rb[i])
        pos, tile_tab = _plan(ids)
        pos_flat = pos[:, :TOP_K].T.reshape(-1)
        tile_end = tile_tab[:n_tiles, 0]
        n_active = jnp.sum((tile_end < N_EXPERTS).astype(I32)).reshape(1)
        tile_expert = jnp.minimum(tile_end, N_EXPERTS - 1)
        xs = _dispatch(pos_flat, x1p, n_rows)
        ys = _experts(i, tile_expert, n_active, xs, exp_w_gate, exp_w_up, exp_w_down)
        xf, xb = _final(i, pos_flat, x1, x1b, gates, p3, ple_wg_bf, vec(ple_b_gate), ple_wp_bf, vec(ln2_g), vec(ln2_b), ys)
    return xf.reshape(bsz, seq, d)
```

```python
import functools
import math

import jax
import jax.numpy as jnp
from jax import lax
from jax.experimental import pallas as pl
from jax.experimental.pallas import tpu as pltpu

F32 = jnp.float32
BF16 = jnp.bfloat16
I32 = jnp.int32
HIGHEST = lax.Precision.HIGHEST

D_MODEL = 2048
DEPTH = 2
PLE_DIM = 256
W_CONF = 512
W_GLA = 512
W_DIL = 512
W_SC = 512
CONF_KERNEL = 31
GLA_HEADS = 4
GLA_DV = 128
GLA_DK = 64
GLA_RANK = 16
GLA_TAU = 16.0
DIL_HEADS = 8
DIL_HD = 64
DIL_BRANCHES = ((128, 1), (512, 4), (2048, 16))
DIL_BLOCK = 128
REL_BUCKETS = 32
REL_MAX_DIST = 2048
SC_KERNEL = 3
N_GROUPS = 4
EXPERTS_PER_GROUP = 8
N_EXPERTS = 32
TOP_K = 2
D_EXPERT = 256
DN_ALPHA = (2 * DEPTH) ** 0.25
LN_EPS = 1e-5
RMS_EPS = 1e-6
IN_SIZES = (2 * W_CONF, GLA_HEADS * GLA_DK, GLA_HEADS * GLA_DK, W_GLA, GLA_RANK, W_GLA,
            W_DIL, W_DIL, W_DIL, W_SC, W_SC, W_SC)

LANES = 128
SUBLANES = 8
VMEM_LIMIT = 56 * 1024 * 1024

GL_PAD = 512
COL_A = 0
COL_GQ = 1024
COL_GK = 1280
COL_GV = 1536
COL_GL = 2048
COL_GR = 2560
COL_CQ = 3072
COL_CK = 3584
COL_CV = 4096
COL_SB = 4608
COL_SC = 5120
COL_SH = 5632
U_COLS = 6144

MOE_TILE = 512


def _cparams(n_axes):
    return pltpu.CompilerParams(dimension_semantics=("arbitrary",) * n_axes, vmem_limit_bytes=VMEM_LIMIT)


def _layer_norm(x, g, b):
    mu = jnp.mean(x, axis=-1, keepdims=True)
    xc = x - mu
    var = jnp.mean(xc * xc, axis=-1, keepdims=True)
    return xc * lax.rsqrt(var + LN_EPS) * g + b


U32 = jnp.uint32


def _pack_bf16_pairs(x):
    n = x.shape[1] // 2
    bits = lax.bitcast_convert_type(x.astype(BF16).astype(F32), U32)
    return (bits[:, :n] >> 16) | (bits[:, n:] & jnp.uint32(0xFFFF0000))


def _unpack_bf16_pairs(w):
    lo = lax.bitcast_convert_type(w << 16, F32)
    hi = lax.bitcast_convert_type(w & jnp.uint32(0xFFFF0000), F32)
    return jnp.concatenate([lo, hi], axis=1)


def _store_row_tiles(ref, packed):
    rows, width = packed.shape
    assert width == SUBLANES * LANES
    for s in range(SUBLANES):
        ref[pl.ds(s, rows, stride=SUBLANES), :] = packed[:, s * LANES:(s + 1) * LANES]


def _load_row_tiles(ref, rows):
    return jnp.concatenate([ref[pl.ds(s, rows, stride=SUBLANES), :] for s in range(SUBLANES)], axis=1)


def _sigmoid(x):
    return 1.0 / (1.0 + jnp.exp(-x))


def _silu(x):
    return x * _sigmoid(x)


def _matmul_kernel(x_ref, w_ref, o_ref):
    o_ref[...] = jnp.dot(x_ref[...].astype(BF16), w_ref[0],
                         preferred_element_type=F32).astype(o_ref.dtype)


def _in_proj(x, w_all, layer, tm, tn, out_dtype):
    m, k = x.shape
    n = w_all.shape[2]
    return pl.pallas_call(
        _matmul_kernel,
        grid=(n // tn, m // tm),
        in_specs=[pl.BlockSpec((tm, k), lambda j, i: (i, 0)),
                  pl.BlockSpec((1, k, tn), lambda j, i: (layer, 0, j))],
        out_specs=pl.BlockSpec((tm, tn), lambda j, i: (i, j)),
        out_shape=jax.ShapeDtypeStruct((m, n), out_dtype),
        compiler_params=_cparams(2),
        name="in_proj",
    )(x, w_all)


CONV_TS = 512
CONV_RC = 64
CONF_HIST = 32
SC_HIST = 8


def _conv_kernel(a_ref, gate_ref, sb_ref, sc_ref, sh_ref, cw_ref, cb_ref, lg_ref, lb_ref, sw_ref,
                 ya_ref, yd_ref, ha, hd, hs):
    s = pl.program_id(1)
    ts = a_ref.shape[1]
    n_ext = CONF_HIST + ts

    @pl.when(s == 0)
    def _():
        ha[0:CONF_HIST, :] = jnp.zeros((CONF_HIST, W_CONF), F32)
        hd[0:SC_HIST, :] = jnp.zeros((SC_HIST, W_SC), F32)

    ha[CONF_HIST:CONF_HIST + ts, :] = a_ref[0] * _sigmoid(gate_ref[0])
    hd[SC_HIST:SC_HIST + ts, :] = sc_ref[0] * sh_ref[0]

    cb = cb_ref[...]
    lg = lg_ref[...]
    lb = lb_ref[...]
    off_a = CONF_HIST - (CONF_KERNEL - 1)
    h_all = ha[...]
    for sh in range(1, SUBLANES):
        h_all = pltpu.roll(h_all, n_ext - 1, 0)
        hs[sh - 1] = h_all
    for c in range(ts // CONV_RC):
        r0 = c * CONV_RC
        acc = jnp.zeros((CONV_RC, W_CONF), F32)
        for j in range(CONF_KERNEL):
            sh = (off_a + j) % SUBLANES
            lo = r0 + off_a + j - sh
            rows = ha[lo:lo + CONV_RC, :] if sh == 0 else hs[sh - 1, lo:lo + CONV_RC, :]
            acc = acc + cw_ref[j:j + 1, :] * rows
        hh = _layer_norm(acc + cb, lg, lb)
        ya_ref[0, r0:r0 + CONV_RC, :] = _silu(hh)

    off_d = SC_HIST - (SC_KERNEL - 1)
    accd = jnp.zeros((ts, W_SC), F32)
    for j in range(SC_KERNEL):
        accd = accd + sw_ref[j:j + 1, :] * hd[off_d + j:off_d + j + ts, :]
    yd_ref[0] = sb_ref[0] * accd

    ha[0:CONF_HIST, :] = ha[ts:ts + CONF_HIST, :]
    hd[0:SC_HIST, :] = hd[ts:ts + SC_HIST, :]


def _conv_mixers(u3, conf_w, conf_b, conf_lg, conf_lb, sc_w):
    bsz, seq, _ = u3.shape
    ts = CONV_TS
    wblk = W_CONF

    def col(c):
        return pl.BlockSpec((1, ts, wblk), lambda b, s, c=c: (b, s, c // wblk))

    def full(shape):
        return pl.BlockSpec(shape, lambda b, s: (0,) * len(shape))

    cw = jnp.zeros((32, W_CONF), F32).at[:CONF_KERNEL].set(conf_w)
    sw = jnp.zeros((8, W_SC), F32).at[:SC_KERNEL].set(sc_w)
    out_spec = pl.BlockSpec((1, ts, wblk), lambda b, s: (b, s, 0))
    return pl.pallas_call(
        _conv_kernel,
        grid=(bsz, seq // ts),
        in_specs=[col(COL_A), col(COL_A + W_CONF), col(COL_SB), col(COL_SC), col(COL_SH),
                  full((32, W_CONF)), full((1, W_CONF)), full((1, W_CONF)), full((1, W_CONF)), full((8, W_SC))],
        out_specs=[out_spec, out_spec],
        out_shape=[jax.ShapeDtypeStruct((bsz, seq, W_CONF), F32), jax.ShapeDtypeStruct((bsz, seq, W_SC), F32)],
        scratch_shapes=[pltpu.VMEM((CONF_HIST + ts, W_CONF), F32), pltpu.VMEM((SC_HIST + ts, W_SC), F32),
                        pltpu.VMEM((SUBLANES - 1, CONF_HIST + ts, W_CONF), F32)],
        compiler_params=_cparams(2),
        name="conv_mixers",
    )(u3, u3, u3, u3, u3, cw, conf_b.reshape(1, -1), conf_lg.reshape(1, -1), conf_lb.reshape(1, -1), sw)


GLA_BLK = 128
GLA_SUB = 64
GLA_UNROLL = 8


def _split3(x):
    hi = x.astype(BF16)
    r1 = x - hi.astype(F32)
    mid = r1.astype(BF16)
    lo = (r1 - mid.astype(F32)).astype(BF16)
    return hi, mid, lo


def _gla_kernel(q_ref, k_ref, v_ref, gl_ref, r_ref, wg2h_ref, wg2l_ref, bg2_ref, ng_ref, tril_ref, o_ref, la_ref):
    seq = q_ref.shape[1]
    nblk = seq // GLA_BLK
    nsub = GLA_BLK // GLA_SUB
    assert nsub == 2 and nblk % GLA_UNROLL == 0

    lane = lax.broadcasted_iota(I32, (1, LANES), 1)
    head_mask = [(lane < GLA_DK).astype(F32), (lane >= GLA_DK).astype(F32)]
    row = lax.broadcasted_iota(I32, (GLA_BLK, GLA_BLK), 0)
    col = lax.broadcasted_iota(I32, (GLA_BLK, GLA_BLK), 1)
    sub_shift = GLA_SUB.bit_length() - 1
    diag_mask = ((row >> sub_shift) == (col >> sub_shift)) & (col <= row)
    second_half = lax.broadcasted_iota(I32, (GLA_BLK, 1), 0) >= GLA_SUB
    scale = GLA_DK ** -0.5
    ng = ng_ref[...]
    tril = tril_ref[...]

    gl = gl_ref[0]
    gl_hi = gl.astype(BF16)
    gl_lo = (gl - gl_hi.astype(F32)).astype(BF16)
    wh = wg2h_ref[...]
    z = (jnp.dot(gl_hi, wh, preferred_element_type=F32) + jnp.dot(gl_lo, wh, preferred_element_type=F32)
         + jnp.dot(gl_hi, wg2l_ref[...], preferred_element_type=F32) + bg2_ref[...])
    la_ref[...] = (jnp.minimum(z, 0.0) - jnp.log(1.0 + jnp.exp(-jnp.abs(z)))) * (1.0 / GLA_TAU)

    nt = (((1,), (1,)), ((), ()))

    def body(it, carry):
        grp = range(GLA_UNROLL)
        r0 = [pl.multiple_of((it * GLA_UNROLL + j) * GLA_BLK, GLA_BLK) for j in grp]
        q = [q_ref[0, pl.ds(r, GLA_BLK), :] * scale for r in r0]
        k = [k_ref[0, pl.ds(r, GLA_BLK), :] for r in r0]
        v = [v_ref[0, pl.ds(r, GLA_BLK), :].astype(BF16) for r in r0]
        b = []
        for r in r0:
            acc = None
            for part in _split3(la_ref[pl.ds(r, GLA_BLK), :]):
                t = jnp.dot(tril, part, preferred_element_type=F32)
                acc = t if acc is None else acc + t
            b.append(acc)

        s_in, q_st, kk, b_last = [], [], [], []
        for j in grp:
            b3 = b[j].reshape(nsub, GLA_SUB, LANES)
            mid = jnp.broadcast_to(b3[:, GLA_SUB // 2 - 1:GLA_SUB // 2, :], b3.shape).reshape(GLA_BLK, LANES)
            qd = q[j] * jnp.exp(b[j] - mid)
            kd = (k[j] * jnp.exp(mid - b[j])).astype(BF16)
            e0 = b[j][GLA_SUB - 1:GLA_SUB, :]
            q_off = jnp.where(second_half, q[j] * jnp.exp(jnp.minimum(b[j] - e0, 0.0)), 0.0)
            k_off = jnp.where(second_half, 0.0, k[j] * jnp.exp(jnp.minimum(e0 - b[j], 0.0))).astype(BF16)
            s_in.append((qd, kd, q_off, k_off))
            q_st.append(q[j] * jnp.exp(b[j]))
            b_last.append(b[j][GLA_BLK - 1:GLA_BLK, :])
            kk.append((k[j] * jnp.exp(b_last[j] - b[j])).astype(BF16))

        scores = []
        for j in grp:
            qd, kd, q_off, k_off = s_in[j]
            per_head = []
            for h in range(2):
                hm = head_mask[h]
                s_d = lax.dot_general((qd * hm).astype(BF16), kd, nt, preferred_element_type=F32)
                s_o = lax.dot_general((q_off * hm).astype(BF16), k_off, nt, preferred_element_type=F32)
                per_head.append((jnp.where(diag_mask, s_d, 0.0) + s_o).astype(BF16))
            scores.append(per_head)

        states = []
        st = carry
        for j in grp:
            states.append(st.astype(BF16))
            upd = lax.dot_general(kk[j], v[j], (((0,), (0,)), ((), ())), preferred_element_type=F32)
            decay = jnp.transpose(jnp.broadcast_to(jnp.exp(b_last[j]), (GLA_BLK, LANES)))
            st = st * jnp.concatenate([decay, decay], axis=1) + upd

        for j in grp:
            outs = []
            for h in range(2):
                lhs = jnp.concatenate([scores[j][h], (q_st[j] * head_mask[h]).astype(BF16)], axis=1)
                rhs = jnp.concatenate([v[j][:, h * GLA_DV:(h + 1) * GLA_DV],
                                       states[j][:, h * GLA_DV:(h + 1) * GLA_DV]], axis=0)
                o_h = jnp.dot(lhs, rhs, preferred_element_type=F32)
                outs.append(o_h * lax.rsqrt(jnp.mean(o_h * o_h, axis=-1, keepdims=True) + RMS_EPS))
            o = jnp.concatenate(outs, axis=1) * ng
            o_ref[0, pl.ds(r0[j], GLA_BLK), :] = o * _silu(r_ref[0, pl.ds(r0[j], GLA_BLK), :])
        return st

    lax.fori_loop(0, nblk // GLA_UNROLL, body, jnp.zeros((2 * GLA_DK, 2 * GLA_DV), F32))


def _gla(u3, w_g2, b_g2, norm_g):
    bsz, seq, _ = u3.shape
    npair = GLA_HEADS // 2
    kw = 2 * GLA_DK
    vw = 2 * GLA_DV

    def col(c, w):
        return pl.BlockSpec((1, seq, w), lambda b, p, c=c, w=w: (b, 0, c // w + p))

    wg2 = jnp.zeros((LANES, GLA_HEADS * GLA_DK), F32).at[:GLA_RANK].set(w_g2)
    wg2_hi = wg2.astype(BF16)
    wg2_lo = (wg2 - wg2_hi.astype(F32)).astype(BF16)
    ng = jnp.concatenate([norm_g, norm_g]).reshape(1, vw)
    tril = jnp.tril(jnp.ones((GLA_BLK, GLA_BLK), BF16))
    return pl.pallas_call(
        _gla_kernel,
        grid=(bsz, npair),
        in_specs=[col(COL_GQ, kw), col(COL_GK, kw), col(COL_GV, vw),
                  pl.BlockSpec((1, seq, LANES), lambda b, p: (b, 0, COL_GL // LANES)),
                  col(COL_GR, vw),
                  pl.BlockSpec((LANES, kw), lambda b, p: (0, p)),
                  pl.BlockSpec((LANES, kw), lambda b, p: (0, p)),
                  pl.BlockSpec((1, kw), lambda b, p: (0, p)),
                  pl.BlockSpec((1, vw), lambda b, p: (0, 0)),
                  pl.BlockSpec((GLA_BLK, GLA_BLK), lambda b, p: (0, 0))],
        out_specs=pl.BlockSpec((1, seq, vw), lambda b, p: (b, 0, p)),
        out_shape=jax.ShapeDtypeStruct((bsz, seq, W_GLA), F32),
        scratch_shapes=[pltpu.VMEM((seq, LANES), F32)],
        compiler_params=_cparams(2),
        name="gla",
    )(u3, u3, u3, u3, u3, wg2_hi, wg2_lo, b_g2.reshape(1, -1), ng, tril)


def _t5_bucket(dist):
    max_exact = REL_BUCKETS // 2
    large = max_exact + (jnp.log(jnp.maximum(dist, 1).astype(F32) / max_exact)
                         / math.log(REL_MAX_DIST / max_exact) * (REL_BUCKETS - max_exact)).astype(I32)
    large = jnp.minimum(large, REL_BUCKETS - 1)
    return jnp.where(dist < max_exact, dist, large)


LOG2E = math.log2(math.e)
DIL_STAGE = 4
DIL_GROUP = 8


def _bias_kernel(rel_ref, bucket_ref, o_ref):
    h = pl.program_id(1)
    blk = DIL_BLOCK
    bucket = bucket_ref[0]
    acc = jnp.zeros(bucket.shape, F32)
    for b in range(REL_BUCKETS):
        acc = jnp.where(bucket == b, rel_ref[b, h], acc)
    acc = acc * LOG2E
    qi = lax.broadcasted_iota(I32, (blk, 2 * blk), 0)
    kj = lax.broadcasted_iota(I32, (blk, 2 * blk), 1)
    steps = qi + blk - kj
    in_window = (steps >= 0) & (steps <= blk)
    neg = jnp.float32(-jnp.inf)
    o_ref[0, 0, 0] = jnp.where(in_window, acc, neg)
    o_ref[0, 0, 1] = jnp.where(in_window & (kj >= blk), acc, neg)


def _bias_tables(rel_bias):
    assert all(window // dil == DIL_BLOCK for window, dil in DIL_BRANCHES)
    qi = jnp.arange(DIL_BLOCK)[:, None]
    kj = jnp.arange(2 * DIL_BLOCK)[None, :]
    steps = jnp.maximum(qi + DIL_BLOCK - kj, 0)
    buckets = jnp.stack([_t5_bucket(steps * dil) for _, dil in DIL_BRANCHES]).astype(I32)
    nbr = len(DIL_BRANCHES)
    return pl.pallas_call(
        _bias_kernel,
        grid=(nbr, DIL_HEADS),
        in_specs=[pl.BlockSpec(memory_space=pltpu.SMEM),
                  pl.BlockSpec((1, DIL_BLOCK, 2 * DIL_BLOCK), lambda i, h: (i, 0, 0))],
        out_specs=pl.BlockSpec((1, 1, 2, DIL_BLOCK, 2 * DIL_BLOCK), lambda i, h: (i, h, 0, 0, 0)),
        out_shape=jax.ShapeDtypeStruct((nbr, DIL_HEADS, 2, DIL_BLOCK, 2 * DIL_BLOCK), F32),
        compiler_params=_cparams(2),
        name="rel_bias_tables",
    )(rel_bias, buckets)


def _dil_kernel(q_ref, k_ref, v_ref, bias_ref, o_ref, qs, ks, vs, ob, lb, stage):
    seq = q_ref.shape[1]
    blk = DIL_BLOCK
    scale = DIL_HD ** -0.5 * LOG2E
    lane = lax.broadcasted_iota(I32, (1, LANES), 1)
    first = lane < DIL_HD
    head_mask = [first.astype(F32), 1.0 - first.astype(F32)]

    ks[0:blk, :] = jnp.zeros((blk, LANES), BF16)
    vs[0:blk, :] = jnp.zeros((blk, LANES), BF16)

    for bi, (window, dil) in enumerate(DIL_BRANCHES):
        sub_len = seq // dil
        nb = sub_len // blk
        nb_shift = nb.bit_length() - 1
        for r in range(dil):
            dst = slice(r * sub_len, (r + 1) * sub_len)
            if dil == 1:
                q, k, v = q_ref[0], k_ref[0], v_ref[0]
            elif dil == DIL_STAGE:
                rows = pl.ds(r, sub_len, stride=dil)
                q, k, v = q_ref[0, rows, :], k_ref[0, rows, :], v_ref[0, rows, :]
                stage[0, dst, :] = q
                stage[1, dst, :] = k
                stage[2, dst, :] = v
            else:
                assert dil == DIL_STAGE * DIL_STAGE
                rows = pl.ds((r % DIL_STAGE) * (seq // DIL_STAGE) + r // DIL_STAGE, sub_len, stride=DIL_STAGE)
                q, k, v = stage[0, rows, :], stage[1, rows, :], stage[2, rows, :]
            q = q * scale
            for h in range(2):
                qs[h, dst, :] = (q * head_mask[h]).astype(BF16)
            ks[blk + r * sub_len:blk + (r + 1) * sub_len, :] = k.astype(BF16)
            vs[blk + r * sub_len:blk + (r + 1) * sub_len, :] = v.astype(BF16)

        def body(it, carry, bi=bi, dil=dil, nb=nb, nb_shift=nb_shift):
            grp = range(DIL_GROUP)
            g = [it * DIL_GROUP + gg for gg in grp]
            n = [gi & (nb - 1) for gi in g]
            g0 = [pl.multiple_of(gi * blk, blk) for gi in g]
            kb = [ks[pl.ds(s0, 2 * blk), :] for s0 in g0]
            sc = [[lax.dot_general(qs[h, pl.ds(g0[gg], blk), :], kb[gg], (((1,), (1,)), ((), ())),
                                   preferred_element_type=F32) + bias_ref[bi, h, jnp.where(n[gg] == 0, 1, 0)]
                   for h in range(2)] for gg in grp]
            stats = []
            for gg in grp:
                per_head = []
                for h in range(2):
                    m = jnp.max(sc[gg][h], axis=-1, keepdims=True)
                    p = jnp.exp2(sc[gg][h] - m)
                    l = jnp.sum(p, axis=-1, keepdims=True)
                    per_head.append((p.astype(BF16), m, l))
                stats.append(per_head)
            for gg in grp:
                vb = vs[pl.ds(g0[gg], 2 * blk), :]
                o_acc = None
                l_acc = None
                for h in range(2):
                    p, m, l = stats[gg][h]
                    o_h = jnp.dot(p, vb, preferred_element_type=F32) * (1.0 / l)
                    lse = jnp.broadcast_to(m + jnp.log2(l), (blk, LANES))
                    if h == 0:
                        o_acc, l_acc = o_h, lse
                    else:
                        o_acc = jnp.where(first, o_acc, o_h)
                        l_acc = jnp.where(first, l_acc, lse)
                if dil > 1:
                    dst = pl.ds(n[gg] * (blk * dil) + (g[gg] >> nb_shift), blk, stride=dil)
                else:
                    dst = pl.ds(g0[gg], blk)
                ob[bi, dst, :] = o_acc
                lb[bi, dst, :] = l_acc
            return carry

        lax.fori_loop(0, seq // blk // DIL_GROUP, body, 0)

    nbr = len(DIL_BRANCHES)
    mc = 256
    for c in range(seq // mc):
        rs = slice(c * mc, (c + 1) * mc)
        ls = [lb[i, rs, :] for i in range(nbr)]
        m = functools.reduce(jnp.maximum, ls)
        ws = [jnp.exp2(l - m) for l in ls]
        num = functools.reduce(lambda a, b: a + b, [w * ob[i, rs, :] for i, w in enumerate(ws)])
        den = functools.reduce(lambda a, b: a + b, ws)
        o_ref[0, rs, :] = num / den


def _dilated(u3, bias_tab):
    bsz, seq, _ = u3.shape
    npair = DIL_HEADS // 2
    nbr = len(DIL_BRANCHES)

    def col(c):
        return pl.BlockSpec((1, seq, LANES), lambda b, p, c=c: (b, 0, c // LANES + p))

    return pl.pallas_call(
        _dil_kernel,
        grid=(bsz, npair),
        in_specs=[col(COL_CQ), col(COL_CK), col(COL_CV),
                  pl.BlockSpec((nbr, 2, 2, DIL_BLOCK, 2 * DIL_BLOCK), lambda b, p: (0, p, 0, 0, 0))],
        out_specs=pl.BlockSpec((1, seq, LANES), lambda b, p: (b, 0, p)),
        out_shape=jax.ShapeDtypeStruct((bsz, seq, W_DIL), F32),
        scratch_shapes=[pltpu.VMEM((2, seq, LANES), BF16),
                        pltpu.VMEM((DIL_BLOCK + seq, LANES), BF16),
                        pltpu.VMEM((DIL_BLOCK + seq, LANES), BF16),
                        pltpu.VMEM((nbr, seq, LANES), F32),
                        pltpu.VMEM((nbr, seq, LANES), F32),
                        pltpu.VMEM((3, seq, LANES), F32)],
        compiler_params=_cparams(2),
        name="dilated_attn",
    )(u3, u3, u3, bias_tab)


TOK_TILE = 256
OUT_TILE = 512
OUT_SPLIT = 2
FINAL_SPLIT = 2
GATHER_SLOTS = 3
PLAN_TILE = 1024


def _outproj_kernel(x_ref, ya_ref, yb_ref, yc_ref, yd_ref, ms_ref, wo_ref, g_ref, b_ref, rw_ref, rb_ref,
                    x1_ref, x1b_ref, x1p_ref, ids_ref, gates_ref):
    tm = x_ref.shape[0]
    rows = tm // OUT_SPLIT
    parts = [slice(pi * rows, (pi + 1) * rows) for pi in range(OUT_SPLIT)]
    accs = []
    for rs in parts:
        acc = None
        for gi, y_ref in enumerate((ya_ref, yb_ref, yc_ref, yd_ref)):
            lo = gi * W_CONF
            y = (y_ref[rs, :] * ms_ref[0, :, lo:lo + W_CONF]).astype(BF16)
            part = jnp.dot(y, wo_ref[0, lo:lo + W_CONF, :], preferred_element_type=F32)
            acc = part if acc is None else acc + part
        accs.append(acc)

    logits = []
    for rs, acc in zip(parts, accs):
        x1 = _layer_norm(DN_ALPHA * x_ref[rs, :] + acc, g_ref[0], b_ref[0])
        x1_ref[rs, :] = x1
        x1_hi = x1.astype(BF16)
        x1b_ref[rs, :] = x1_hi
        _store_row_tiles(x1p_ref.at[pl.ds(rs.start * SUBLANES, rows * SUBLANES)], _pack_bf16_pairs(x1))
        x1_lo = (x1 - x1_hi.astype(F32)).astype(BF16)
        hh_hl = jnp.dot(x1_hi, rw_ref[...], preferred_element_type=F32)
        logits.append(hh_hl[:, :LANES] + hh_hl[:, LANES:]
                      + jnp.dot(x1_lo, rw_ref[:, :LANES], preferred_element_type=F32) + rb_ref[...])

    lane = lax.broadcasted_iota(I32, (rows, LANES), 1)
    neg = jnp.float32(-jnp.inf)
    for rs, lg in zip(parts, logits):
        gl = jnp.where(lane < N_GROUPS, lg, neg)
        gmax = jnp.max(gl, axis=-1, keepdims=True)
        g_top = 1.0 / jnp.sum(jnp.exp(gl - gmax), axis=-1, keepdims=True)
        g_idx = jnp.min(jnp.where(gl == gmax, lane, LANES), axis=-1, keepdims=True)
        e_lo = N_GROUPS + g_idx * EXPERTS_PER_GROUP
        el = jnp.where((lane >= e_lo) & (lane < e_lo + EXPERTS_PER_GROUP), lg, neg)
        m1 = jnp.max(el, axis=-1, keepdims=True)
        i1 = jnp.min(jnp.where(el == m1, lane, LANES), axis=-1, keepdims=True)
        el2 = jnp.where(lane == i1, neg, el)
        m2 = jnp.max(el2, axis=-1, keepdims=True)
        i2 = jnp.min(jnp.where(el2 == m2, lane, LANES), axis=-1, keepdims=True)
        e21 = jnp.exp(m2 - m1)
        w1 = g_top / (1.0 + e21)
        w2 = g_top * e21 / (1.0 + e21)
        ids_ref[rs, :] = jnp.where(lane == 0, i1 - N_GROUPS, jnp.where(lane == 1, i2 - N_GROUPS, N_EXPERTS))
        gates_ref[rs, :] = jnp.where(lane == 0, w1, jnp.where(lane == 1, w2, 0.0))


def _outproj(layer, x, ya, yb, yc, yd, mix_scale, w_out_bf, ln_g, ln_b, rw_cat, rb):
    t, d = x.shape
    tm = OUT_TILE

    def row(w):
        return pl.BlockSpec((tm, w), lambda i: (i, 0))

    def full(shape):
        return pl.BlockSpec(shape, lambda i: (0,) * len(shape))

    def per_layer(shape):
        return pl.BlockSpec((1,) + shape, lambda i: (layer,) + (0,) * len(shape))

    return pl.pallas_call(
        _outproj_kernel,
        grid=(t // tm,),
        in_specs=[row(d), row(W_CONF), row(W_GLA), row(W_DIL), row(W_SC), per_layer((1, d)), per_layer((d, d)),
                  per_layer((1, d)), per_layer((1, d)), full((d, 2 * LANES)), full((1, LANES))],
        out_specs=[row(d), row(d), pl.BlockSpec((tm * SUBLANES, LANES), lambda i: (i, 0)), row(LANES), row(LANES)],
        out_shape=[jax.ShapeDtypeStruct((t, d), F32), jax.ShapeDtypeStruct((t, d), BF16),
                   jax.ShapeDtypeStruct((t * SUBLANES, LANES), U32),
                   jax.ShapeDtypeStruct((t, LANES), I32), jax.ShapeDtypeStruct((t, LANES), F32)],
        compiler_params=_cparams(1),
        name="out_proj_ln_router",
    )(x, ya, yb, yc, yd, mix_scale, w_out_bf, ln_g, ln_b, rw_cat, rb)


def _plan_kernel(ids_ref, tril_ref, upper_ref, pos_ref, tile_ref, cnt, carry, start):
    phase = pl.program_id(0)
    i = pl.program_id(1)
    tm = ids_ref.shape[0]
    lane = lax.broadcasted_iota(I32, (tm, LANES), 1)
    ids = ids_ref[...]
    e1 = ids[:, 0:1]
    e2 = ids[:, 1:2]
    oh1 = (lane == e1).astype(F32)
    oh2 = (lane == e2).astype(F32)
    oh = oh1 + oh2

    @pl.when((phase == 0) & (i == 0))
    def _():
        cnt[...] = jnp.zeros(cnt.shape, F32)

    @pl.when(phase == 0)
    def _():
        cnt[...] = cnt[...] + jnp.sum(oh, axis=0, keepdims=True)

    @pl.when((phase == 1) & (i == 0))
    def _():
        padded = jnp.floor((cnt[...] + (MOE_TILE - 1)) * (1.0 / MOE_TILE)) * MOE_TILE
        p8 = jnp.broadcast_to(padded, (SUBLANES, LANES))
        st = jnp.dot(p8, upper_ref[...], precision=HIGHEST, preferred_element_type=F32)
        start[...] = st[0:1, :]
        carry[...] = jnp.zeros(carry.shape, F32)
        end = st[0:1, :] + padded
        trow = lax.broadcasted_iota(I32, (LANES, LANES), 0).astype(F32) * MOE_TILE
        tlane = lax.broadcasted_iota(I32, (LANES, LANES), 1)
        done = jnp.where((end <= trow) & (tlane < N_EXPERTS), 1.0, 0.0)
        tile_ref[...] = jnp.broadcast_to(jnp.sum(done, axis=-1, keepdims=True), (LANES, LANES)).astype(I32)

    @pl.when(phase == 1)
    def _():
        incl = jnp.dot(tril_ref[...], oh.astype(BF16), preferred_element_type=F32) + carry[...]
        base = start[...] + incl - oh
        p1 = jnp.sum(oh1 * base, axis=-1, keepdims=True)
        p2 = jnp.sum(oh2 * base, axis=-1, keepdims=True)
        pos_ref[...] = jnp.where(lane == 0, p1, jnp.where(lane == 1, p2, 0.0)).astype(I32)
        carry[...] = carry[...] + jnp.sum(oh, axis=0, keepdims=True)


def _plan(ids):
    t = ids.shape[0]
    tm = PLAN_TILE
    tril = jnp.tril(jnp.ones((tm, tm), BF16))
    upper = jnp.triu(jnp.ones((LANES, LANES), F32), k=1)
    return pl.pallas_call(
        _plan_kernel,
        grid=(2, t // tm),
        in_specs=[pl.BlockSpec((tm, LANES), lambda ph, i: (i, 0)),
                  pl.BlockSpec((tm, tm), lambda ph, i: (0, 0)),
                  pl.BlockSpec((LANES, LANES), lambda ph, i: (0, 0))],
        out_specs=[pl.BlockSpec((tm, LANES), lambda ph, i: (i * ph, 0)),
                   pl.BlockSpec((LANES, LANES), lambda ph, i: (0, 0))],
        out_shape=[jax.ShapeDtypeStruct((t, LANES), I32), jax.ShapeDtypeStruct((LANES, LANES), I32)],
        scratch_shapes=[pltpu.VMEM((1, LANES), F32), pltpu.VMEM((1, LANES), F32), pltpu.VMEM((1, LANES), F32)],
        compiler_params=_cparams(2),
        name="route_plan",
    )(ids, tril, upper)


DISPATCH_TILE = 512
DISPATCH_SLOTS = 3


def _dispatch_kernel(pos_ref, x_ref, xs_ref, stage, sem):
    i = pl.program_id(0)
    n_steps = pl.num_programs(0)
    tm = x_ref.shape[0] // SUBLANES
    t = n_steps * tm
    slot = lax.rem(i, DISPATCH_SLOTS)

    def wait_slot(s):
        for _ in range(TOP_K):
            pltpu.make_async_copy(stage.at[s], xs_ref.at[pl.ds(0, tm * SUBLANES)], sem.at[s]).wait()

    @pl.when(i >= DISPATCH_SLOTS)
    def _():
        wait_slot(slot)

    stage[slot] = x_ref[...]
    for m in range(tm):
        for k in range(TOP_K):
            dst = pl.multiple_of(pos_ref[k * t + i * tm + m] * SUBLANES, SUBLANES)
            pltpu.make_async_copy(stage.at[slot, pl.ds(m * SUBLANES, SUBLANES)], xs_ref.at[pl.ds(dst, SUBLANES)],
                                  sem.at[slot]).start(priority=k)

    @pl.when(i == n_steps - 1)
    def _():
        for s in range(DISPATCH_SLOTS):
            wait_slot(s)


def _dispatch(pos_flat, x1p, n_rows):
    t = x1p.shape[0] // SUBLANES
    tm = DISPATCH_TILE
    assert t // tm >= DISPATCH_SLOTS
    return pl.pallas_call(
        _dispatch_kernel,
        grid_spec=pltpu.PrefetchScalarGridSpec(
            num_scalar_prefetch=1,
            grid=(t // tm,),
            in_specs=[pl.BlockSpec((tm * SUBLANES, LANES), lambda i, pos: (i, 0))],
            out_specs=pl.BlockSpec(memory_space=pl.ANY),
            scratch_shapes=[pltpu.VMEM((DISPATCH_SLOTS, tm * SUBLANES, LANES), x1p.dtype),
                            pltpu.SemaphoreType.DMA((DISPATCH_SLOTS,))],
        ),
        out_shape=jax.ShapeDtypeStruct((n_rows * SUBLANES, LANES), x1p.dtype),
        compiler_params=_cparams(1),
        name="moe_dispatch",
    )(pos_flat, x1p)


def _expert_kernel(layer, te_ref, na_ref, first_ref, run_ref, fetch3_ref, pro_ref,
                   xs_ref, wg_hbm, wu_hbm, wd_hbm, ys_ref, wg_buf, wu_buf, wd_buf, wg_bf, wu_bf, wd_bf, sem):
    i = pl.program_id(0)

    def copies(e, s):
        return (pltpu.make_async_copy(wg_hbm.at[layer, e], wg_buf.at[s], sem.at[s]),
                pltpu.make_async_copy(wu_hbm.at[layer, e], wu_buf.at[s], sem.at[s]),
                pltpu.make_async_copy(wd_hbm.at[layer, e], wd_buf.at[s], sem.at[s]))

    def fetch(e, s):
        for c in copies(e, s):
            c.start()

    def wait_and_convert(s):
        for c in copies(0, s):
            c.wait()
        wg_bf[s] = wg_buf[s].astype(BF16)
        wu_bf[s] = wu_buf[s].astype(BF16)
        wd_bf[s] = wd_buf[s].astype(BF16)

    def compute(s):
        xb = _unpack_bf16_pairs(_load_row_tiles(xs_ref, MOE_TILE)).astype(BF16)
        g = jnp.dot(xb, wg_bf[s], preferred_element_type=F32)
        u = jnp.dot(xb, wu_bf[s], preferred_element_type=F32)
        h = (_silu(g) * u).astype(BF16)
        y = jnp.dot(h, wd_bf[s], preferred_element_type=F32)
        _store_row_tiles(ys_ref, _pack_bf16_pairs(y))

    @pl.when(i == 0)
    def _():
        fetch(pro_ref[0], 0)
        fetch(pro_ref[1], 1)
        wait_and_convert(0)
        fetch(pro_ref[2], 0)

    active = i < na_ref[0]
    s = run_ref[i] % 2

    @pl.when(active & (first_ref[i] == 1))
    def _():
        compute(s)
        wait_and_convert(1 - s)
        fetch(fetch3_ref[i], 1 - s)

    @pl.when(active & (first_ref[i] == 0))
    def _():
        compute(s)

    @pl.when(i == pl.num_programs(0) - 1)
    def _():
        for slot in range(2):
            for c in copies(0, slot):
                c.wait()


def _experts(layer, tile_expert, n_active, xs, w_gate, w_up, w_down):
    n_rows = xs.shape[0] // SUBLANES
    d = LANES
    n_tiles = n_rows // MOE_TILE
    d_model = w_gate.shape[2]

    tile = jnp.arange(n_tiles, dtype=I32)
    active = tile < n_active[0]
    first = active & ((tile == 0) | (tile_expert != jnp.roll(tile_expert, 1)))
    run_of_tile = jnp.cumsum(first.astype(I32)) - 1
    n_runs = jnp.sum(first.astype(I32)).reshape(1)
    is_run_start = first[None, :] & (run_of_tile[None, :] == tile[:, None])
    run_expert = jnp.sum(jnp.where(is_run_start, tile_expert[None, :], 0), axis=1).astype(I32)
    last_run = n_runs[0] - 1
    fetch3 = run_expert[jnp.minimum(run_of_tile + 3, last_run)]
    prologue = run_expert[jnp.minimum(jnp.arange(3, dtype=I32), last_run)]
    first = first.astype(I32)

    def rows(i, te, na, *_):
        return (jnp.minimum(i, na[0] - 1), 0)

    any_spec = pl.BlockSpec(memory_space=pl.ANY)
    return pl.pallas_call(
        functools.partial(_expert_kernel, layer),
        grid_spec=pltpu.PrefetchScalarGridSpec(
            num_scalar_prefetch=6,
            grid=(n_tiles,),
            in_specs=[pl.BlockSpec((MOE_TILE * SUBLANES, d), rows), any_spec, any_spec, any_spec],
            out_specs=pl.BlockSpec((MOE_TILE * SUBLANES, d), rows),
            scratch_shapes=[pltpu.VMEM((2, d_model, D_EXPERT), F32), pltpu.VMEM((2, d_model, D_EXPERT), F32),
                            pltpu.VMEM((2, D_EXPERT, d_model), F32),
                            pltpu.VMEM((2, d_model, D_EXPERT), BF16), pltpu.VMEM((2, d_model, D_EXPERT), BF16),
                            pltpu.VMEM((2, D_EXPERT, d_model), BF16),
                            pltpu.SemaphoreType.DMA((2,))],
        ),
        out_shape=jax.ShapeDtypeStruct((n_rows * SUBLANES, d), U32),
        compiler_params=_cparams(1),
        name="moe_experts",
    )(tile_expert, n_active, first, run_of_tile, fetch3, prologue, xs, w_gate, w_up, w_down)


def _final_kernel(pos_ref, x1_ref, x1b_ref, gates_ref, p_ref, wg_ref, bg_ref, wp_ref, g_ref, b_ref, ys_ref,
                  x2_ref, x2b_ref, ybuf, sem):
    i = pl.program_id(0)
    n_steps = pl.num_programs(0)
    tm = x1_ref.shape[0]
    t = n_steps * tm

    def gather(step, slot):
        for m in range(tm):
            for k in range(TOP_K):
                src = pl.multiple_of(pos_ref[k * t + step * tm + m] * SUBLANES, SUBLANES)
                pltpu.make_async_copy(ys_ref.at[pl.ds(src, SUBLANES)],
                                      ybuf.at[slot, k, pl.ds(m * SUBLANES, SUBLANES)],
                                      sem.at[slot]).start(priority=k)

    def wait_slot(slot):
        for k in range(TOP_K):
            pltpu.make_async_copy(ys_ref.at[pl.ds(0, tm * SUBLANES)], ybuf.at[slot, k], sem.at[slot]).wait()

    @pl.when(i == 0)
    def _():
        gather(0, 0)
        gather(1, 1)

    slot = lax.rem(i, GATHER_SLOTS)
    wait_slot(slot)
    gather(jnp.minimum(i + 2, n_steps - 1), lax.rem(i + 2, GATHER_SLOTS))

    rows = tm // FINAL_SPLIT
    parts = [slice(pi * rows, (pi + 1) * rows) for pi in range(FINAL_SPLIT)]
    ples = []
    for rs in parts:
        gate = _sigmoid(jnp.dot(x1b_ref[rs, :], wg_ref[0], preferred_element_type=F32) + bg_ref[0])
        ples.append(gate * jnp.dot(p_ref[0, rs, :].astype(BF16), wp_ref[0], preferred_element_type=F32))
    for rs, ple in zip(parts, ples):
        gates = gates_ref[rs, :]
        tiles = pl.ds(rs.start * SUBLANES, rows * SUBLANES)
        ffn = (gates[:, 0:1] * _unpack_bf16_pairs(_load_row_tiles(ybuf.at[slot, 0, tiles], rows))
               + gates[:, 1:2] * _unpack_bf16_pairs(_load_row_tiles(ybuf.at[slot, 1, tiles], rows)))
        x2 = _layer_norm(DN_ALPHA * x1_ref[rs, :] + ffn + ple, g_ref[0], b_ref[0])
        x2_ref[rs, :] = x2
        x2b_ref[rs, :] = x2.astype(BF16)

    @pl.when(i == n_steps - 1)
    def _():
        wait_slot(lax.rem(i + 1, GATHER_SLOTS))
        wait_slot(lax.rem(i + 2, GATHER_SLOTS))


def _final(layer, pos_flat, x1, x1b, gates, p, wg_bf, bg, wp_bf, ln_g, ln_b, ys):
    t, d = x1.shape
    tm = TOK_TILE

    def row(w):
        return pl.BlockSpec((tm, w), lambda i, pos: (i, 0))

    def per_layer(shape):
        return pl.BlockSpec((1,) + shape, lambda i, pos: (layer,) + (0,) * len(shape))

    return pl.pallas_call(
        _final_kernel,
        grid_spec=pltpu.PrefetchScalarGridSpec(
            num_scalar_prefetch=1,
            grid=(t // tm,),
            in_specs=[row(d), row(d), row(LANES), pl.BlockSpec((1, tm, PLE_DIM), lambda i, pos: (layer, i, 0)),
                      per_layer((d, d)), per_layer((1, d)), per_layer((PLE_DIM, d)),
                      per_layer((1, d)), per_layer((1, d)), pl.BlockSpec(memory_space=pl.ANY)],
            out_specs=[row(d), row(d)],
            scratch_shapes=[pltpu.VMEM((GATHER_SLOTS, TOP_K, tm * SUBLANES, LANES), U32),
                            pltpu.SemaphoreType.DMA((GATHER_SLOTS,))],
        ),
        out_shape=[jax.ShapeDtypeStruct((t, d), F32), jax.ShapeDtypeStruct((t, d), BF16)],
        compiler_params=_cparams(1),
        name="combine_ple_ln",
    )(pos_flat, x1, x1b, gates, p, wg_bf, bg, wp_bf, ln_g, ln_b, ys)


W_PREP_COLS = 512


def _arrange_kernel(wt_ref, o_ref):
    j = pl.program_id(1)
    wt = wt_ref[0]
    row = lax.broadcasted_iota(I32, (wt.shape[0], 1), 0)
    n_valid = jnp.where(j == COL_GL // W_PREP_COLS, GLA_RANK, W_PREP_COLS)
    o_ref[0] = jnp.where(row < n_valid, wt, 0.0).T.astype(BF16)


def _arrange_w_in(w):
    n_layer, k, n_in = w.shape
    assert sum(IN_SIZES[:5]) == COL_GL + GLA_RANK and n_in + GL_PAD - GLA_RANK == U_COLS
    assert COL_GL % W_PREP_COLS == 0 and GL_PAD == W_PREP_COLS
    gl_blk = COL_GL // W_PREP_COLS

    def src_row(l, j):
        row = jnp.where(j <= gl_blk, j * W_PREP_COLS, j * W_PREP_COLS - (GL_PAD - GLA_RANK))
        return (l, pl.multiple_of(row, GLA_RANK), 0)

    return pl.pallas_call(
        _arrange_kernel,
        grid=(n_layer, U_COLS // W_PREP_COLS),
        in_specs=[pl.BlockSpec((pl.Element(1), pl.Element(W_PREP_COLS), pl.Element(k)), src_row)],
        out_specs=pl.BlockSpec((1, k, W_PREP_COLS), lambda l, j: (l, 0, j)),
        out_shape=jax.ShapeDtypeStruct((n_layer, k, U_COLS), BF16),
        compiler_params=_cparams(2),
        name="arrange_w_in",
    )(jnp.swapaxes(w, 1, 2))


def kernel(x, p, w_in, conf_dw_w, conf_dw_b, conf_ln_g, conf_ln_b, gla_w_g2, gla_b_g2, gla_norm_g, sc_conv_w, mix_scale, w_out, rel_bias, ln1_g, ln1_b, router_g_w, router_g_b, router_e_w, router_e_b, exp_w_gate, exp_w_up, exp_w_down, ple_w_gate, ple_b_gate, ple_w_proj, ln2_g, ln2_b):
    bsz, seq, d = x.shape
    t = bsz * seq
    n_rows = TOP_K * t + N_EXPERTS * MOE_TILE
    n_tiles = n_rows // MOE_TILE
    bias_tab = _bias_tables(rel_bias)
    w_in_bf = _arrange_w_in(w_in)
    w_out_bf = w_out.astype(BF16)
    ple_wg_bf = ple_w_gate.astype(BF16)
    ple_wp_bf = ple_w_proj.astype(BF16)
    p3 = p.reshape(DEPTH, t, PLE_DIM)
    n_route = N_GROUPS + N_EXPERTS
    rw = jnp.concatenate([router_g_w, router_e_w, jnp.zeros((DEPTH, d, LANES - n_route), F32)], axis=2)
    rb = jnp.concatenate([router_g_b, router_e_b, jnp.zeros((DEPTH, LANES - n_route), F32)], axis=1)[:, None, :]
    rw_hi = rw.astype(BF16)
    rw_cat = jnp.concatenate([rw_hi, (rw - rw_hi.astype(F32)).astype(BF16)], axis=2)

    def vec(a):
        return a.reshape(DEPTH, 1, -1)

    xf = x.reshape(t, d)
    xb = xf
    for i in range(DEPTH):
        u = _in_proj(xb, w_in_bf, i, 512, 2048, F32)
        u3 = u.reshape(bsz, seq, U_COLS)
        ya, yd = _conv_mixers(u3, conf_dw_w[i], conf_dw_b[i], conf_ln_g[i], conf_ln_b[i], sc_conv_w[i])
        yb = _gla(u3, gla_w_g2[i], gla_b_g2[i], gla_norm_g[i])
        yc = _dilated(u3, bias_tab)
        x1, x1b, x1p, ids, gates = _outproj(i, xf, ya.reshape(t, -1), yb.reshape(t, -1), yc.reshape(t, -1),
                                            yd.reshape(t, -1), vec(mix_scale), w_out_bf, vec(ln1_g), vec(ln1_b),
                                            rw_cat[i], rb[i])
        pos, tile_tab = _plan(ids)
        pos_flat = pos[:, :TOP_K].T.reshape(-1)
        tile_end = tile_tab[:n_tiles, 0]
        n_active = jnp.sum((tile_end < N_EXPERTS).astype(I32)).reshape(1)
        tile_expert = jnp.minimum(tile_end, N_EXPERTS - 1)
        xs = _dispatch(pos_flat, x1p, n_rows)
        ys = _experts(i, tile_expert, n_active, xs, exp_w_gate, exp_w_up, exp_w_down)
        xf, xb = _final(i, pos_flat, x1, x1b, gates, p3, ple_wg_bf, vec(ple_b_gate), ple_wp_bf, vec(ln2_g), vec(ln2_b), ys)
    return xf.reshape(bsz, seq, d)
```

```python
import functools
import math

import jax
import jax.numpy as jnp
from jax import lax
from jax.experimental import pallas as pl
from jax.experimental.pallas import tpu as pltpu

F32 = jnp.float32
BF16 = jnp.bfloat16
I32 = jnp.int32
HIGHEST = lax.Precision.HIGHEST

D_MODEL = 2048
DEPTH = 2
PLE_DIM = 256
W_CONF = 512
W_GLA = 512
W_DIL = 512
W_SC = 512
CONF_KERNEL = 31
GLA_HEADS = 4
GLA_DV = 128
GLA_DK = 64
GLA_RANK = 16
GLA_TAU = 16.0
DIL_HEADS = 8
DIL_HD = 64
DIL_BRANCHES = ((128, 1), (512, 4), (2048, 16))
DIL_BLOCK = 128
REL_BUCKETS = 32
REL_MAX_DIST = 2048
SC_KERNEL = 3
N_GROUPS = 4
EXPERTS_PER_GROUP = 8
N_EXPERTS = 32
TOP_K = 2
D_EXPERT = 256
DN_ALPHA = (2 * DEPTH) ** 0.25
LN_EPS = 1e-5
RMS_EPS = 1e-6
IN_SIZES = (2 * W_CONF, GLA_HEADS * GLA_DK, GLA_HEADS * GLA_DK, W_GLA, GLA_RANK, W_GLA,
            W_DIL, W_DIL, W_DIL, W_SC, W_SC, W_SC)

LANES = 128
SUBLANES = 8
VMEM_LIMIT = 56 * 1024 * 1024

GL_PAD = 512
COL_A = 0
COL_GQ = 1024
COL_GK = 1280
COL_GV = 1536
COL_GL = 2048
COL_GR = 2560
COL_CQ = 3072
COL_CK = 3584
COL_CV = 4096
COL_SB = 4608
COL_SC = 5120
COL_SH = 5632
U_COLS = 6144

MOE_TILE = 512
EXPERT_SPLIT = 1


def _cparams(n_axes):
    return pltpu.CompilerParams(dimension_semantics=("arbitrary",) * n_axes, vmem_limit_bytes=VMEM_LIMIT)


def _layer_norm(x, g, b):
    mu = jnp.mean(x, axis=-1, keepdims=True)
    xc = x - mu
    var = jnp.mean(xc * xc, axis=-1, keepdims=True)
    return xc * lax.rsqrt(var + LN_EPS) * g + b


U32 = jnp.uint32


def _pack_bf16_pairs(x):
    n = x.shape[1] // 2
    bits = lax.bitcast_convert_type(x.astype(BF16).astype(F32), U32)
    return (bits[:, :n] >> 16) | (bits[:, n:] & jnp.uint32(0xFFFF0000))


def _unpack_bf16_pairs(w):
    lo = lax.bitcast_convert_type(w << 16, F32)
    hi = lax.bitcast_convert_type(w & jnp.uint32(0xFFFF0000), F32)
    return jnp.concatenate([lo, hi], axis=1)


def _store_row_tiles(ref, packed):
    rows, width = packed.shape
    assert width == SUBLANES * LANES
    for s in range(SUBLANES):
        ref[pl.ds(s, rows, stride=SUBLANES), :] = packed[:, s * LANES:(s + 1) * LANES]


def _load_row_tiles(ref, rows):
    return jnp.concatenate([ref[pl.ds(s, rows, stride=SUBLANES), :] for s in range(SUBLANES)], axis=1)


def _sigmoid(x):
    return 1.0 / (1.0 + jnp.exp(-x))


def _silu(x):
    return x * _sigmoid(x)


def _matmul_kernel(x_ref, w_ref, o_ref):
    o_ref[...] = jnp.dot(x_ref[...].astype(BF16), w_ref[0],
                         preferred_element_type=F32).astype(o_ref.dtype)


def _in_proj(x, w_all, layer, tm, tn, out_dtype):
    m, k = x.shape
    n = w_all.shape[2]
    return pl.pallas_call(
        _matmul_kernel,
        grid=(n // tn, m // tm),
        in_specs=[pl.BlockSpec((tm, k), lambda j, i: (i, 0)),
                  pl.BlockSpec((1, k, tn), lambda j, i: (layer, 0, j))],
        out_specs=pl.BlockSpec((tm, tn), lambda j, i: (i, j)),
        out_shape=jax.ShapeDtypeStruct((m, n), out_dtype),
        compiler_params=_cparams(2),
        name="in_proj",
    )(x, w_all)


CONV_TS = 512
CONV_RC = 64
CONF_HIST = 32
SC_HIST = 8


def _conv_kernel(a_ref, gate_ref, sb_ref, sc_ref, sh_ref, cw_ref, cb_ref, lg_ref, lb_ref, sw_ref,
                 ya_ref, yd_ref, ha, hd, hs):
    s = pl.program_id(1)
    ts = a_ref.shape[1]
    n_ext = CONF_HIST + ts

    @pl.when(s == 0)
    def _():
        ha[0:CONF_HIST, :] = jnp.zeros((CONF_HIST, W_CONF), F32)
        hd[0:SC_HIST, :] = jnp.zeros((SC_HIST, W_SC), F32)

    ha[CONF_HIST:CONF_HIST + ts, :] = a_ref[0] * _sigmoid(gate_ref[0])
    hd[SC_HIST:SC_HIST + ts, :] = sc_ref[0] * sh_ref[0]

    cb = cb_ref[...]
    lg = lg_ref[...]
    lb = lb_ref[...]
    off_a = CONF_HIST - (CONF_KERNEL - 1)
    h_all = ha[...]
    for sh in range(1, SUBLANES):
        h_all = pltpu.roll(h_all, n_ext - 1, 0)
        hs[sh - 1] = h_all
    for c in range(ts // CONV_RC):
        r0 = c * CONV_RC
        acc = jnp.zeros((CONV_RC, W_CONF), F32)
        for j in range(CONF_KERNEL):
            sh = (off_a + j) % SUBLANES
            lo = r0 + off_a + j - sh
            rows = ha[lo:lo + CONV_RC, :] if sh == 0 else hs[sh - 1, lo:lo + CONV_RC, :]
            acc = acc + cw_ref[j:j + 1, :] * rows
        hh = _layer_norm(acc + cb, lg, lb)
        ya_ref[0, r0:r0 + CONV_RC, :] = _silu(hh)

    off_d = SC_HIST - (SC_KERNEL - 1)
    accd = jnp.zeros((ts, W_SC), F32)
    for j in range(SC_KERNEL):
        accd = accd + sw_ref[j:j + 1, :] * hd[off_d + j:off_d + j + ts, :]
    yd_ref[0] = sb_ref[0] * accd

    ha[0:CONF_HIST, :] = ha[ts:ts + CONF_HIST, :]
    hd[0:SC_HIST, :] = hd[ts:ts + SC_HIST, :]


def _conv_mixers(u3, conf_w, conf_b, conf_lg, conf_lb, sc_w):
    bsz, seq, _ = u3.shape
    ts = CONV_TS
    wblk = W_CONF

    def col(c):
        return pl.BlockSpec((1, ts, wblk), lambda b, s, c=c: (b, s, c // wblk))

    def full(shape):
        return pl.BlockSpec(shape, lambda b, s: (0,) * len(shape))

    cw = jnp.zeros((32, W_CONF), F32).at[:CONF_KERNEL].set(conf_w)
    sw = jnp.zeros((8, W_SC), F32).at[:SC_KERNEL].set(sc_w)
    out_spec = pl.BlockSpec((1, ts, wblk), lambda b, s: (b, s, 0))
    return pl.pallas_call(
        _conv_kernel,
        grid=(bsz, seq // ts),
        in_specs=[col(COL_A), col(COL_A + W_CONF), col(COL_SB), col(COL_SC), col(COL_SH),
                  full((32, W_CONF)), full((1, W_CONF)), full((1, W_CONF)), full((1, W_CONF)), full((8, W_SC))],
        out_specs=[out_spec, out_spec],
        out_shape=[jax.ShapeDtypeStruct((bsz, seq, W_CONF), F32), jax.ShapeDtypeStruct((bsz, seq, W_SC), F32)],
        scratch_shapes=[pltpu.VMEM((CONF_HIST + ts, W_CONF), F32), pltpu.VMEM((SC_HIST + ts, W_SC), F32),
                        pltpu.VMEM((SUBLANES - 1, CONF_HIST + ts, W_CONF), F32)],
        compiler_params=_cparams(2),
        name="conv_mixers",
    )(u3, u3, u3, u3, u3, cw, conf_b.reshape(1, -1), conf_lg.reshape(1, -1), conf_lb.reshape(1, -1), sw)


GLA_BLK = 128
GLA_SUB = 64
GLA_UNROLL = 8


def _split3(x):
    hi = x.astype(BF16)
    r1 = x - hi.astype(F32)
    mid = r1.astype(BF16)
    lo = (r1 - mid.astype(F32)).astype(BF16)
    return hi, mid, lo


def _gla_kernel(q_ref, k_ref, v_ref, gl_ref, r_ref, wg2h_ref, wg2l_ref, bg2_ref, ng_ref, tril_ref, o_ref, la_ref):
    seq = q_ref.shape[1]
    nblk = seq // GLA_BLK
    nsub = GLA_BLK // GLA_SUB
    assert nsub == 2 and nblk % GLA_UNROLL == 0

    lane = lax.broadcasted_iota(I32, (1, LANES), 1)
    head_mask = [(lane < GLA_DK).astype(F32), (lane >= GLA_DK).astype(F32)]
    row = lax.broadcasted_iota(I32, (GLA_BLK, GLA_BLK), 0)
    col = lax.broadcasted_iota(I32, (GLA_BLK, GLA_BLK), 1)
    sub_shift = GLA_SUB.bit_length() - 1
    diag_mask = ((row >> sub_shift) == (col >> sub_shift)) & (col <= row)
    second_half = lax.broadcasted_iota(I32, (GLA_BLK, 1), 0) >= GLA_SUB
    scale = GLA_DK ** -0.5
    ng = ng_ref[...]
    tril = tril_ref[...]

    gl = gl_ref[0]
    gl_hi = gl.astype(BF16)
    gl_lo = (gl - gl_hi.astype(F32)).astype(BF16)
    wh = wg2h_ref[...]
    z = (jnp.dot(gl_hi, wh, preferred_element_type=F32) + jnp.dot(gl_lo, wh, preferred_element_type=F32)
         + jnp.dot(gl_hi, wg2l_ref[...], preferred_element_type=F32) + bg2_ref[...])
    la_ref[...] = (jnp.minimum(z, 0.0) - jnp.log(1.0 + jnp.exp(-jnp.abs(z)))) * (1.0 / GLA_TAU)

    nt = (((1,), (1,)), ((), ()))

    def body(it, carry):
        grp = range(GLA_UNROLL)
        r0 = [pl.multiple_of((it * GLA_UNROLL + j) * GLA_BLK, GLA_BLK) for j in grp]
        q = [q_ref[0, pl.ds(r, GLA_BLK), :] * scale for r in r0]
        k = [k_ref[0, pl.ds(r, GLA_BLK), :] for r in r0]
        v = [v_ref[0, pl.ds(r, GLA_BLK), :].astype(BF16) for r in r0]
        b = []
        for r in r0:
            acc = None
            for part in _split3(la_ref[pl.ds(r, GLA_BLK), :]):
                t = jnp.dot(tril, part, preferred_element_type=F32)
                acc = t if acc is None else acc + t
            b.append(acc)

        s_in, q_st, kk, b_last = [], [], [], []
        for j in grp:
            b3 = b[j].reshape(nsub, GLA_SUB, LANES)
            mid = jnp.broadcast_to(b3[:, GLA_SUB // 2 - 1:GLA_SUB // 2, :], b3.shape).reshape(GLA_BLK, LANES)
            qd = q[j] * jnp.exp(b[j] - mid)
            kd = (k[j] * jnp.exp(mid - b[j])).astype(BF16)
            e0 = b[j][GLA_SUB - 1:GLA_SUB, :]
            q_off = jnp.where(second_half, q[j] * jnp.exp(jnp.minimum(b[j] - e0, 0.0)), 0.0)
            k_off = jnp.where(second_half, 0.0, k[j] * jnp.exp(jnp.minimum(e0 - b[j], 0.0))).astype(BF16)
            s_in.append((qd, kd, q_off, k_off))
            q_st.append(q[j] * jnp.exp(b[j]))
            b_last.append(b[j][GLA_BLK - 1:GLA_BLK, :])
            kk.append((k[j] * jnp.exp(b_last[j] - b[j])).astype(BF16))

        scores = []
        for j in grp:
            qd, kd, q_off, k_off = s_in[j]
            per_head = []
            for h in range(2):
                hm = head_mask[h]
                s_d = lax.dot_general((qd * hm).astype(BF16), kd, nt, preferred_element_type=F32)
                s_o = lax.dot_general((q_off * hm).astype(BF16), k_off, nt, preferred_element_type=F32)
                per_head.append((jnp.where(diag_mask, s_d, 0.0) + s_o).astype(BF16))
            scores.append(per_head)

        states = []
        st = carry
        for j in grp:
            states.append(st.astype(BF16))
            upd = lax.dot_general(kk[j], v[j], (((0,), (0,)), ((), ())), preferred_element_type=F32)
            decay = jnp.transpose(jnp.broadcast_to(jnp.exp(b_last[j]), (GLA_BLK, LANES)))
            st = st * jnp.concatenate([decay, decay], axis=1) + upd

        for j in grp:
            outs = []
            for h in range(2):
                lhs = jnp.concatenate([scores[j][h], (q_st[j] * head_mask[h]).astype(BF16)], axis=1)
                rhs = jnp.concatenate([v[j][:, h * GLA_DV:(h + 1) * GLA_DV],
                                       states[j][:, h * GLA_DV:(h + 1) * GLA_DV]], axis=0)
                o_h = jnp.dot(lhs, rhs, preferred_element_type=F32)
                outs.append(o_h * lax.rsqrt(jnp.mean(o_h * o_h, axis=-1, keepdims=True) + RMS_EPS))
            o = jnp.concatenate(outs, axis=1) * ng
            o_ref[0, pl.ds(r0[j], GLA_BLK), :] = o * _silu(r_ref[0, pl.ds(r0[j], GLA_BLK), :])
        return st

    lax.fori_loop(0, nblk // GLA_UNROLL, body, jnp.zeros((2 * GLA_DK, 2 * GLA_DV), F32))


def _gla(u3, w_g2, b_g2, norm_g):
    bsz, seq, _ = u3.shape
    npair = GLA_HEADS // 2
    kw = 2 * GLA_DK
    vw = 2 * GLA_DV

    def col(c, w):
        return pl.BlockSpec((1, seq, w), lambda b, p, c=c, w=w: (b, 0, c // w + p))

    wg2 = jnp.zeros((LANES, GLA_HEADS * GLA_DK), F32).at[:GLA_RANK].set(w_g2)
    wg2_hi = wg2.astype(BF16)
    wg2_lo = (wg2 - wg2_hi.astype(F32)).astype(BF16)
    ng = jnp.concatenate([norm_g, norm_g]).reshape(1, vw)
    tril = jnp.tril(jnp.ones((GLA_BLK, GLA_BLK), BF16))
    return pl.pallas_call(
        _gla_kernel,
        grid=(bsz, npair),
        in_specs=[col(COL_GQ, kw), col(COL_GK, kw), col(COL_GV, vw),
                  pl.BlockSpec((1, seq, LANES), lambda b, p: (b, 0, COL_GL // LANES)),
                  col(COL_GR, vw),
                  pl.BlockSpec((LANES, kw), lambda b, p: (0, p)),
                  pl.BlockSpec((LANES, kw), lambda b, p: (0, p)),
                  pl.BlockSpec((1, kw), lambda b, p: (0, p)),
                  pl.BlockSpec((1, vw), lambda b, p: (0, 0)),
                  pl.BlockSpec((GLA_BLK, GLA_BLK), lambda b, p: (0, 0))],
        out_specs=pl.BlockSpec((1, seq, vw), lambda b, p: (b, 0, p)),
        out_shape=jax.ShapeDtypeStruct((bsz, seq, W_GLA), F32),
        scratch_shapes=[pltpu.VMEM((seq, LANES), F32)],
        compiler_params=_cparams(2),
        name="gla",
    )(u3, u3, u3, u3, u3, wg2_hi, wg2_lo, b_g2.reshape(1, -1), ng, tril)


def _t5_bucket(dist):
    max_exact = REL_BUCKETS // 2
    large = max_exact + (jnp.log(jnp.maximum(dist, 1).astype(F32) / max_exact)
                         / math.log(REL_MAX_DIST / max_exact) * (REL_BUCKETS - max_exact)).astype(I32)
    large = jnp.minimum(large, REL_BUCKETS - 1)
    return jnp.where(dist < max_exact, dist, large)


LOG2E = math.log2(math.e)
DIL_STAGE = 4
DIL_GROUP = 8


def _bias_kernel(rel_ref, bucket_ref, o_ref):
    h = pl.program_id(1)
    blk = DIL_BLOCK
    bucket = bucket_ref[0]
    acc = jnp.zeros(bucket.shape, F32)
    for b in range(REL_BUCKETS):
        acc = jnp.where(bucket == b, rel_ref[b, h], acc)
    acc = acc * LOG2E
    qi = lax.broadcasted_iota(I32, (blk, 2 * blk), 0)
    kj = lax.broadcasted_iota(I32, (blk, 2 * blk), 1)
    steps = qi + blk - kj
    in_window = (steps >= 0) & (steps <= blk)
    neg = jnp.float32(-jnp.inf)
    o_ref[0, 0, 0] = jnp.where(in_window, acc, neg)
    o_ref[0, 0, 1] = jnp.where(in_window & (kj >= blk), acc, neg)


def _bias_tables(rel_bias):
    assert all(window // dil == DIL_BLOCK for window, dil in DIL_BRANCHES)
    qi = jnp.arange(DIL_BLOCK)[:, None]
    kj = jnp.arange(2 * DIL_BLOCK)[None, :]
    steps = jnp.maximum(qi + DIL_BLOCK - kj, 0)
    buckets = jnp.stack([_t5_bucket(steps * dil) for _, dil in DIL_BRANCHES]).astype(I32)
    nbr = len(DIL_BRANCHES)
    return pl.pallas_call(
        _bias_kernel,
        grid=(nbr, DIL_HEADS),
        in_specs=[pl.BlockSpec(memory_space=pltpu.SMEM),
                  pl.BlockSpec((1, DIL_BLOCK, 2 * DIL_BLOCK), lambda i, h: (i, 0, 0))],
        out_specs=pl.BlockSpec((1, 1, 2, DIL_BLOCK, 2 * DIL_BLOCK), lambda i, h: (i, h, 0, 0, 0)),
        out_shape=jax.ShapeDtypeStruct((nbr, DIL_HEADS, 2, DIL_BLOCK, 2 * DIL_BLOCK), F32),
        compiler_params=_cparams(2),
        name="rel_bias_tables",
    )(rel_bias, buckets)


def _dil_kernel(q_ref, k_ref, v_ref, bias_ref, o_ref, qs, ks, vs, ob, lb, stage):
    seq = q_ref.shape[1]
    blk = DIL_BLOCK
    scale = DIL_HD ** -0.5 * LOG2E
    lane = lax.broadcasted_iota(I32, (1, LANES), 1)
    first = lane < DIL_HD
    head_mask = [first.astype(F32), 1.0 - first.astype(F32)]

    ks[0:blk, :] = jnp.zeros((blk, LANES), BF16)
    vs[0:blk, :] = jnp.zeros((blk, LANES), BF16)

    for bi, (window, dil) in enumerate(DIL_BRANCHES):
        sub_len = seq // dil
        nb = sub_len // blk
        nb_shift = nb.bit_length() - 1
        for r in range(dil):
            dst = slice(r * sub_len, (r + 1) * sub_len)
            if dil == 1:
                q, k, v = q_ref[0], k_ref[0], v_ref[0]
            elif dil == DIL_STAGE:
                rows = pl.ds(r, sub_len, stride=dil)
                q, k, v = q_ref[0, rows, :], k_ref[0, rows, :], v_ref[0, rows, :]
                stage[0, dst, :] = q
                stage[1, dst, :] = k
                stage[2, dst, :] = v
            else:
                assert dil == DIL_STAGE * DIL_STAGE
                rows = pl.ds((r % DIL_STAGE) * (seq // DIL_STAGE) + r // DIL_STAGE, sub_len, stride=DIL_STAGE)
                q, k, v = stage[0, rows, :], stage[1, rows, :], stage[2, rows, :]
            q = q * scale
            for h in range(2):
                qs[h, dst, :] = (q * head_mask[h]).astype(BF16)
            ks[blk + r * sub_len:blk + (r + 1) * sub_len, :] = k.astype(BF16)
            vs[blk + r * sub_len:blk + (r + 1) * sub_len, :] = v.astype(BF16)

        def body(it, carry, bi=bi, dil=dil, nb=nb, nb_shift=nb_shift):
            grp = range(DIL_GROUP)
            g = [it * DIL_GROUP + gg for gg in grp]
            n = [gi & (nb - 1) for gi in g]
            g0 = [pl.multiple_of(gi * blk, blk) for gi in g]
            kb = [ks[pl.ds(s0, 2 * blk), :] for s0 in g0]
            sc = [[lax.dot_general(qs[h, pl.ds(g0[gg], blk), :], kb[gg], (((1,), (1,)), ((), ())),
                                   preferred_element_type=F32) + bias_ref[bi, h, jnp.where(n[gg] == 0, 1, 0)]
                   for h in range(2)] for gg in grp]
            stats = []
            for gg in grp:
                per_head = []
                for h in range(2):
                    m = jnp.max(sc[gg][h], axis=-1, keepdims=True)
                    per_head.append((jnp.exp2((sc[gg][h] - m).astype(BF16)), m))
                stats.append(per_head)
            ones = jnp.ones((2 * blk, LANES), BF16)
            for gg in grp:
                vb = jnp.concatenate([vs[pl.ds(g0[gg], 2 * blk), :], ones], axis=1)
                o_acc = None
                l_acc = None
                for h in range(2):
                    p, m = stats[gg][h]
                    res = jnp.dot(p, vb, preferred_element_type=F32)
                    l = res[:, LANES:]
                    o_h = res[:, :LANES] * (1.0 / l)
                    lse = m + jnp.log2(l)
                    if h == 0:
                        o_acc, l_acc = o_h, lse
                    else:
                        o_acc = jnp.where(first, o_acc, o_h)
                        l_acc = jnp.where(first, l_acc, lse)
                if dil > 1:
                    dst = pl.ds(n[gg] * (blk * dil) + (g[gg] >> nb_shift), blk, stride=dil)
                else:
                    dst = pl.ds(g0[gg], blk)
                ob[bi, dst, :] = o_acc
                lb[bi, dst, :] = l_acc
            return carry

        lax.fori_loop(0, seq // blk // DIL_GROUP, body, 0)

    nbr = len(DIL_BRANCHES)
    mc = 256
    for c in range(seq // mc):
        rs = slice(c * mc, (c + 1) * mc)
        ls = [lb[i, rs, :] for i in range(nbr)]
        m = functools.reduce(jnp.maximum, ls)
        ws = [jnp.exp2(l - m) for l in ls]
        num = functools.reduce(lambda a, b: a + b, [w * ob[i, rs, :] for i, w in enumerate(ws)])
        den = functools.reduce(lambda a, b: a + b, ws)
        o_ref[0, rs, :] = num / den


def _dilated(u3, bias_tab):
    bsz, seq, _ = u3.shape
    npair = DIL_HEADS // 2
    nbr = len(DIL_BRANCHES)

    def col(c):
        return pl.BlockSpec((1, seq, LANES), lambda b, p, c=c: (b, 0, c // LANES + p))

    return pl.pallas_call(
        _dil_kernel,
        grid=(bsz, npair),
        in_specs=[col(COL_CQ), col(COL_CK), col(COL_CV),
                  pl.BlockSpec((nbr, 2, 2, DIL_BLOCK, 2 * DIL_BLOCK), lambda b, p: (0, p, 0, 0, 0))],
        out_specs=pl.BlockSpec((1, seq, LANES), lambda b, p: (b, 0, p)),
        out_shape=jax.ShapeDtypeStruct((bsz, seq, W_DIL), F32),
        scratch_shapes=[pltpu.VMEM((2, seq, LANES), BF16),
                        pltpu.VMEM((DIL_BLOCK + seq, LANES), BF16),
                        pltpu.VMEM((DIL_BLOCK + seq, LANES), BF16),
                        pltpu.VMEM((nbr, seq, LANES), F32),
                        pltpu.VMEM((nbr, seq, LANES), F32),
                        pltpu.VMEM((3, seq, LANES), F32)],
        compiler_params=_cparams(2),
        name="dilated_attn",
    )(u3, u3, u3, bias_tab)


TOK_TILE = 256
OUT_TILE = 512
OUT_SPLIT = 2
FINAL_SPLIT = 2
GATHER_SLOTS = 3
PLAN_TILE = 1024


def _outproj_kernel(x_ref, ya_ref, yb_ref, yc_ref, yd_ref, ms_ref, wo_ref, g_ref, b_ref, rw_ref, rb_ref,
                    x1_ref, x1b_ref, x1p_ref, ids_ref, gates_ref):
    tm = x_ref.shape[0]
    rows = tm // OUT_SPLIT
    parts = [slice(pi * rows, (pi + 1) * rows) for pi in range(OUT_SPLIT)]
    accs = []
    for rs in parts:
        acc = None
        for gi, y_ref in enumerate((ya_ref, yb_ref, yc_ref, yd_ref)):
            lo = gi * W_CONF
            y = (y_ref[rs, :] * ms_ref[0, :, lo:lo + W_CONF]).astype(BF16)
            part = jnp.dot(y, wo_ref[0, lo:lo + W_CONF, :], preferred_element_type=F32)
            acc = part if acc is None else acc + part
        accs.append(acc)

    logits = []
    for rs, acc in zip(parts, accs):
        x1 = _layer_norm(DN_ALPHA * x_ref[rs, :] + acc, g_ref[0], b_ref[0])
        x1_ref[rs, :] = x1
        x1_hi = x1.astype(BF16)
        x1b_ref[rs, :] = x1_hi
        _store_row_tiles(x1p_ref.at[pl.ds(rs.start * SUBLANES, rows * SUBLANES)], _pack_bf16_pairs(x1))
        x1_lo = (x1 - x1_hi.astype(F32)).astype(BF16)
        hh_hl = jnp.dot(x1_hi, rw_ref[...], preferred_element_type=F32)
        logits.append(hh_hl[:, :LANES] + hh_hl[:, LANES:]
                      + jnp.dot(x1_lo, rw_ref[:, :LANES], preferred_element_type=F32) + rb_ref[...])

    lane = lax.broadcasted_iota(I32, (rows, LANES), 1)
    neg = jnp.float32(-jnp.inf)
    for rs, lg in zip(parts, logits):
        gl = jnp.where(lane < N_GROUPS, lg, neg)
        gmax = jnp.max(gl, axis=-1, keepdims=True)
        g_top = 1.0 / jnp.sum(jnp.exp(gl - gmax), axis=-1, keepdims=True)
        g_idx = jnp.min(jnp.where(gl == gmax, lane, LANES), axis=-1, keepdims=True)
        e_lo = N_GROUPS + g_idx * EXPERTS_PER_GROUP
        el = jnp.where((lane >= e_lo) & (lane < e_lo + EXPERTS_PER_GROUP), lg, neg)
        m1 = jnp.max(el, axis=-1, keepdims=True)
        i1 = jnp.min(jnp.where(el == m1, lane, LANES), axis=-1, keepdims=True)
        el2 = jnp.where(lane == i1, neg, el)
        m2 = jnp.max(el2, axis=-1, keepdims=True)
        i2 = jnp.min(jnp.where(el2 == m2, lane, LANES), axis=-1, keepdims=True)
        e21 = jnp.exp(m2 - m1)
        w1 = g_top / (1.0 + e21)
        w2 = g_top * e21 / (1.0 + e21)
        ids_ref[rs, :] = jnp.where(lane == 0, i1 - N_GROUPS, jnp.where(lane == 1, i2 - N_GROUPS, N_EXPERTS))
        gates_ref[rs, :] = jnp.where(lane == 0, w1, jnp.where(lane == 1, w2, 0.0))


def _outproj(layer, x, ya, yb, yc, yd, mix_scale, w_out_bf, ln_g, ln_b, rw_cat, rb):
    t, d = x.shape
    tm = OUT_TILE

    def row(w):
        return pl.BlockSpec((tm, w), lambda i: (i, 0))

    def full(shape):
        return pl.BlockSpec(shape, lambda i: (0,) * len(shape))

    def per_layer(shape):
        return pl.BlockSpec((1,) + shape, lambda i: (layer,) + (0,) * len(shape))

    return pl.pallas_call(
        _outproj_kernel,
        grid=(t // tm,),
        in_specs=[row(d), row(W_CONF), row(W_GLA), row(W_DIL), row(W_SC), per_layer((1, d)), per_layer((d, d)),
                  per_layer((1, d)), per_layer((1, d)), full((d, 2 * LANES)), full((1, LANES))],
        out_specs=[row(d), row(d), pl.BlockSpec((tm * SUBLANES, LANES), lambda i: (i, 0)), row(LANES), row(LANES)],
        out_shape=[jax.ShapeDtypeStruct((t, d), F32), jax.ShapeDtypeStruct((t, d), BF16),
                   jax.ShapeDtypeStruct((t * SUBLANES, LANES), U32),
                   jax.ShapeDtypeStruct((t, LANES), I32), jax.ShapeDtypeStruct((t, LANES), F32)],
        compiler_params=_cparams(1),
        name="out_proj_ln_router",
    )(x, ya, yb, yc, yd, mix_scale, w_out_bf, ln_g, ln_b, rw_cat, rb)


def _plan_kernel(ids_ref, tril_ref, upper_ref, pos_ref, tile_ref, cnt, carry, start):
    phase = pl.program_id(0)
    i = pl.program_id(1)
    tm = ids_ref.shape[0]
    lane = lax.broadcasted_iota(I32, (tm, LANES), 1)
    ids = ids_ref[...]
    e1 = ids[:, 0:1]
    e2 = ids[:, 1:2]
    oh1 = (lane == e1).astype(F32)
    oh2 = (lane == e2).astype(F32)
    oh = oh1 + oh2

    @pl.when((phase == 0) & (i == 0))
    def _():
        cnt[...] = jnp.zeros(cnt.shape, F32)

    @pl.when(phase == 0)
    def _():
        cnt[...] = cnt[...] + jnp.sum(oh, axis=0, keepdims=True)

    @pl.when((phase == 1) & (i == 0))
    def _():
        padded = jnp.floor((cnt[...] + (MOE_TILE - 1)) * (1.0 / MOE_TILE)) * MOE_TILE
        p8 = jnp.broadcast_to(padded, (SUBLANES, LANES))
        st = jnp.dot(p8, upper_ref[...], precision=HIGHEST, preferred_element_type=F32)
        start[...] = st[0:1, :]
        carry[...] = jnp.zeros(carry.shape, F32)
        end = st[0:1, :] + padded
        trow = lax.broadcasted_iota(I32, (LANES, LANES), 0).astype(F32) * MOE_TILE
        tlane = lax.broadcasted_iota(I32, (LANES, LANES), 1)
        done = jnp.where((end <= trow) & (tlane < N_EXPERTS), 1.0, 0.0)
        tile_ref[...] = jnp.broadcast_to(jnp.sum(done, axis=-1, keepdims=True), (LANES, LANES)).astype(I32)

    @pl.when(phase == 1)
    def _():
        incl = jnp.dot(tril_ref[...], oh.astype(BF16), preferred_element_type=F32) + carry[...]
        base = start[...] + incl - oh
        p1 = jnp.sum(oh1 * base, axis=-1, keepdims=True)
        p2 = jnp.sum(oh2 * base, axis=-1, keepdims=True)
        pos_ref[...] = jnp.where(lane == 0, p1, jnp.where(lane == 1, p2, 0.0)).astype(I32)
        carry[...] = carry[...] + jnp.sum(oh, axis=0, keepdims=True)


def _plan(ids):
    t = ids.shape[0]
    tm = PLAN_TILE
    tril = jnp.tril(jnp.ones((tm, tm), BF16))
    upper = jnp.triu(jnp.ones((LANES, LANES), F32), k=1)
    return pl.pallas_call(
        _plan_kernel,
        grid=(2, t // tm),
        in_specs=[pl.BlockSpec((tm, LANES), lambda ph, i: (i, 0)),
                  pl.BlockSpec((tm, tm), lambda ph, i: (0, 0)),
                  pl.BlockSpec((LANES, LANES), lambda ph, i: (0, 0))],
        out_specs=[pl.BlockSpec((tm, LANES), lambda ph, i: (i * ph, 0)),
                   pl.BlockSpec((LANES, LANES), lambda ph, i: (0, 0))],
        out_shape=[jax.ShapeDtypeStruct((t, LANES), I32), jax.ShapeDtypeStruct((LANES, LANES), I32)],
        scratch_shapes=[pltpu.VMEM((1, LANES), F32), pltpu.VMEM((1, LANES), F32), pltpu.VMEM((1, LANES), F32)],
        compiler_params=_cparams(2),
        name="route_plan",
    )(ids, tril, upper)


DISPATCH_TILE = 512
DISPATCH_SLOTS = 3


def _dispatch_kernel(pos_ref, x_ref, xs_ref, stage, sem):
    i = pl.program_id(0)
    n_steps = pl.num_programs(0)
    tm = x_ref.shape[0] // SUBLANES
    t = n_steps * tm
    slot = lax.rem(i, DISPATCH_SLOTS)

    def wait_slot(s):
        for _ in range(TOP_K):
            pltpu.make_async_copy(stage.at[s], xs_ref.at[pl.ds(0, tm * SUBLANES)], sem.at[s]).wait()

    @pl.when(i >= DISPATCH_SLOTS)
    def _():
        wait_slot(slot)

    stage[slot] = x_ref[...]
    for m in range(tm):
        for k in range(TOP_K):
            dst = pl.multiple_of(pos_ref[k * t + i * tm + m] * SUBLANES, SUBLANES)
            pltpu.make_async_copy(stage.at[slot, pl.ds(m * SUBLANES, SUBLANES)], xs_ref.at[pl.ds(dst, SUBLANES)],
                                  sem.at[slot]).start(priority=k)

    @pl.when(i == n_steps - 1)
    def _():
        for s in range(DISPATCH_SLOTS):
            wait_slot(s)


def _dispatch(pos_flat, x1p, n_rows):
    t = x1p.shape[0] // SUBLANES
    tm = DISPATCH_TILE
    assert t // tm >= DISPATCH_SLOTS
    return pl.pallas_call(
        _dispatch_kernel,
        grid_spec=pltpu.PrefetchScalarGridSpec(
            num_scalar_prefetch=1,
            grid=(t // tm,),
            in_specs=[pl.BlockSpec((tm * SUBLANES, LANES), lambda i, pos: (i, 0))],
            out_specs=pl.BlockSpec(memory_space=pl.ANY),
            scratch_shapes=[pltpu.VMEM((DISPATCH_SLOTS, tm * SUBLANES, LANES), x1p.dtype),
                            pltpu.SemaphoreType.DMA((DISPATCH_SLOTS,))],
        ),
        out_shape=jax.ShapeDtypeStruct((n_rows * SUBLANES, LANES), x1p.dtype),
        compiler_params=_cparams(1),
        name="moe_dispatch",
    )(pos_flat, x1p)


def _expert_kernel(layer, te_ref, na_ref, first_ref, run_ref, next2_ref, runs_ref, nruns_ref,
                   xs_ref, wg_hbm, wu_hbm, wd_hbm, ys_ref, wg_buf, wu_buf, wd_buf, wg_bf, wu_bf, wd_bf, sem):
    i = pl.program_id(0)

    def copies(e, s):
        return (pltpu.make_async_copy(wg_hbm.at[layer, e], wg_buf.at[s], sem.at[s]),
                pltpu.make_async_copy(wu_hbm.at[layer, e], wu_buf.at[s], sem.at[s]),
                pltpu.make_async_copy(wd_hbm.at[layer, e], wd_buf.at[s], sem.at[s]))

    def fetch(e, s):
        for c in copies(e, s):
            c.start()

    @pl.when(i == 0)
    def _():
        fetch(runs_ref[0], 0)

        @pl.when(nruns_ref[0] > 1)
        def _():
            fetch(runs_ref[1], 1)

    active = i < na_ref[0]

    @pl.when(active & (first_ref[i] == 1))
    def _():
        s = run_ref[i] % 2
        for c in copies(te_ref[i], s):
            c.wait()
        wg_bf[...] = wg_buf[s].astype(BF16)
        wu_bf[...] = wu_buf[s].astype(BF16)
        wd_bf[...] = wd_buf[s].astype(BF16)

        @pl.when(next2_ref[i] >= 0)
        def _():
            fetch(next2_ref[i], s)

    @pl.when(active)
    def _():
        rows = MOE_TILE // EXPERT_SPLIT
        views = [pl.ds(pi * rows * SUBLANES, rows * SUBLANES) for pi in range(EXPERT_SPLIT)]
        xb = [_unpack_bf16_pairs(_load_row_tiles(xs_ref.at[v], rows)).astype(BF16) for v in views]
        gu = [(jnp.dot(x, wg_bf[...], preferred_element_type=F32), jnp.dot(x, wu_bf[...], preferred_element_type=F32))
              for x in xb]
        h = [(_silu(g) * u).astype(BF16) for g, u in gu]
        y = [jnp.dot(hh, wd_bf[...], preferred_element_type=F32) for hh in h]
        for v, yy in zip(views, y):
            _store_row_tiles(ys_ref.at[v], _pack_bf16_pairs(yy))


def _experts(layer, tile_expert, n_active, xs, w_gate, w_up, w_down):
    n_rows = xs.shape[0] // SUBLANES
    d = LANES
    n_tiles = n_rows // MOE_TILE
    d_model = w_gate.shape[2]

    tile = jnp.arange(n_tiles, dtype=I32)
    active = tile < n_active[0]
    first = active & ((tile == 0) | (tile_expert != jnp.roll(tile_expert, 1)))
    run_of_tile = jnp.cumsum(first.astype(I32)) - 1
    n_runs = jnp.sum(first.astype(I32)).reshape(1)
    is_run_start = first[None, :] & (run_of_tile[None, :] == tile[:, None])
    run_expert = jnp.sum(jnp.where(is_run_start, tile_expert[None, :], 0), axis=1).astype(I32)
    nxt = run_of_tile + 2
    next2 = jnp.where(nxt < n_runs[0], run_expert[jnp.minimum(nxt, n_tiles - 1)], -1).astype(I32)
    first = first.astype(I32)

    def rows(i, te, na, *_):
        return (jnp.minimum(i, na[0] - 1), 0)

    any_spec = pl.BlockSpec(memory_space=pl.ANY)
    return pl.pallas_call(
        functools.partial(_expert_kernel, layer),
        grid_spec=pltpu.PrefetchScalarGridSpec(
            num_scalar_prefetch=7,
            grid=(n_tiles,),
            in_specs=[pl.BlockSpec((MOE_TILE * SUBLANES, d), rows), any_spec, any_spec, any_spec],
            out_specs=pl.BlockSpec((MOE_TILE * SUBLANES, d), rows),
            scratch_shapes=[pltpu.VMEM((2, d_model, D_EXPERT), F32), pltpu.VMEM((2, d_model, D_EXPERT), F32),
                            pltpu.VMEM((2, D_EXPERT, d_model), F32),
                            pltpu.VMEM((d_model, D_EXPERT), BF16), pltpu.VMEM((d_model, D_EXPERT), BF16),
                            pltpu.VMEM((D_EXPERT, d_model), BF16),
                            pltpu.SemaphoreType.DMA((2,))],
        ),
        out_shape=jax.ShapeDtypeStruct((n_rows * SUBLANES, d), U32),
        compiler_params=_cparams(1),
        name="moe_experts",
    )(tile_expert, n_active, first, run_of_tile, next2, run_expert, n_runs, xs, w_gate, w_up, w_down)


def _final_kernel(pos_ref, x1_ref, x1b_ref, gates_ref, p_ref, wg_ref, bg_ref, wp_ref, g_ref, b_ref, ys_ref,
                  x2_ref, x2b_ref, ybuf, sem):
    i = pl.program_id(0)
    n_steps = pl.num_programs(0)
    tm = x1_ref.shape[0]
    t = n_steps * tm

    def gather(step, slot):
        for m in range(tm):
            for k in range(TOP_K):
                src = pl.multiple_of(pos_ref[k * t + step * tm + m] * SUBLANES, SUBLANES)
                pltpu.make_async_copy(ys_ref.at[pl.ds(src, SUBLANES)],
                                      ybuf.at[slot, k, pl.ds(m * SUBLANES, SUBLANES)],
                                      sem.at[slot]).start(priority=k)

    def wait_slot(slot):
        for k in range(TOP_K):
            pltpu.make_async_copy(ys_ref.at[pl.ds(0, tm * SUBLANES)], ybuf.at[slot, k], sem.at[slot]).wait()

    @pl.when(i == 0)
    def _():
        gather(0, 0)
        gather(1, 1)

    slot = lax.rem(i, GATHER_SLOTS)
    wait_slot(slot)
    gather(jnp.minimum(i + 2, n_steps - 1), lax.rem(i + 2, GATHER_SLOTS))

    rows = tm // FINAL_SPLIT
    parts = [slice(pi * rows, (pi + 1) * rows) for pi in range(FINAL_SPLIT)]
    ples = []
    for rs in parts:
        gate = _sigmoid(jnp.dot(x1b_ref[rs, :], wg_ref[0], preferred_element_type=F32) + bg_ref[0])
        ples.append(gate * jnp.dot(p_ref[0, rs, :].astype(BF16), wp_ref[0], preferred_element_type=F32))
    for rs, ple in zip(parts, ples):
        gates = gates_ref[rs, :]
        tiles = pl.ds(rs.start * SUBLANES, rows * SUBLANES)
        ffn = (gates[:, 0:1] * _unpack_bf16_pairs(_load_row_tiles(ybuf.at[slot, 0, tiles], rows))
               + gates[:, 1:2] * _unpack_bf16_pairs(_load_row_tiles(ybuf.at[slot, 1, tiles], rows)))
        x2 = _layer_norm(DN_ALPHA * x1_ref[rs, :] + ffn + ple, g_ref[0], b_ref[0])
        x2_ref[rs, :] = x2
        x2b_ref[rs, :] = x2.astype(BF16)

    @pl.when(i == n_steps - 1)
    def _():
        wait_slot(lax.rem(i + 1, GATHER_SLOTS))
        wait_slot(lax.rem(i + 2, GATHER_SLOTS))


def _final(layer, pos_flat, x1, x1b, gates, p, wg_bf, bg, wp_bf, ln_g, ln_b, ys):
    t, d = x1.shape
    tm = TOK_TILE

    def row(w):
        return pl.BlockSpec((tm, w), lambda i, pos: (i, 0))

    def per_layer(shape):
        return pl.BlockSpec((1,) + shape, lambda i, pos: (layer,) + (0,) * len(shape))

    return pl.pallas_call(
        _final_kernel,
        grid_spec=pltpu.PrefetchScalarGridSpec(
            num_scalar_prefetch=1,
            grid=(t // tm,),
            in_specs=[row(d), row(d), row(LANES), pl.BlockSpec((1, tm, PLE_DIM), lambda i, pos: (layer, i, 0)),
                      per_layer((d, d)), per_layer((1, d)), per_layer((PLE_DIM, d)),
                      per_layer((1, d)), per_layer((1, d)), pl.BlockSpec(memory_space=pl.ANY)],
            out_specs=[row(d), row(d)],
            scratch_shapes=[pltpu.VMEM((GATHER_SLOTS, TOP_K, tm * SUBLANES, LANES), U32),
                            pltpu.SemaphoreType.DMA((GATHER_SLOTS,))],
        ),
        out_shape=[jax.ShapeDtypeStruct((t, d), F32), jax.ShapeDtypeStruct((t, d), BF16)],
        compiler_params=_cparams(1),
        name="combine_ple_ln",
    )(pos_flat, x1, x1b, gates, p, wg_bf, bg, wp_bf, ln_g, ln_b, ys)


W_PREP_COLS = 512


def _arrange_kernel(wt_ref, o_ref):
    j = pl.program_id(1)
    wt = wt_ref[0]
    row = lax.broadcasted_iota(I32, (wt.shape[0], 1), 0)
    n_valid = jnp.where(j == COL_GL // W_PREP_COLS, GLA_RANK, W_PREP_COLS)
    o_ref[0] = jnp.where(row < n_valid, wt, 0.0).T.astype(BF16)


def _arrange_w_in(w):
    n_layer, k, n_in = w.shape
    assert sum(IN_SIZES[:5]) == COL_GL + GLA_RANK and n_in + GL_PAD - GLA_RANK == U_COLS
    assert COL_GL % W_PREP_COLS == 0 and GL_PAD == W_PREP_COLS
    gl_blk = COL_GL // W_PREP_COLS

    def src_row(l, j):
        row = jnp.where(j <= gl_blk, j * W_PREP_COLS, j * W_PREP_COLS - (GL_PAD - GLA_RANK))
        return (l, pl.multiple_of(row, GLA_RANK), 0)

    return pl.pallas_call(
        _arrange_kernel,
        grid=(n_layer, U_COLS // W_PREP_COLS),
        in_specs=[pl.BlockSpec((pl.Element(1), pl.Element(W_PREP_COLS), pl.Element(k)), src_row)],
        out_specs=pl.BlockSpec((1, k, W_PREP_COLS), lambda l, j: (l, 0, j)),
        out_shape=jax.ShapeDtypeStruct((n_layer, k, U_COLS), BF16),
        compiler_params=_cparams(2),
        name="arrange_w_in",
    )(jnp.swapaxes(w, 1, 2))


def kernel(x, p, w_in, conf_dw_w, conf_dw_b, conf_ln_g, conf_ln_b, gla_w_g2, gla_b_g2, gla_norm_g, sc_conv_w, mix_scale, w_out, rel_bias, ln1_g, ln1_b, router_g_w, router_g_b, router_e_w, router_e_b, exp_w_gate, exp_w_up, exp_w_down, ple_w_gate, ple_b_gate, ple_w_proj, ln2_g, ln2_b):
    bsz, seq, d = x.shape
    t = bsz * seq
    n_rows = TOP_K * t + N_EXPERTS * MOE_TILE
    n_tiles = n_rows // MOE_TILE
    bias_tab = _bias_tables(rel_bias)
    w_in_bf = _arrange_w_in(w_in)
    w_out_bf = w_out.astype(BF16)
    ple_wg_bf = ple_w_gate.astype(BF16)
    ple_wp_bf = ple_w_proj.astype(BF16)
    p3 = p.reshape(DEPTH, t, PLE_DIM)
    n_route = N_GROUPS + N_EXPERTS
    rw = jnp.concatenate([router_g_w, router_e_w, jnp.zeros((DEPTH, d, LANES - n_route), F32)], axis=2)
    rb = jnp.concatenate([router_g_b, router_e_b, jnp.zeros((DEPTH, LANES - n_route), F32)], axis=1)[:, None, :]
    rw_hi = rw.astype(BF16)
    rw_cat = jnp.concatenate([rw_hi, (rw - rw_hi.astype(F32)).astype(BF16)], axis=2)

    def vec(a):
        return a.reshape(DEPTH, 1, -1)

    xf = x.reshape(t, d)
    xb = xf
    for i in range(DEPTH):
        u = _in_proj(xb, w_in_bf, i, 512, 2048, F32)
        u3 = u.reshape(bsz, seq, U_COLS)
        ya, yd = _conv_mixers(u3, conf_dw_w[i], conf_dw_b[i], conf_ln_g[i], conf_ln_b[i], sc_conv_w[i])
        yb = _gla(u3, gla_w_g2[i], gla_b_g2[i], gla_norm_g[i])
        yc = _dilated(u3, bias_tab)
        x1, x1b, x1p, ids, gates = _outproj(i, xf, ya.reshape(t, -1), yb.reshape(t, -1), yc.reshape(t, -1),
                                            yd.reshape(t, -1), vec(mix_scale), w_out_bf, vec(ln1_g), vec(ln1_b),
                                            rw_cat[i], rb[i])
        pos, tile_tab = _plan(ids)
        pos_flat = pos[:, :TOP_K].T.reshape(-1)
        tile_end = tile_tab[:n_tiles, 0]
        n_active = jnp.sum((tile_end < N_EXPERTS).astype(I32)).reshape(1)
        tile_expert = jnp.minimum(tile_end, N_EXPERTS - 1)
        xs = _dispatch(pos_flat, x1p, n_rows)
        ys = _experts(i, tile_expert, n_active, xs, exp_w_gate, exp_w_up, exp_w_down)
        xf, xb = _final(i, pos_flat, x1, x1b, gates, p3, ple_wg_bf, vec(ple_b_gate), ple_wp_bf, vec(ln2_g), vec(ln2_b), ys)
    return xf.reshape(bsz, seq, d)
```
